```python
import math
import jax, jax.numpy as jnp
from jax import lax
import numpy as np

D_MODEL = 4096
BATCH = 4
SEQ = 2048
DEPTH = 1
DEC_BATCH = 128
DEC_SEQ = 8
PAST_LEN = 16384
PAGE_SIZE = 128

POOL_GROUPS = 4
POOL_WINDOWS = (2, 4, 8, 16)
POOL_IN = D_MODEL // 2
POOL_GIN = POOL_IN // POOL_GROUPS
POOL_GOUT = D_MODEL // POOL_GROUPS
POOL_BUF = 15
M_HEADS = 8
M_DK = D_MODEL // 16
M_DV = D_MODEL // M_HEADS
M_QK = M_HEADS * M_DK
M_V = M_HEADS * M_DV
M_CHUNK = 64
N_GROUPS = 8
EXP_PER_GROUP = 8
N_EXPERTS = N_GROUPS * EXP_PER_GROUP
TOP_K = 2
D_EXPERT = D_MODEL // 4
MOE_BLOCK = 128
EPS = 1e-6
IN_SIZES = (POOL_IN, M_QK, M_QK, M_V, M_V, D_MODEL, D_MODEL, M_HEADS, M_HEADS)
D_IN = POOL_IN + 2 * M_QK + 2 * M_V + 2 * D_MODEL + 2 * M_HEADS

kernel_name = "hybrid_pool_mlstm_hmoe_step"


def rmsnorm(x, w):
    xf = x.astype(jnp.float32)
    r = lax.rsqrt(jnp.mean(xf * xf, axis=-1, keepdims=True) + EPS)
    return (xf * r).astype(x.dtype) * w


def pool_mix(p, buf, pos0, w_pool, pool_scale):
    B, T, _ = p.shape
    ext = jnp.concatenate([buf, p], axis=1)
    cs = jnp.cumsum(ext.astype(jnp.float32), axis=1)
    cs = jnp.concatenate([jnp.zeros_like(cs[:, :1]), cs], axis=1)
    pos = pos0 + jnp.arange(T)
    outs = []
    for g, w in enumerate(POOL_WINDOWS):
        sl = slice(g * POOL_GIN, (g + 1) * POOL_GIN)
        s = cs[:, POOL_BUF + 1:, sl] - cs[:, POOL_BUF + 1 - w:POOL_BUF + 1 - w + T, sl]
        cnt = jnp.minimum(pos + 1, w).astype(jnp.float32)[None, :, None]
        outs.append(s / cnt)
    mean = jnp.concatenate(outs, axis=-1)
    d = (mean - p.astype(jnp.float32)).astype(p.dtype).reshape(B, T, POOL_GROUPS, POOL_GIN)
    y = jnp.einsum('btgi,gio->btgo', d, w_pool).reshape(B, T, D_MODEL)
    return y * pool_scale, ext[:, -POOL_BUF:]


def mlstm_chunk(carry, inp):
    C, n, m = carry
    q, k, v, li, lf = inp
    L = q.shape[2]
    b = jnp.cumsum(lf, axis=-1)
    causal = jnp.tril(jnp.ones((L, L), dtype=bool))
    dmat = jnp.where(causal, b[..., :, None] - b[..., None, :] + li[..., None, :], -jnp.inf)
    inter = b + m[..., None]
    m_t = jnp.maximum(inter, jnp.max(dmat, axis=-1))
    dw = jnp.exp(dmat - m_t[..., None])
    iw = jnp.exp(inter - m_t)
    s = jnp.einsum('bhtd,bhsd->bhts', q, k) * dw
    num = jnp.einsum('bhts,bhsv->bhtv', s, v) + iw[..., None] * jnp.einsum('bhtd,bhdv->bhtv', q, C)
    den = jnp.sum(s, axis=-1) + iw * jnp.einsum('bhtd,bhd->bht', q, n)
    h = num / jnp.maximum(jnp.abs(den), jnp.exp(-m_t))[..., None]
    b_last = b[..., -1]
    dl = b_last[..., None] - b + li
    m_new = jnp.maximum(b_last + m, jnp.max(dl, axis=-1))
    wk = jnp.exp(dl - m_new[..., None])
    dec = jnp.exp(b_last + m - m_new)
    C_new = dec[..., None, None] * C + jnp.einsum('bhs,bhsd,bhsv->bhdv', wk, k, v)
    n_new = dec[..., None] * n + jnp.einsum('bhs,bhsd->bhd', wk, k)
    return (C_new, n_new, m_new), h


def mlstm(q, k, v, li, lf, C0, n0, m0):
    B, H, T, _ = q.shape
    chunk = math.gcd(T, M_CHUNK)
    nc = T // chunk

    def to_chunks(a):
        return jnp.moveaxis(a.reshape(a.shape[:2] + (nc, chunk) + a.shape[3:]), 2, 0)

    (C, n, m), h = lax.scan(mlstm_chunk, (C0, n0, m0),
                            (to_chunks(q), to_chunks(k), to_chunks(v), to_chunks(li), to_chunks(lf)))
    h = jnp.moveaxis(h, 0, 2).reshape(B, H, T, M_DV)
    return h, C, n, m


def hier_moe(u, w_rg, w_re, w_g, w_u, w_d):
    N, D = u.shape
    gl = (u @ w_rg).astype(jnp.float32)
    gp = jax.nn.softmax(gl, axis=-1)
    grp = jnp.argmax(gl, axis=-1)
    p_grp = jnp.take_along_axis(gp, grp[:, None], axis=1)
    el = (u @ w_re).astype(jnp.float32).reshape(N, N_GROUPS, EXP_PER_GROUP)
    el = jnp.take_along_axis(el, grp[:, None, None], axis=1)[:, 0]
    tv, ti = lax.top_k(el, TOP_K)
    wts = jax.nn.softmax(tv, axis=-1) * p_grp
    eid = grp[:, None] * EXP_PER_GROUP + ti
    A = N * TOP_K
    flat_e = eid.reshape(A).astype(jnp.int32)
    flat_t = jnp.repeat(jnp.arange(N, dtype=jnp.int32), TOP_K)
    flat_w = wts.reshape(A)
    order = jnp.argsort(flat_e, stable=True)
    e_s, t_s, w_s = flat_e[order], flat_t[order], flat_w[order]
    counts = jnp.bincount(flat_e, length=N_EXPERTS)
    start = jnp.cumsum(counts) - counts
    padded = (counts + MOE_BLOCK - 1) // MOE_BLOCK * MOE_BLOCK
    pend = jnp.cumsum(padded)
    pstart = pend - padded
    dest = pstart[e_s] + (jnp.arange(A) - start[e_s])
    n_blocks = -(-A // MOE_BLOCK) + N_EXPERTS
    P = n_blocks * MOE_BLOCK
    buf_t = jnp.full((P,), N, dtype=jnp.int32).at[dest].set(t_s)
    buf_w = jnp.zeros((P,), dtype=flat_w.dtype).at[dest].set(w_s)
    blk_start = jnp.arange(n_blocks) * MOE_BLOCK
    blk_e = jnp.minimum(jnp.searchsorted(pend, blk_start, side='right'), N_EXPERTS - 1)
    blk_used = blk_start < pend[-1]
    u_pad = jnp.concatenate([u, jnp.zeros((1, D), u.dtype)], axis=0)

    def run_block(args):
        tok, e, used = args

        def compute(_):
            xb = u_pad[tok]
            hdn = jax.nn.silu(xb @ w_g[e]) * (xb @ w_u[e])
            return hdn @ w_d[e]

        return lax.cond(used, compute, lambda _: jnp.zeros((MOE_BLOCK, D), u.dtype), None)

    yb = lax.map(run_block, (buf_t.reshape(n_blocks, MOE_BLOCK), blk_e, blk_used))
    y = yb.reshape(P, D) * buf_w[:, None].astype(u.dtype)
    return jax.ops.segment_sum(y, buf_t, num_segments=N + 1)[:N]


def trunk_layer(x, c, pos0, pool_buf, C0, n0, m0, w_ada, b_ada, norm_mix_w, w_in, b_igate, b_fgate,
                w_pool, pool_scale, mlstm_norm_w, w_out, norm_ffn_w, w_router_group, w_router_expert,
                w_exp_gate, w_exp_up, w_exp_down):
    B, T, _ = x.shape
    f32 = jnp.float32
    mod = jnp.einsum('bd,de->be', jax.nn.silu(c), w_ada) + b_ada
    sh1, sc1, g1, sh2, sc2, g2 = jnp.split(mod[:, None, :], 6, axis=-1)
    u = rmsnorm(x, norm_mix_w) * (1 + sc1) + sh1
    z = u @ w_in
    offs, acc = [], 0
    for s in IN_SIZES[:-1]:
        acc += s
        offs.append(acc)
    p, q, k, v, o, ga, gb, ip, fp = jnp.split(z, offs, axis=-1)
    pool_out, pool_new = pool_mix(p, pool_buf, pos0, w_pool, pool_scale)
    qh = (q.reshape(B, T, M_HEADS, M_DK).transpose(0, 2, 1, 3) * (M_DK ** -0.5)).astype(f32)
    kh = k.reshape(B, T, M_HEADS, M_DK).transpose(0, 2, 1, 3).astype(f32)
    vh = v.reshape(B, T, M_HEADS, M_DV).transpose(0, 2, 1, 3).astype(f32)
    li = (ip + b_igate).astype(f32).transpose(0, 2, 1)
    lf = jax.nn.log_sigmoid((fp + b_fgate).astype(f32)).transpose(0, 2, 1)
    h, C, n, m = mlstm(qh, kh, vh, li, lf, C0, n0, m0)
    h = rmsnorm(h.transpose(0, 2, 1, 3), mlstm_norm_w).astype(x.dtype)
    mlstm_out = jax.nn.sigmoid(o) * h.reshape(B, T, M_V)
    merged = jax.nn.sigmoid(ga) * pool_out + jax.nn.sigmoid(gb) * mlstm_out
    x = x + g1 * (merged @ w_out)
    u2 = rmsnorm(x, norm_ffn_w) * (1 + sc2) + sh2
    ff = hier_moe(u2.reshape(B * T, D_MODEL), w_router_group, w_router_expert,
                  w_exp_gate, w_exp_up, w_exp_down).reshape(B, T, D_MODEL)
    x = x + g2 * ff
    return x, pool_new, C, n, m


def setup_inputs(seed: int = 0) -> dict:
    key = jax.random.key(seed)
    ks = jax.random.split(key, 32)
    nrm = jax.random.normal
    L = DEPTH
    sD = D_MODEL ** -0.5
    return {
        "x_prompt": nrm(ks[0], (BATCH, SEQ, D_MODEL), jnp.float32),
        "x_sample": nrm(ks[1], (DEC_BATCH, DEC_SEQ, D_MODEL), jnp.float32),
        "c_prompt": nrm(ks[2], (BATCH, D_MODEL), jnp.float32),
        "c_sample": nrm(ks[3], (DEC_BATCH, D_MODEL), jnp.float32),
        "state_pool": nrm(ks[4], (L, DEC_BATCH, POOL_BUF, POOL_IN), jnp.float32),
        "state_mlstm_C": 0.1 * nrm(ks[5], (L, DEC_BATCH, M_HEADS, M_DK, M_DV), jnp.float32),
        "state_mlstm_n": 0.5 * nrm(ks[6], (L, DEC_BATCH, M_HEADS, M_DK), jnp.float32),
        "state_mlstm_m": nrm(ks[7], (L, DEC_BATCH, M_HEADS), jnp.float32),
        "w_ada": nrm(ks[8], (L, D_MODEL, 6 * D_MODEL), jnp.float32) * sD,
        "b_ada": 0.02 * nrm(ks[9], (L, 6 * D_MODEL), jnp.float32),
        "norm_mix_w": 1.0 + 0.02 * nrm(ks[10], (L, D_MODEL), jnp.float32),
        "w_in": nrm(ks[11], (L, D_MODEL, D_IN), jnp.float32) * sD,
        "b_igate": -2.0 + 0.1 * nrm(ks[12], (L, M_HEADS), jnp.float32),
        "b_fgate": 3.0 + 0.5 * nrm(ks[13], (L, M_HEADS), jnp.float32),
        "w_pool": nrm(ks[14], (L, POOL_GROUPS, POOL_GIN, POOL_GOUT), jnp.float32) * POOL_GIN ** -0.5,
        "pool_scale": 1.0 + 0.1 * nrm(ks[15], (L, D_MODEL), jnp.float32),
        "mlstm_norm_w": 1.0 + 0.02 * nrm(ks[16], (L, M_HEADS, M_DV), jnp.float32),
        "w_out": nrm(ks[17], (L, D_MODEL, D_MODEL), jnp.float32) * sD,
        "norm_ffn_w": 1.0 + 0.02 * nrm(ks[18], (L, D_MODEL), jnp.float32),
        "w_router_group": nrm(ks[19], (L, D_MODEL, N_GROUPS), jnp.float32) * sD,
        "w_router_expert": nrm(ks[20], (L, D_MODEL, N_EXPERTS), jnp.float32) * sD,
        "w_exp_gate": nrm(ks[21], (L, N_EXPERTS, D_MODEL, D_EXPERT), jnp.float32) * sD,
        "w_exp_up": nrm(ks[22], (L, N_EXPERTS, D_MODEL, D_EXPERT), jnp.float32) * sD,
        "w_exp_down": nrm(ks[23], (L, N_EXPERTS, D_EXPERT, D_MODEL), jnp.float32) * D_EXPERT ** -0.5,
        "final_norm_w": 1.0 + 0.02 * nrm(ks[24], (D_MODEL,), jnp.float32),
    }


def reference(x_prompt, x_sample, c_prompt, c_sample, state_pool, state_mlstm_C, state_mlstm_n,
              state_mlstm_m, w_ada, b_ada, norm_mix_w, w_in, b_igate, b_fgate, w_pool, pool_scale,
              mlstm_norm_w, w_out, norm_ffn_w, w_router_group, w_router_expert, w_exp_gate,
              w_exp_up, w_exp_down, final_norm_w):
    f32 = jnp.float32
    B = x_prompt.shape[0]
    hp, hs = x_prompt, x_sample
    pool_p, C_p, n_p, m_p = [], [], [], []
    pool_s, C_s, n_s, m_s = [], [], [], []
    for l in range(DEPTH):
        lw = (w_ada[l], b_ada[l], norm_mix_w[l], w_in[l], b_igate[l], b_fgate[l], w_pool[l],
              pool_scale[l], mlstm_norm_w[l], w_out[l], norm_ffn_w[l], w_router_group[l],
              w_router_expert[l], w_exp_gate[l], w_exp_up[l], w_exp_down[l])
        hp, pb, Cb, nb, mb = trunk_layer(
            hp, c_prompt, 0, jnp.zeros((B, POOL_BUF, POOL_IN), x_prompt.dtype),
            jnp.zeros((B, M_HEADS, M_DK, M_DV), f32), jnp.zeros((B, M_HEADS, M_DK), f32),
            jnp.zeros((B, M_HEADS), f32), *lw)
        pool_p.append(pb); C_p.append(Cb); n_p.append(nb); m_p.append(mb)
        hs, pb, Cb, nb, mb = trunk_layer(
            hs, c_sample, PAST_LEN, state_pool[l], state_mlstm_C[l].astype(f32),
            state_mlstm_n[l].astype(f32), state_mlstm_m[l].astype(f32), *lw)
        pool_s.append(pb); C_s.append(Cb); n_s.append(nb); m_s.append(mb)
    y_prompt = rmsnorm(hp, final_norm_w)
    y_sample = rmsnorm(hs, final_norm_w)
    return (y_prompt, y_sample, jnp.stack(pool_p), jnp.stack(C_p), jnp.stack(n_p), jnp.stack(m_p),
            jnp.stack(pool_s), jnp.stack(C_s), jnp.stack(n_s), jnp.stack(m_s))
```

```python
import functools
from typing import NamedTuple

import jax
import jax.numpy as jnp
from jax import lax
from jax.experimental import pallas as pl
from jax.experimental.pallas import tpu as pltpu

F32, BF16, I32, U32 = jnp.float32, jnp.bfloat16, jnp.int32, jnp.uint32

EPS = 1e-6
PAST_LEN = 16384
POOL_WINDOWS = (2, 4, 8, 16)
POOL_BUF = 15
N_GROUPS = 8
EXP_PER_GROUP = 8
N_EXPERTS = N_GROUPS * EXP_PER_GROUP
EXP_SHIFT = EXP_PER_GROUP.bit_length() - 1
TOP_K = 2

V7X_VMEM_BYTES = 64 * 2**20
LANES = 128
SUBLANES = 8
BF16_ROWS = 16
MIB = 2**20


class Tok(NamedTuple):
    n_prompt: int
    n_sample: int
    rows_p: int
    rows_s: int
    nb_p: int
    nb_s: int

    @property
    def n(self):
        return self.n_prompt + self.n_sample


class Tiles(NamedTuple):
    ada_tn: int = 512
    row_tm: int = 256
    mm_tm: int = 512
    mm_tn: int = 1024
    pool_tt: int = 256
    pool_bt: int = 16
    mlstm_l: int = 256
    mlstm_hb_s: int = 4
    merge_tm: int = 512
    rank_tr: int = 256
    moe_bm: int = 256
    moe_cw: int = 256
    moe_dcw: int = 2048
    comb_tm: int = 256
    dma_wave: int = 128


def _cparams(sem, vmem_bytes):
    limit = int(min(max(vmem_bytes * 5 // 4 + 2 * MIB, 16 * MIB), V7X_VMEM_BYTES - 6 * MIB))
    return pltpu.CompilerParams(dimension_semantics=sem, vmem_limit_bytes=limit)


def _batch_of(row, tok):
    return jnp.where(row < tok.n_prompt, row // tok.rows_p,
                     tok.nb_p + (row - tok.n_prompt) // tok.rows_s)


def _mod_rows(ref, row0, tok, d):
    top = jnp.broadcast_to(ref[pl.ds(_batch_of(row0, tok), 1), :], (SUBLANES, d))
    bot = jnp.broadcast_to(ref[pl.ds(_batch_of(row0 + SUBLANES, tok), 1), :], (SUBLANES, d))
    return jnp.concatenate([top, bot], axis=0)


def _rms(x, w):
    r = lax.rsqrt(jnp.mean(x * x, axis=-1, keepdims=True) + EPS)
    return (x * r) * w


def _log_sigmoid(x):
    return jnp.minimum(x, 0.0) - jnp.log1p(jnp.exp(-jnp.abs(x)))


def _ada_kernel(c_ref, w_ref, b_ref, o_ref):
    c = c_ref[...]
    a = (c * jax.nn.sigmoid(c)).astype(BF16)
    o_ref[...] = jnp.dot(a, w_ref[...].astype(BF16), preferred_element_type=F32) + b_ref[...]


def _ada(c_all, w_ada, b_ada, tiles):
    g, d = c_all.shape
    n = w_ada.shape[1]
    tn = tiles.ada_tn
    vmem = 2 * (g * d * 4 + d * tn * 4 + g * tn * 4) + d * tn * 2
    return pl.pallas_call(
        _ada_kernel,
        grid=(n // tn,),
        in_specs=[pl.BlockSpec((g, d), lambda j: (0, 0)),
                  pl.BlockSpec((d, tn), lambda j: (0, j)),
                  pl.BlockSpec((1, tn), lambda j: (0, j))],
        out_specs=pl.BlockSpec((g, tn), lambda j: (0, j)),
        out_shape=jax.ShapeDtypeStruct((g, n), F32),
        compiler_params=_cparams(("arbitrary",), vmem),
        name="ada",
    )(c_all, w_ada, b_ada.reshape(1, n))


def _dual_specs(tok, tm, d):
    npt = tok.n_prompt // tm
    nst = tok.n_sample // tm
    return (pl.BlockSpec((tm, d), lambda i: (jnp.minimum(i, npt - 1), 0)),
            pl.BlockSpec((tm, d), lambda i: (jnp.clip(i - npt, 0, nst - 1), 0)))


def _prenorm_kernel(xp_ref, xs_ref, sc_ref, sh_ref, nw_ref, wg_ref, u_ref, gates_ref, *, tok, tm):
    i = pl.program_id(0)
    d = u_ref.shape[1]

    def run(x_ref):
        def slab(s, carry):
            r = pl.multiple_of(s * BF16_ROWS, BF16_ROWS)
            row0 = i * tm + r
            xn = _rms(x_ref[pl.ds(r, BF16_ROWS), :], nw_ref[...])
            u = xn * (1.0 + _mod_rows(sc_ref, row0, tok, d)) + _mod_rows(sh_ref, row0, tok, d)
            u_ref[pl.ds(r, BF16_ROWS), :] = u.astype(BF16)
            return carry
        lax.fori_loop(0, tm // BF16_ROWS, slab, 0)

    npt = tok.n_prompt // tm
    pl.when(i < npt)(lambda: run(xp_ref))
    pl.when(i >= npt)(lambda: run(xs_ref))
    gates_ref[...] = jnp.dot(u_ref[...], wg_ref[...], preferred_element_type=F32)


def _prenorm(xp, xs, mod, norm_w, w_gates, tok, tiles):
    d = xp.shape[1]
    g = mod.shape[0]
    tm = tiles.row_tm
    xp_spec, xs_spec = _dual_specs(tok, tm, d)
    vmem = 2 * (2 * tm * d * 4 + 2 * g * d * 4 + d * LANES * 2 + tm * d * 2 + tm * LANES * 4)
    return pl.pallas_call(
        functools.partial(_prenorm_kernel, tok=tok, tm=tm),
        grid=(tok.n // tm,),
        in_specs=[xp_spec, xs_spec,
                  pl.BlockSpec((g, d), lambda i: (0, 1)),
                  pl.BlockSpec((g, d), lambda i: (0, 0)),
                  pl.BlockSpec((1, d), lambda i: (0, 0)),
                  pl.BlockSpec((d, LANES), lambda i: (0, 0))],
        out_specs=[pl.BlockSpec((tm, d), lambda i: (i, 0)),
                   pl.BlockSpec((tm, LANES), lambda i: (i, 0))],
        out_shape=[jax.ShapeDtypeStruct((tok.n, d), BF16),
                   jax.ShapeDtypeStruct((tok.n, LANES), F32)],
        compiler_params=_cparams(("arbitrary",), vmem),
        name="prenorm",
    )(xp, xs, mod, mod, norm_w.reshape(1, d), w_gates)


def _matmul_kernel(a_ref, w_ref, o_ref, wb_ref):
    @pl.when(pl.program_id(1) == 0)
    def _():
        wb_ref[...] = w_ref[...].astype(BF16)
    o_ref[...] = jnp.dot(a_ref[...], wb_ref[...], preferred_element_type=F32)


def _matmul(a, w, n_cols, tiles):
    m, k = a.shape
    tm, tn = tiles.mm_tm, tiles.mm_tn
    vmem = 2 * (tm * k * 2 + k * tn * 4 + tm * tn * 4) + k * tn * 2
    return pl.pallas_call(
        _matmul_kernel,
        grid=(n_cols // tn, m // tm),
        in_specs=[pl.BlockSpec((tm, k), lambda j, i: (i, 0)),
                  pl.BlockSpec((k, tn), lambda j, i: (0, j))],
        out_specs=pl.BlockSpec((tm, tn), lambda j, i: (i, j)),
        out_shape=jax.ShapeDtypeStruct((m, n_cols), F32),
        scratch_shapes=[pltpu.VMEM((k, tn), BF16)],
        compiler_params=_cparams(("arbitrary", "arbitrary"), vmem),
        name="matmul",
    )(a, w)


def _pool_kernel(p_ref, st_ref, d_ref, new_ref, ext_ref, *, bt, tt, gin, pos0, nt):
    t = pl.program_id(1)
    cdim = ext_ref.shape[2]
    halo = POOL_BUF + 1

    @pl.when(t == 0)
    def _():
        ext_ref[:, 1:halo, :] = st_ref[...]

    @pl.when(t > 0)
    def _():
        ext_ref[:, 0:halo, :] = ext_ref[:, tt:tt + halo, :]

    ext_ref[:, halo:halo + tt, :] = p_ref[...].reshape(bt, tt, cdim)
    pos = pos0 + t * tt + lax.broadcasted_iota(I32, (bt, tt, gin), 1)
    for g, w in enumerate(POOL_WINDOWS):
        cs = slice(g * gin, (g + 1) * gin)
        cur = ext_ref[:, halo:halo + tt, cs]
        acc = cur
        for j in range(1, w):
            acc = acc + ext_ref[:, halo - j:halo - j + tt, cs]
        cnt = jnp.minimum(pos + 1, w).astype(F32)
        d_ref[:, cs] = (acc / cnt - cur).reshape(bt * tt, gin)

    @pl.when(t == nt - 1)
    def _():
        new_ref[...] = ext_ref[:, tt + 1:tt + halo, :]


def _pool(z, state, row_off, nb, t_len, bt, tt, pos0):
    cdim = state.shape[2]
    gin = cdim // len(POOL_WINDOWS)
    nt = t_len // tt
    rows = bt * tt
    rb0 = row_off // rows
    vmem = 2 * (2 * rows * cdim * 4 + 2 * bt * 16 * cdim * 4) + bt * (tt + 16) * cdim * 4 + 8 * rows * gin * 4
    return pl.pallas_call(
        functools.partial(_pool_kernel, bt=bt, tt=tt, gin=gin, pos0=pos0, nt=nt),
        grid=(nb // bt, nt),
        in_specs=[pl.BlockSpec((rows, cdim), lambda b, t: (rb0 + b * nt + t, 0)),
                  pl.BlockSpec((bt, POOL_BUF, cdim), lambda b, t: (b, 0, 0))],
        out_specs=[pl.BlockSpec((rows, cdim), lambda b, t: (b * nt + t, 0)),
                   pl.BlockSpec((bt, POOL_BUF, cdim), lambda b, t: (b, 0, 0))],
        out_shape=[jax.ShapeDtypeStruct((nb * t_len, cdim), F32),
                   jax.ShapeDtypeStruct((nb, POOL_BUF, cdim), F32)],
        scratch_shapes=[pltpu.VMEM((bt, tt + POOL_BUF + 1, cdim), F32)],
        compiler_params=_cparams(("arbitrary", "arbitrary"), vmem),
        name="pool",
    )(z, state)


def _mlstm_kernel(q_ref, k_ref, v_ref, o_ref, gc_ref, gr_ref, brow_ref, bcol_ref, nw_ref, c0_ref, n0_ref, m0_ref,
                  mo_ref, c_out, n_out, m_out, c_s, n_s, m_s, *, L, hb, dk, dv, nc, nheads):
    hblk = pl.program_id(1)
    c = pl.program_id(2)

    @pl.when(c == 0)
    def _():
        c_s[...] = c0_ref[0]
        n_s[...] = n0_ref[0]
        m_s[...] = m0_ref[0]

    gcb = gc_ref[...] + brow_ref[...]
    grb = gr_ref[0] + bcol_ref[...]
    lane = lax.broadcasted_iota(I32, gcb.shape, 1)
    sub = lax.broadcasted_iota(I32, grb.shape, 0)
    rowi = lax.broadcasted_iota(I32, (L, L), 0)
    coli = lax.broadcasted_iota(I32, (L, L), 1)
    tri = rowi >= coli
    scale = dk ** -0.5

    for hh in range(hb):
        h = hblk * hb + hh
        li_col = jnp.sum(jnp.where(lane == h, gcb, 0.0), axis=1, keepdims=True)
        lf_col = _log_sigmoid(jnp.sum(jnp.where(lane == h + nheads, gcb, 0.0), axis=1, keepdims=True))
        li_row = jnp.sum(jnp.where(sub == h, grb, 0.0), axis=0, keepdims=True)
        lf_row = _log_sigmoid(jnp.sum(jnp.where(sub == h + nheads, grb, 0.0), axis=0, keepdims=True))
        b_col = jnp.sum(jnp.where(tri, lf_row, 0.0), axis=1, keepdims=True)
        b_row = jnp.sum(jnp.where(rowi <= coli, lf_col, 0.0), axis=0, keepdims=True)

        q = q_ref[:, hh * dk:(hh + 1) * dk] * scale
        k = k_ref[:, hh * dk:(hh + 1) * dk]
        qb = q.astype(BF16)
        kb = k.astype(BF16)
        vb = v_ref[:, hh * dv:(hh + 1) * dv].astype(BF16)
        cmat = c_s[hh]
        nvec = n_s[hh]
        m_prev = m_s[hh]

        dmat = jnp.where(tri, b_col - b_row + li_row, -jnp.inf)
        inter = b_col + m_prev
        m_t = jnp.maximum(inter, jnp.max(dmat, axis=1, keepdims=True))
        dw = jnp.exp(dmat - m_t)
        iw = jnp.exp(inter - m_t)
        s = lax.dot_general(qb, kb, (((1,), (1,)), ((), ())), preferred_element_type=F32) * dw
        num = (jnp.dot(s.astype(BF16), vb, preferred_element_type=F32)
               + iw * jnp.dot(qb, cmat.astype(BF16), preferred_element_type=F32))
        den = jnp.sum(s, axis=1, keepdims=True) + iw * jnp.sum(q * nvec, axis=1, keepdims=True)
        hval = num / jnp.maximum(jnp.abs(den), jnp.exp(-m_t))

        hs = slice(hh * dv, (hh + 1) * dv)
        hn = _rms(hval, nw_ref[:, hs])
        mo_ref[:, hs] = jax.nn.sigmoid(o_ref[:, hs]) * hn

        b_last = b_col[L - 1:L, :]
        dl_col = b_last - b_col + li_col
        dl_row = b_last - b_row + li_row
        m_new = jnp.maximum(b_last + m_prev, jnp.max(dl_row, axis=1, keepdims=True))
        dec = jnp.exp(b_last + m_prev - m_new)
        kk = jnp.exp(dl_col - m_new) * k
        upd = lax.dot_general(kk.astype(BF16), vb, (((0,), (0,)), ((), ())), preferred_element_type=F32)
        c_s[hh] = dec * cmat + upd
        n_s[hh] = dec * nvec + jnp.sum(kk, axis=0, keepdims=True)
        m_s[hh] = m_new

    @pl.when(c == nc - 1)
    def _():
        c_out[0] = c_s[...]
        n_out[0] = n_s[...]
        m_out[0] = m_s[...]


def _mlstm(z, gates, gates_t, bias_row, bias_col, norm_w, c0, n0, m0, row_off, nb, t_len, L, hb, offs):
    _, nheads, dk, dv = c0.shape
    nc = t_len // L
    rb0 = row_off // L
    nhb = nheads // hb
    q_off, k_off, v_off, o_off = offs
    wq, wv = hb * dk, hb * dv

    def rowblk(b, c):
        return rb0 + b * nc + c

    in_specs = [
        pl.BlockSpec((L, wq), lambda b, h, c: (rowblk(b, c), q_off // wq + h)),
        pl.BlockSpec((L, wq), lambda b, h, c: (rowblk(b, c), k_off // wq + h)),
        pl.BlockSpec((L, wv), lambda b, h, c: (rowblk(b, c), v_off // wv + h)),
        pl.BlockSpec((L, wv), lambda b, h, c: (rowblk(b, c), o_off // wv + h)),
        pl.BlockSpec((L, LANES), lambda b, h, c: (rowblk(b, c), 0)),
        pl.BlockSpec((1, 2 * nheads, L), lambda b, h, c: (b * nc + c, 0, 0)),
        pl.BlockSpec((1, LANES), lambda b, h, c: (0, 0)),
        pl.BlockSpec((2 * nheads, 1), lambda b, h, c: (0, 0)),
        pl.BlockSpec((1, wv), lambda b, h, c: (0, h)),
        pl.BlockSpec((1, hb, dk, dv), lambda b, h, c: (b, h, 0, 0)),
        pl.BlockSpec((1, hb, 1, dk), lambda b, h, c: (b, h, 0, 0)),
        pl.BlockSpec((1, hb, 1, 1), lambda b, h, c: (b, h, 0, 0)),
    ]
    state_bytes = hb * dk * dv * 4
    vmem = 2 * (2 * L * wq * 4 + 3 * L * wv * 4 + 2 * state_bytes) + state_bytes + 24 * L * max(L, dv) * 4
    return pl.pallas_call(
        functools.partial(_mlstm_kernel, L=L, hb=hb, dk=dk, dv=dv, nc=nc, nheads=nheads),
        grid=(nb, nhb, nc),
        in_specs=in_specs,
        out_specs=[pl.BlockSpec((L, wv), lambda b, h, c: (b * nc + c, h)),
                   pl.BlockSpec((1, hb, dk, dv), lambda b, h, c: (b, h, 0, 0)),
                   pl.BlockSpec((1, hb, 1, dk), lambda b, h, c: (b, h, 0, 0)),
                   pl.BlockSpec((1, hb, 1, 1), lambda b, h, c: (b, h, 0, 0))],
        out_shape=[jax.ShapeDtypeStruct((nb * t_len, nheads * dv), F32),
                   jax.ShapeDtypeStruct(c0.shape, F32),
                   jax.ShapeDtypeStruct(n0.shape, F32),
                   jax.ShapeDtypeStruct(m0.shape, F32)],
        scratch_shapes=[pltpu.VMEM((hb, dk, dv), F32), pltpu.VMEM((hb, 1, dk), F32),
                        pltpu.VMEM((hb, 1, 1), F32)],
        compiler_params=_cparams(("arbitrary", "arbitrary", "arbitrary"), vmem),
        name="mlstm",
    )(z, z, z, z, gates, gates_t, bias_row, bias_col, norm_w, c0, n0, m0)


def _merge_kernel(dp_ref, ds_ref, wp_ref, ps_ref, ga_ref, gb_ref, mop_ref, mos_ref, o_ref, *, gin, gout, npt):
    def run(d_ref, mo_ref):
        for gg in range(wp_ref.shape[0]):
            y = jnp.dot(d_ref[:, gg * gin:(gg + 1) * gin].astype(BF16), wp_ref[gg], preferred_element_type=F32)
            cs = slice(gg * gout, (gg + 1) * gout)
            pool_out = y * ps_ref[:, cs]
            merged = jax.nn.sigmoid(ga_ref[:, cs]) * pool_out + jax.nn.sigmoid(gb_ref[:, cs]) * mo_ref[:, cs]
            o_ref[:, cs] = merged.astype(BF16)

    i = pl.program_id(0)
    pl.when(i < npt)(lambda: run(dp_ref, mop_ref))
    pl.when(i >= npt)(lambda: run(ds_ref, mos_ref))


def _merge(d_p, d_s, w_pool_b, pool_scale, z, mo_p, mo_s, ga_off, gb_off, tok, tiles):
    cdim = d_p.shape[1]
    ng, gin, gout = w_pool_b.shape
    dm = ng * gout
    half = dm // 2
    gh = ng // 2
    tm = tiles.merge_tm
    npt = tok.n_prompt // tm
    nst = tok.n_sample // tm

    def prompt_spec(width):
        return pl.BlockSpec((tm, width), lambda i, j: (jnp.minimum(i, npt - 1), jnp.where(i < npt, j, 1)))

    def sample_spec(width):
        return pl.BlockSpec((tm, width), lambda i, j: (jnp.clip(i - npt, 0, nst - 1), jnp.where(i < npt, 0, j)))

    vmem = 2 * (2 * tm * cdim * 2 + gh * gin * gout * 2 + 4 * tm * half * 4 + tm * half * 2) + 6 * tm * gout * 4
    return pl.pallas_call(
        functools.partial(_merge_kernel, gin=gin, gout=gout, npt=npt),
        grid=(tok.n // tm, 2),
        in_specs=[prompt_spec(cdim // 2), sample_spec(cdim // 2),
                  pl.BlockSpec((gh, gin, gout), lambda i, j: (j, 0, 0)),
                  pl.BlockSpec((1, half), lambda i, j: (0, j)),
                  pl.BlockSpec((tm, half), lambda i, j: (i, ga_off // half + j)),
                  pl.BlockSpec((tm, half), lambda i, j: (i, gb_off // half + j)),
                  prompt_spec(half), sample_spec(half)],
        out_specs=pl.BlockSpec((tm, half), lambda i, j: (i, j)),
        out_shape=jax.ShapeDtypeStruct((tok.n, dm), BF16),
        compiler_params=_cparams(("arbitrary", "arbitrary"), vmem),
        name="merge",
    )(d_p, d_s, w_pool_b, pool_scale.reshape(1, dm), z, z, mo_p, mo_s)


def _ffn_norm_kernel(xp_ref, xs_ref, acc_ref, g1_ref, sc_ref, sh_ref, nw_ref, wr_ref,
                     x1_ref, up_ref, re_ref, rw_ref, ub_ref, *, tok, tm):
    i = pl.program_id(0)
    d = x1_ref.shape[1]
    hd = d // 2

    def run(x_ref):
        def slab(s, carry):
            r = pl.multiple_of(s * BF16_ROWS, BF16_ROWS)
            rows = pl.ds(r, BF16_ROWS)
            row0 = i * tm + r
            x1 = x_ref[rows, :] + _mod_rows(g1_ref, row0, tok, d) * acc_ref[rows, :]
            x1_ref[rows, :] = x1
            u = _rms(x1, nw_ref[...]) * (1.0 + _mod_rows(sc_ref, row0, tok, d)) + _mod_rows(sh_ref, row0, tok, d)
            ub = u.astype(BF16)
            ub_ref[rows, :] = ub
            bits = lax.bitcast_convert_type(ub.astype(F32), U32)
            up_ref[rows, :] = (bits[:, hd:] & jnp.uint32(0xFFFF0000)) | (bits[:, :hd] >> 16)
            return carry
        lax.fori_loop(0, tm // BF16_ROWS, slab, 0)

    npt = tok.n_prompt // tm
    pl.when(i < npt)(lambda: run(xp_ref))
    pl.when(i >= npt)(lambda: run(xs_ref))

    lg = jnp.dot(ub_ref[...], wr_ref[...], preferred_element_type=F32)
    lane = lax.broadcasted_iota(I32, lg.shape, 1)
    lanef = lane.astype(F32)
    big = float(LANES)
    is_g = lane < N_GROUPS
    gl = jnp.where(is_g, lg, -jnp.inf)
    gmax = jnp.max(gl, axis=1, keepdims=True)
    grp = jnp.min(jnp.where(gl == gmax, lanef, big), axis=1, keepdims=True)
    p_grp = 1.0 / jnp.sum(jnp.where(is_g, jnp.exp(lg - gmax), 0.0), axis=1, keepdims=True)
    eidx = lane - N_GROUPS
    in_grp = (eidx >= 0) & (eidx < N_EXPERTS) & ((eidx >> EXP_SHIFT).astype(F32) == grp)
    el = jnp.where(in_grp, lg, -jnp.inf)
    v1 = jnp.max(el, axis=1, keepdims=True)
    i1 = jnp.min(jnp.where(el == v1, lanef, big), axis=1, keepdims=True)
    el2 = jnp.where(lanef == i1, -jnp.inf, el)
    v2 = jnp.max(el2, axis=1, keepdims=True)
    i2 = jnp.min(jnp.where(el2 == v2, lanef, big), axis=1, keepdims=True)
    e2 = jnp.exp(v2 - v1)
    w1 = (1.0 / (1.0 + e2)) * p_grp
    w2 = (e2 / (1.0 + e2)) * p_grp
    re_ref[...] = jnp.where(lane == 0, i1 - N_GROUPS, jnp.where(lane == 1, i2 - N_GROUPS, 0.0))
    rw_ref[...] = jnp.where(lane == 0, w1, jnp.where(lane == 1, w2, 0.0))


def _ffn_norm(xp, xs, acc, mod, norm_w, w_router, tok, tiles):
    d = xp.shape[1]
    g = mod.shape[0]
    tm = tiles.row_tm
    xp_spec, xs_spec = _dual_specs(tok, tm, d)
    vmem = (2 * (3 * tm * d * 4 + 3 * g * d * 4 + d * LANES * 2 + tm * d * 4 + tm * d * 2 + 2 * tm * LANES * 4)
            + tm * d * 2 + 16 * tm * LANES * 4)
    return pl.pallas_call(
        functools.partial(_ffn_norm_kernel, tok=tok, tm=tm),
        grid=(tok.n // tm,),
        in_specs=[xp_spec, xs_spec,
                  pl.BlockSpec((tm, d), lambda i: (i, 0)),
                  pl.BlockSpec((g, d), lambda i: (0, 2)),
                  pl.BlockSpec((g, d), lambda i: (0, 4)),
                  pl.BlockSpec((g, d), lambda i: (0, 3)),
                  pl.BlockSpec((1, d), lambda i: (0, 0)),
                  pl.BlockSpec((d, LANES), lambda i: (0, 0))],
        out_specs=[pl.BlockSpec((tm, d), lambda i: (i, 0)),
                   pl.BlockSpec((tm, d // 2), lambda i: (i, 0)),
                   pl.BlockSpec((tm, LANES), lambda i: (i, 0)),
                   pl.BlockSpec((tm, LANES), lambda i: (i, 0))],
        out_shape=[jax.ShapeDtypeStruct((tok.n, d), F32),
                   jax.ShapeDtypeStruct((tok.n, d // 2), U32),
                   jax.ShapeDtypeStruct((tok.n, LANES), F32),
                   jax.ShapeDtypeStruct((tok.n, LANES), F32)],
        scratch_shapes=[pltpu.VMEM((tm, d), BF16)],
        compiler_params=_cparams(("arbitrary",), vmem),
        name="ffn_norm",
    )(xp, xs, acc, mod, mod, mod, norm_w.reshape(1, d), w_router)


def _rank_kernel(re_ref, rank_ref, cnt_ref, carry_ref, *, tr):
    j = pl.program_id(0)
    i = pl.program_id(1)

    @pl.when((j == 0) & (i == 0))
    def _():
        carry_ref[...] = jnp.zeros_like(carry_ref)

    re = re_ref[...]
    lane = lax.broadcasted_iota(I32, re.shape, 1)
    e_col = jnp.sum(jnp.where(lane == j, re, 0.0), axis=1, keepdims=True)
    onehot = lane.astype(F32) == e_col
    rowi = lax.broadcasted_iota(I32, (tr, tr), 0)
    coli = lax.broadcasted_iota(I32, (tr, tr), 1)
    tri = jnp.where(rowi >= coli, 1.0, 0.0).astype(BF16)
    prefix = jnp.dot(tri, jnp.where(onehot, 1.0, 0.0).astype(BF16), preferred_element_type=F32)
    carry = carry_ref[0:1, :]
    rank = jnp.sum(jnp.where(onehot, prefix - 1.0 + carry, 0.0), axis=1, keepdims=True)
    rank_ref[...] = jnp.broadcast_to(rank, rank_ref.shape)
    new_carry = carry + prefix[tr - 1:tr, :]
    carry_ref[...] = jnp.broadcast_to(new_carry, carry_ref.shape)
    cnt_ref[...] = jnp.broadcast_to(new_carry, cnt_ref.shape)


def _rank(re, tiles):
    n = re.shape[0]
    tr = tiles.rank_tr
    nt = n // tr
    return pl.pallas_call(
        functools.partial(_rank_kernel, tr=tr),
        grid=(TOP_K, nt),
        in_specs=[pl.BlockSpec((tr, LANES), lambda j, i: (i, 0))],
        out_specs=[pl.BlockSpec((tr, LANES), lambda j, i: (j * nt + i, 0)),
                   pl.BlockSpec((SUBLANES, LANES), lambda j, i: (0, 0))],
        out_shape=[jax.ShapeDtypeStruct((TOP_K * n, LANES), F32),
                   jax.ShapeDtypeStruct((SUBLANES, LANES), F32)],
        scratch_shapes=[pltpu.VMEM((SUBLANES, LANES), F32)],
        compiler_params=_cparams(("arbitrary", "arbitrary"), 4 * MIB),
        name="rank",
    )(re)


def _scatter_kernel(dest_ref, src_ref, _, out_ref, sems, *, n_assign, wave):
    n_waves = n_assign // wave

    def row_copy(a, slot):
        return pltpu.make_async_copy(src_ref.at[pl.ds(a // TOP_K, 1)], out_ref.at[pl.ds(dest_ref[a], 1)],
                                     sems.at[slot])

    def start_wave(w):
        def body(r, carry):
            row_copy(w * wave + r, w % 2).start()
            return carry
        lax.fori_loop(0, wave, body, 0)

    def wait_wave(w):
        def body(r, carry):
            row_copy(w * wave + r, w % 2).wait()
            return carry
        lax.fori_loop(0, wave, body, 0)

    start_wave(0)

    def step(w, carry):
        start_wave(w)
        wait_wave(w - 1)
        return carry
    lax.fori_loop(1, n_waves, step, 0)
    wait_wave(n_waves - 1)


def _scatter(dest_flat, src, n_rows_out, tiles):
    n_assign = dest_flat.shape[0]
    assert n_assign % tiles.dma_wave == 0
    width = src.shape[1]
    zeros = jnp.zeros((n_rows_out, width), src.dtype)
    return pl.pallas_call(
        functools.partial(_scatter_kernel, n_assign=n_assign, wave=tiles.dma_wave),
        grid_spec=pltpu.PrefetchScalarGridSpec(
            num_scalar_prefetch=1,
            grid=(1,),
            in_specs=[pl.BlockSpec(memory_space=pl.ANY), pl.BlockSpec(memory_space=pl.ANY)],
            out_specs=pl.BlockSpec(memory_space=pl.ANY),
            scratch_shapes=[pltpu.SemaphoreType.DMA((2,))]),
        out_shape=jax.ShapeDtypeStruct((n_rows_out, width), src.dtype),
        input_output_aliases={2: 0},
        compiler_params=pltpu.CompilerParams(dimension_semantics=("arbitrary",), has_side_effects=True),
        name="scatter",
    )(dest_flat, src, zeros)


def _unpack_pair(words):
    lo = lax.bitcast_convert_type(words << 16, F32).astype(BF16)
    hi = lax.bitcast_convert_type(words & jnp.uint32(0xFFFF0000), F32).astype(BF16)
    return lo, hi


def _moe_up_kernel(blk_ref, col_ref, exp_ref, first_ref, valid_ref, xs_ref, wg_ref, wu_ref, h_ref, wgb, wub):
    w = pl.program_id(0)
    hd = xs_ref.shape[1]

    @pl.when(first_ref[w] == 1)
    def _():
        wgb[...] = wg_ref[...].astype(BF16)
        wub[...] = wu_ref[...].astype(BF16)

    @pl.when(valid_ref[w] == 1)
    def _():
        lo, hi = _unpack_pair(xs_ref[...])
        g = (jnp.dot(lo, wgb[:hd, :], preferred_element_type=F32)
             + jnp.dot(hi, wgb[hd:, :], preferred_element_type=F32))
        u = (jnp.dot(lo, wub[:hd, :], preferred_element_type=F32)
             + jnp.dot(hi, wub[hd:, :], preferred_element_type=F32))
        h_ref[...] = ((g * jax.nn.sigmoid(g)) * u).astype(BF16)


def _moe_up(plan, xs, w_gate, w_up, tiles):
    p_rows, hd = xs.shape
    _, d, de = w_gate.shape
    bm, cw = tiles.moe_bm, tiles.moe_cw
    n_items = plan["up"][0].shape[0]
    vmem = 2 * (bm * hd * 4 + 2 * d * cw * 4 + bm * cw * 2) + 2 * d * cw * 2 + 8 * bm * cw * 4 + 2 * bm * d * 2
    return pl.pallas_call(
        _moe_up_kernel,
        grid_spec=pltpu.PrefetchScalarGridSpec(
            num_scalar_prefetch=5,
            grid=(n_items,),
            in_specs=[pl.BlockSpec((bm, hd), lambda w, blk, col, ex, fi, va: (blk[w], 0)),
                      pl.BlockSpec((None, d, cw), lambda w, blk, col, ex, fi, va: (ex[w], 0, col[w])),
                      pl.BlockSpec((None, d, cw), lambda w, blk, col, ex, fi, va: (ex[w], 0, col[w]))],
            out_specs=pl.BlockSpec((bm, cw), lambda w, blk, col, ex, fi, va: (blk[w], col[w])),
            scratch_shapes=[pltpu.VMEM((d, cw), BF16), pltpu.VMEM((d, cw), BF16)]),
        out_shape=jax.ShapeDtypeStruct((p_rows, de), BF16),
        compiler_params=_cparams(("arbitrary",), vmem),
        name="moe_up",
    )(*plan["up"], xs, w_gate, w_up)


def _moe_down_kernel(blk_ref, col_ref, exp_ref, first_ref, valid_ref, h_ref, wd_ref, y_ref, wdb):
    w = pl.program_id(0)

    @pl.when(first_ref[w] == 1)
    def _():
        wdb[...] = wd_ref[...].astype(BF16)

    @pl.when(valid_ref[w] == 1)
    def _():
        y_ref[...] = jnp.dot(h_ref[...], wdb[...], preferred_element_type=F32)


def _moe_down(plan, hdn, w_down, tiles):
    p_rows, de = hdn.shape
    d = w_down.shape[2]
    bm, dcw = tiles.moe_bm, tiles.moe_dcw
    n_items = plan["down"][0].shape[0]
    vmem = 2 * (bm * de * 2 + de * dcw * 4 + bm * dcw * 4) + de * dcw * 2 + 2 * bm * dcw * 4
    return pl.pallas_call(
        _moe_down_kernel,
        grid_spec=pltpu.PrefetchScalarGridSpec(
            num_scalar_prefetch=5,
            grid=(n_items,),
            in_specs=[pl.BlockSpec((bm, de), lambda w, blk, col, ex, fi, va: (blk[w], 0)),
                      pl.BlockSpec((None, de, dcw), lambda w, blk, col, ex, fi, va: (ex[w], 0, col[w]))],
            out_specs=pl.BlockSpec((bm, dcw), lambda w, blk, col, ex, fi, va: (blk[w], col[w])),
            scratch_shapes=[pltpu.VMEM((de, dcw), BF16)]),
        out_shape=jax.ShapeDtypeStruct((p_rows, d), F32),
        compiler_params=_cparams(("arbitrary",), vmem),
        name="moe_down",
    )(*plan["down"], hdn, w_down)


def _work_items(nb_e, blk0_e, n_cols, n_items):
    items_e = nb_e * n_cols
    cum = jnp.cumsum(items_e)
    total = cum[-1]
    w = jnp.minimum(jnp.arange(n_items, dtype=I32), total - 1)
    e = jnp.minimum(jnp.searchsorted(cum, w, side="right"), N_EXPERTS - 1).astype(I32)
    r = w - (cum[e] - items_e[e])
    nb = jnp.maximum(nb_e[e], 1)
    col = r // nb
    blk = blk0_e[e] + r % nb
    valid = (jnp.arange(n_items, dtype=I32) < total).astype(I32)
    key = e * n_cols + col
    first = jnp.concatenate([jnp.ones((1,), I32), (key[1:] != key[:-1]).astype(I32)]) * valid
    return blk.astype(I32), col.astype(I32), e, first.astype(I32), valid


def _dispatch_plan(re, rank_out, cnt_out, n, n_blocks, d_expert, d_model, tiles):
    bm = tiles.moe_bm
    eid = re[:, :TOP_K].astype(I32)
    rank = rank_out[:, 0].reshape(TOP_K, n).T.astype(I32)
    counts = cnt_out[0, :N_EXPERTS].astype(I32)
    nb_e = (counts + bm - 1) // bm
    blk0_e = jnp.cumsum(nb_e) - nb_e
    dest = blk0_e[eid] * bm + rank
    plan = {
        "dest": dest.reshape(-1).astype(I32),
        "up": _work_items(nb_e, blk0_e, d_expert // tiles.moe_cw, n_blocks * (d_expert // tiles.moe_cw)),
        "down": _work_items(nb_e, blk0_e, d_model // tiles.moe_dcw, n_blocks * (d_model // tiles.moe_dcw)),
    }
    return plan


def _combine_kernel(dest_ref, ys_ref, x1_ref, rw_ref, g2_ref, fw_ref, yp_ref, ysm_ref, ybuf, sem, *, tok, tm):
    i = pl.program_id(0)
    d = x1_ref.shape[1]
    base = i * tm

    def row_copy(r, j):
        return pltpu.make_async_copy(ys_ref.at[pl.ds(dest_ref[(base + r) * TOP_K + j], 1)],
                                     ybuf.at[j, pl.ds(r, 1)], sem.at[0])

    def issue(r, carry):
        for j in range(TOP_K):
            row_copy(r, j).start()
        return carry
    lax.fori_loop(0, tm, issue, 0)

    def drain(r, carry):
        for j in range(TOP_K):
            row_copy(r, j).wait()
        return carry
    lax.fori_loop(0, tm, drain, 0)

    def run(out_ref):
        def slab(s, carry):
            r = pl.multiple_of(s * SUBLANES, SUBLANES)
            rows = pl.ds(r, SUBLANES)
            wts = rw_ref[rows, :]
            ff = wts[:, 0:1] * ybuf[0, rows, :] + wts[:, 1:2] * ybuf[1, rows, :]
            g2 = jnp.broadcast_to(g2_ref[pl.ds(_batch_of(base + r, tok), 1), :], (SUBLANES, d))
            out_ref[rows, :] = _rms(x1_ref[rows, :] + g2 * ff, fw_ref[...])
            return carry
        lax.fori_loop(0, tm // SUBLANES, slab, 0)

    npt = tok.n_prompt // tm
    pl.when(i < npt)(lambda: run(yp_ref))
    pl.when(i >= npt)(lambda: run(ysm_ref))


def _combine(dest_flat, ys, x1, rw, mod, final_w, tok, tiles):
    d = x1.shape[1]
    g = mod.shape[0]
    tm = tiles.comb_tm
    npt = tok.n_prompt // tm
    nst = tok.n_sample // tm
    vmem = 2 * (tm * d * 4 + tm * LANES * 4 + g * d * 4 + 2 * tm * d * 4) + TOP_K * tm * d * 4
    return pl.pallas_call(
        functools.partial(_combine_kernel, tok=tok, tm=tm),
        grid_spec=pltpu.PrefetchScalarGridSpec(
            num_scalar_prefetch=1,
            grid=(tok.n // tm,),
            in_specs=[pl.BlockSpec(memory_space=pl.ANY),
                      pl.BlockSpec((tm, d), lambda i, dest: (i, 0)),
                      pl.BlockSpec((tm, LANES), lambda i, dest: (i, 0)),
                      pl.BlockSpec((g, d), lambda i, dest: (0, 5)),
                      pl.BlockSpec((1, d), lambda i, dest: (0, 0))],
            out_specs=[pl.BlockSpec((tm, d), lambda i, dest: (jnp.minimum(i, npt - 1), 0)),
                       pl.BlockSpec((tm, d), lambda i, dest: (jnp.clip(i - npt, 0, nst - 1), 0))],
            scratch_shapes=[pltpu.VMEM((TOP_K, tm, d), F32), pltpu.SemaphoreType.DMA((1,))]),
        out_shape=[jax.ShapeDtypeStruct((tok.n_prompt, d), F32),
                   jax.ShapeDtypeStruct((tok.n_sample, d), F32)],
        compiler_params=_cparams(("arbitrary",), vmem),
        name="combine",
    )(dest_flat, ys, x1, rw, mod, final_w.reshape(1, d))


def _gate_layouts(gates, n_rows, row_off, L, n_gate):
    g = gates[row_off:row_off + n_rows, :n_gate]
    return g.reshape(n_rows // L, L, n_gate).transpose(0, 2, 1)


def _layer(xp, xs, c_all, pool_s, c_s, n_s, m_s, lw, tok, tiles):
    (w_ada, b_ada, norm_mix_w, w_in, b_igate, b_fgate, w_pool, pool_scale, mlstm_norm_w, w_out, norm_ffn_w,
     w_router_group, w_router_expert, w_exp_gate, w_exp_up, w_exp_down) = lw
    d = xp.shape[1]
    nheads, dk, dv = c_s.shape[1:]
    pool_in = pool_s.shape[2]
    mqk, mv = nheads * dk, nheads * dv
    offs = (pool_in, pool_in + mqk, pool_in + 2 * mqk, pool_in + 2 * mqk + mv)
    ga_off = offs[3] + mv
    gb_off = ga_off + d
    n_main = gb_off + d
    n_gate = 2 * nheads
    n = tok.n

    mod = _ada(c_all, w_ada, b_ada, tiles)
    w_gates = jnp.pad(w_in[:, n_main:], ((0, 0), (0, LANES - n_gate))).astype(BF16)
    u, gates = _prenorm(xp, xs, mod, norm_mix_w, w_gates, tok, tiles)
    z = _matmul(u, w_in, n_main, tiles)

    zeros_pool = jnp.zeros((tok.nb_p, POOL_BUF, pool_in), F32)
    d_p, pool_new_p = _pool(z, zeros_pool, 0, tok.nb_p, tok.rows_p, 1, tiles.pool_tt, 0)
    d_s, pool_new_s = _pool(z, pool_s, tok.n_prompt, tok.nb_s, tok.rows_s, tiles.pool_bt, tok.rows_s, PAST_LEN)

    bias = jnp.concatenate([b_igate, b_fgate])
    bias_row = jnp.pad(bias, (0, LANES - n_gate)).reshape(1, LANES)
    bias_col = bias.reshape(n_gate, 1)
    nw = mlstm_norm_w.reshape(1, mv)
    lp = tiles.mlstm_l
    zc = jnp.zeros((tok.nb_p, nheads, dk, dv), F32)
    zn = jnp.zeros((tok.nb_p, nheads, 1, dk), F32)
    zm = jnp.zeros((tok.nb_p, nheads, 1, 1), F32)
    mo_p, c_p, n_p, m_p = _mlstm(z, gates, _gate_layouts(gates, tok.n_prompt, 0, lp, n_gate), bias_row, bias_col,
                                 nw, zc, zn, zm, 0, tok.nb_p, tok.rows_p, lp, 1, offs)
    ls = tok.rows_s
    mo_s, c_n, n_n, m_n = _mlstm(z, gates, _gate_layouts(gates, tok.n_sample, tok.n_prompt, ls, n_gate), bias_row,
                                 bias_col, nw, c_s, n_s.reshape(tok.nb_s, nheads, 1, dk),
                                 m_s.reshape(tok.nb_s, nheads, 1, 1), tok.n_prompt, tok.nb_s, ls, ls,
                                 tiles.mlstm_hb_s, offs)

    merged = _merge(d_p, d_s, w_pool.astype(BF16), pool_scale, z, mo_p, mo_s, ga_off, gb_off, tok, tiles)
    acc = _matmul(merged, w_out, d, tiles)

    w_router = jnp.pad(jnp.concatenate([w_router_group, w_router_expert], axis=1),
                       ((0, 0), (0, LANES - N_GROUPS - N_EXPERTS))).astype(BF16)
    x1, u2p, re, rw = _ffn_norm(xp, xs, acc, mod, norm_ffn_w, w_router, tok, tiles)
    rank_out, cnt_out = _rank(re, tiles)
    d_expert = w_exp_gate.shape[2]
    n_blocks = -(-(n * TOP_K) // tiles.moe_bm) + N_EXPERTS
    plan = _dispatch_plan(re, rank_out, cnt_out, n, n_blocks, d_expert, d, tiles)
    xsort = _scatter(plan["dest"], u2p, n_blocks * tiles.moe_bm, tiles)
    hdn = _moe_up(plan, xsort, w_exp_gate, w_exp_up, tiles)
    ysort = _moe_down(plan, hdn, w_exp_down, tiles)
    states = (pool_new_p, c_p, n_p.reshape(tok.nb_p, nheads, dk), m_p.reshape(tok.nb_p, nheads),
              pool_new_s, c_n, n_n.reshape(tok.nb_s, nheads, dk), m_n.reshape(tok.nb_s, nheads))
    return (plan["dest"], ysort, x1, rw, mod), states


def _forward(x_prompt, x_sample, c_prompt, c_sample, state_pool, state_mlstm_C, state_mlstm_n, state_mlstm_m,
             w_ada, b_ada, norm_mix_w, w_in, b_igate, b_fgate, w_pool, pool_scale, mlstm_norm_w, w_out,
             norm_ffn_w, w_router_group, w_router_expert, w_exp_gate, w_exp_up, w_exp_down, final_norm_w,
             tiles=Tiles()):
    nb_p, rows_p, d = x_prompt.shape
    nb_s, rows_s, _ = x_sample.shape
    depth = w_ada.shape[0]
    assert depth == 1, "the merged-token pipeline is written for a single layer"
    tok = Tok(nb_p * rows_p, nb_s * rows_s, rows_p, rows_s, nb_p, nb_s)
    for tm in (tiles.row_tm, tiles.mm_tm, tiles.merge_tm, tiles.rank_tr, tiles.comb_tm):
        assert tok.n_prompt % tm == 0 and tok.n_sample % tm == 0
    xp = x_prompt.reshape(tok.n_prompt, d)
    xs = x_sample.reshape(tok.n_sample, d)
    g = nb_p + nb_s
    g_pad = -(-g // SUBLANES) * SUBLANES
    c_all = jnp.pad(jnp.concatenate([c_prompt, c_sample], axis=0), ((0, g_pad - g), (0, 0)))
    lw = (w_ada[0], b_ada[0], norm_mix_w[0], w_in[0], b_igate[0], b_fgate[0], w_pool[0], pool_scale[0],
          mlstm_norm_w[0], w_out[0], norm_ffn_w[0], w_router_group[0], w_router_expert[0], w_exp_gate[0],
          w_exp_up[0], w_exp_down[0])
    (dest, ysort, x1, rw, mod), st = _layer(xp, xs, c_all, state_pool[0], state_mlstm_C[0], state_mlstm_n[0],
                                            state_mlstm_m[0], lw, tok, tiles)
    y_p, y_s = _combine(dest, ysort, x1, rw, mod, final_norm_w, tok, tiles)
    return (y_p.reshape(x_prompt.shape), y_s.reshape(x_sample.shape)) + tuple(s[None] for s in st)


def kernel(x_prompt, x_sample, c_prompt, c_sample, state_pool, state_mlstm_C, state_mlstm_n, state_mlstm_m,
           w_ada, b_ada, norm_mix_w, w_in, b_igate, b_fgate, w_pool, pool_scale, mlstm_norm_w, w_out,
           norm_ffn_w, w_router_group, w_router_expert, w_exp_gate, w_exp_up, w_exp_down, final_norm_w):
    return _forward(x_prompt, x_sample, c_prompt, c_sample, state_pool, state_mlstm_C, state_mlstm_n,
                    state_mlstm_m, w_ada, b_ada, norm_mix_w, w_in, b_igate, b_fgate, w_pool, pool_scale,
                    mlstm_norm_w, w_out, norm_ffn_w, w_router_group, w_router_expert, w_exp_gate, w_exp_up,
                    w_exp_down, final_norm_w)
```

```python
import functools
from typing import NamedTuple

import jax
import jax.numpy as jnp
from jax import lax
from jax.experimental import pallas as pl
from jax.experimental.pallas import tpu as pltpu

F32, BF16, I32, U32 = jnp.float32, jnp.bfloat16, jnp.int32, jnp.uint32

EPS = 1e-6
PAST_LEN = 16384
POOL_WINDOWS = (2, 4, 8, 16)
POOL_BUF = 15
N_GROUPS = 8
EXP_PER_GROUP = 8
N_EXPERTS = N_GROUPS * EXP_PER_GROUP
EXP_SHIFT = EXP_PER_GROUP.bit_length() - 1
TOP_K = 2

V7X_VMEM_BYTES = 64 * 2**20
LANES = 128
SUBLANES = 8
BF16_ROWS = 16
MIB = 2**20


class Tok(NamedTuple):
    n_prompt: int
    n_sample: int
    rows_p: int
    rows_s: int
    nb_p: int
    nb_s: int

    @property
    def n(self):
        return self.n_prompt + self.n_sample


class Tiles(NamedTuple):
    ada_tn: int = 512
    row_tm: int = 256
    mm_tm: int = 512
    mm_tn: int = 1024
    pool_tt: int = 256
    pool_bt: int = 16
    mlstm_l: int = 256
    mlstm_hb_s: int = 4
    merge_tm: int = 512
    rank_tr: int = 256
    moe_bm: int = 256
    moe_cw: int = 256
    moe_dcw: int = 2048
    comb_tm: int = 256


def _cparams(sem, vmem_bytes):
    limit = int(min(max(vmem_bytes * 5 // 4 + 2 * MIB, 16 * MIB), V7X_VMEM_BYTES - 6 * MIB))
    return pltpu.CompilerParams(dimension_semantics=sem, vmem_limit_bytes=limit)


def _batch_of(row, tok):
    return jnp.where(row < tok.n_prompt, row // tok.rows_p,
                     tok.nb_p + (row - tok.n_prompt) // tok.rows_s)


def _mod_rows(ref, row0, tok, d):
    top = jnp.broadcast_to(ref[pl.ds(_batch_of(row0, tok), 1), :], (SUBLANES, d))
    bot = jnp.broadcast_to(ref[pl.ds(_batch_of(row0 + SUBLANES, tok), 1), :], (SUBLANES, d))
    return jnp.concatenate([top, bot], axis=0)


def _rms(x, w):
    r = lax.rsqrt(jnp.mean(x * x, axis=-1, keepdims=True) + EPS)
    return (x * r) * w


def _log_sigmoid(x):
    return jnp.minimum(x, 0.0) - jnp.log1p(jnp.exp(-jnp.abs(x)))


def _ada_kernel(c_ref, w_ref, b_ref, o_ref):
    c = c_ref[...]
    a = (c * jax.nn.sigmoid(c)).astype(BF16)
    o_ref[...] = jnp.dot(a, w_ref[...].astype(BF16), preferred_element_type=F32) + b_ref[...]


def _ada(c_all, w_ada, b_ada, tiles):
    g, d = c_all.shape
    n = w_ada.shape[1]
    tn = tiles.ada_tn
    vmem = 2 * (g * d * 4 + d * tn * 4 + g * tn * 4) + d * tn * 2
    return pl.pallas_call(
        _ada_kernel,
        grid=(n // tn,),
        in_specs=[pl.BlockSpec((g, d), lambda j: (0, 0)),
                  pl.BlockSpec((d, tn), lambda j: (0, j)),
                  pl.BlockSpec((1, tn), lambda j: (0, j))],
        out_specs=pl.BlockSpec((g, tn), lambda j: (0, j)),
        out_shape=jax.ShapeDtypeStruct((g, n), F32),
        compiler_params=_cparams(("arbitrary",), vmem),
        name="ada",
    )(c_all, w_ada, b_ada.reshape(1, n))


def _dual_specs(tok, tm, d):
    npt = tok.n_prompt // tm
    nst = tok.n_sample // tm
    return (pl.BlockSpec((tm, d), lambda i: (jnp.minimum(i, npt - 1), 0)),
            pl.BlockSpec((tm, d), lambda i: (jnp.clip(i - npt, 0, nst - 1), 0)))


def _prenorm_kernel(xp_ref, xs_ref, sc_ref, sh_ref, nw_ref, wg_ref, u_ref, gates_ref, wgb_ref, *, tok, tm, n_gate):
    i = pl.program_id(0)
    d = u_ref.shape[1]

    @pl.when(i == 0)
    def _():
        wgb_ref[...] = jnp.concatenate([wg_ref[:, :n_gate], jnp.zeros((d, LANES - n_gate), F32)],
                                       axis=1).astype(BF16)

    def run(x_ref):
        def slab(s, carry):
            r = pl.multiple_of(s * BF16_ROWS, BF16_ROWS)
            row0 = i * tm + r
            xn = _rms(x_ref[pl.ds(r, BF16_ROWS), :], nw_ref[...])
            u = xn * (1.0 + _mod_rows(sc_ref, row0, tok, d)) + _mod_rows(sh_ref, row0, tok, d)
            u_ref[pl.ds(r, BF16_ROWS), :] = u.astype(BF16)
            return carry
        lax.fori_loop(0, tm // BF16_ROWS, slab, 0)

    npt = tok.n_prompt // tm
    pl.when(i < npt)(lambda: run(xp_ref))
    pl.when(i >= npt)(lambda: run(xs_ref))
    gates_ref[...] = jnp.dot(u_ref[...], wgb_ref[...], preferred_element_type=F32)


def _prenorm(xp, xs, mod, norm_w, w_in, n_main, n_gate, tok, tiles):
    d = xp.shape[1]
    g = mod.shape[0]
    tm = tiles.row_tm
    assert n_main % LANES == 0 and n_gate <= LANES
    xp_spec, xs_spec = _dual_specs(tok, tm, d)
    vmem = 2 * (2 * tm * d * 4 + 2 * g * d * 4 + d * LANES * 4 + tm * d * 2 + tm * LANES * 4) + d * LANES * 2
    return pl.pallas_call(
        functools.partial(_prenorm_kernel, tok=tok, tm=tm, n_gate=n_gate),
        grid=(tok.n // tm,),
        in_specs=[xp_spec, xs_spec,
                  pl.BlockSpec((g, d), lambda i: (0, 1)),
                  pl.BlockSpec((g, d), lambda i: (0, 0)),
                  pl.BlockSpec((1, d), lambda i: (0, 0)),
                  pl.BlockSpec((d, LANES), lambda i: (0, n_main // LANES))],
        out_specs=[pl.BlockSpec((tm, d), lambda i: (i, 0)),
                   pl.BlockSpec((tm, LANES), lambda i: (i, 0))],
        out_shape=[jax.ShapeDtypeStruct((tok.n, d), BF16),
                   jax.ShapeDtypeStruct((tok.n, LANES), F32)],
        scratch_shapes=[pltpu.VMEM((d, LANES), BF16)],
        compiler_params=_cparams(("arbitrary",), vmem),
        name="prenorm",
    )(xp, xs, mod, mod, norm_w.reshape(1, d), w_in)


def _matmul_kernel(a_ref, w_ref, o_ref, wb_ref):
    @pl.when(pl.program_id(1) == 0)
    def _():
        wb_ref[...] = w_ref[...].astype(BF16)
    o_ref[...] = jnp.dot(a_ref[...], wb_ref[...], preferred_element_type=F32)


def _matmul(a, w, n_cols, tiles):
    m, k = a.shape
    tm, tn = tiles.mm_tm, tiles.mm_tn
    vmem = 2 * (tm * k * 2 + k * tn * 4 + tm * tn * 4) + k * tn * 2
    return pl.pallas_call(
        _matmul_kernel,
        grid=(n_cols // tn, m // tm),
        in_specs=[pl.BlockSpec((tm, k), lambda j, i: (i, 0)),
                  pl.BlockSpec((k, tn), lambda j, i: (0, j))],
        out_specs=pl.BlockSpec((tm, tn), lambda j, i: (i, j)),
        out_shape=jax.ShapeDtypeStruct((m, n_cols), F32),
        scratch_shapes=[pltpu.VMEM((k, tn), BF16)],
        compiler_params=_cparams(("arbitrary", "arbitrary"), vmem),
        name="matmul",
    )(a, w)


def _pool_kernel(p_ref, st_ref, d_ref, new_ref, ext_ref, *, bt, tt, gin, pos0, nt):
    t = pl.program_id(1)
    cdim = ext_ref.shape[2]
    halo = POOL_BUF + 1

    @pl.when(t == 0)
    def _():
        ext_ref[:, 1:halo, :] = st_ref[...]

    @pl.when(t > 0)
    def _():
        ext_ref[:, 0:halo, :] = ext_ref[:, tt:tt + halo, :]

    ext_ref[:, halo:halo + tt, :] = p_ref[...].reshape(bt, tt, cdim)
    pos = pos0 + t * tt + lax.broadcasted_iota(I32, (bt, tt, gin), 1)
    for g, w in enumerate(POOL_WINDOWS):
        cs = slice(g * gin, (g + 1) * gin)
        cur = ext_ref[:, halo:halo + tt, cs]
        acc = cur
        for j in range(1, w):
            acc = acc + ext_ref[:, halo - j:halo - j + tt, cs]
        cnt = jnp.minimum(pos + 1, w).astype(F32)
        d_ref[:, cs] = (acc / cnt - cur).reshape(bt * tt, gin)

    @pl.when(t == nt - 1)
    def _():
        new_ref[...] = ext_ref[:, tt + 1:tt + halo, :]


def _pool(z, state, row_off, nb, t_len, bt, tt, pos0):
    cdim = state.shape[2]
    gin = cdim // len(POOL_WINDOWS)
    nt = t_len // tt
    rows = bt * tt
    rb0 = row_off // rows
    vmem = 2 * (2 * rows * cdim * 4 + 2 * bt * 16 * cdim * 4) + bt * (tt + 16) * cdim * 4 + 8 * rows * gin * 4
    return pl.pallas_call(
        functools.partial(_pool_kernel, bt=bt, tt=tt, gin=gin, pos0=pos0, nt=nt),
        grid=(nb // bt, nt),
        in_specs=[pl.BlockSpec((rows, cdim), lambda b, t: (rb0 + b * nt + t, 0)),
                  pl.BlockSpec((bt, POOL_BUF, cdim), lambda b, t: (b, 0, 0))],
        out_specs=[pl.BlockSpec((rows, cdim), lambda b, t: (b * nt + t, 0)),
                   pl.BlockSpec((bt, POOL_BUF, cdim), lambda b, t: (b, 0, 0))],
        out_shape=[jax.ShapeDtypeStruct((nb * t_len, cdim), F32),
                   jax.ShapeDtypeStruct((nb, POOL_BUF, cdim), F32)],
        scratch_shapes=[pltpu.VMEM((bt, tt + POOL_BUF + 1, cdim), F32)],
        compiler_params=_cparams(("arbitrary", "arbitrary"), vmem),
        name="pool",
    )(z, state)


def _mlstm_kernel(q_ref, k_ref, v_ref, o_ref, gc_ref, gr_ref, brow_ref, bcol_ref, nw_ref, c0_ref, n0_ref, m0_ref,
                  mo_ref, c_out, n_out, m_out, c_s, n_s, m_s, *, L, hb, dk, dv, nc, nheads):
    hblk = pl.program_id(1)
    c = pl.program_id(2)

    @pl.when(c == 0)
    def _():
        c_s[...] = c0_ref[0]
        n_s[...] = n0_ref[0]
        m_s[...] = m0_ref[0]

    gcb = gc_ref[...] + brow_ref[...]
    grb = gr_ref[0] + bcol_ref[...]
    lane = lax.broadcasted_iota(I32, gcb.shape, 1)
    sub = lax.broadcasted_iota(I32, grb.shape, 0)
    rowi = lax.broadcasted_iota(I32, (L, L), 0)
    coli = lax.broadcasted_iota(I32, (L, L), 1)
    tri = rowi >= coli
    scale = dk ** -0.5

    for hh in range(hb):
        h = hblk * hb + hh
        li_col = jnp.sum(jnp.where(lane == h, gcb, 0.0), axis=1, keepdims=True)
        lf_col = _log_sigmoid(jnp.sum(jnp.where(lane == h + nheads, gcb, 0.0), axis=1, keepdims=True))
        li_row = jnp.sum(jnp.where(sub == h, grb, 0.0), axis=0, keepdims=True)
        lf_row = _log_sigmoid(jnp.sum(jnp.where(sub == h + nheads, grb, 0.0), axis=0, keepdims=True))
        b_col = jnp.sum(jnp.where(tri, lf_row, 0.0), axis=1, keepdims=True)
        b_row = jnp.sum(jnp.where(rowi <= coli, lf_col, 0.0), axis=0, keepdims=True)

        q = q_ref[:, hh * dk:(hh + 1) * dk] * scale
        k = k_ref[:, hh * dk:(hh + 1) * dk]
        qb = q.astype(BF16)
        kb = k.astype(BF16)
        vb = v_ref[:, hh * dv:(hh + 1) * dv].astype(BF16)
        cmat = c_s[hh]
        nvec = n_s[hh]
        m_prev = m_s[hh]

        dmat = jnp.where(tri, b_col - b_row + li_row, -jnp.inf)
        inter = b_col + m_prev
        m_t = jnp.maximum(inter, jnp.max(dmat, axis=1, keepdims=True))
        dw = jnp.exp(dmat - m_t)
        iw = jnp.exp(inter - m_t)
        s = lax.dot_general(qb, kb, (((1,), (1,)), ((), ())), preferred_element_type=F32) * dw
        num = (jnp.dot(s.astype(BF16), vb, preferred_element_type=F32)
               + iw * jnp.dot(qb, cmat.astype(BF16), preferred_element_type=F32))
        den = jnp.sum(s, axis=1, keepdims=True) + iw * jnp.sum(q * nvec, axis=1, keepdims=True)
        hval = num / jnp.maximum(jnp.abs(den), jnp.exp(-m_t))

        hs = slice(hh * dv, (hh + 1) * dv)
        hn = _rms(hval, nw_ref[:, hs])
        mo_ref[:, hs] = jax.nn.sigmoid(o_ref[:, hs]) * hn

        b_last = b_col[L - 1:L, :]
        dl_col = b_last - b_col + li_col
        dl_row = b_last - b_row + li_row
        m_new = jnp.maximum(b_last + m_prev, jnp.max(dl_row, axis=1, keepdims=True))
        dec = jnp.exp(b_last + m_prev - m_new)
        kk = jnp.exp(dl_col - m_new) * k
        upd = lax.dot_general(kk.astype(BF16), vb, (((0,), (0,)), ((), ())), preferred_element_type=F32)
        c_s[hh] = dec * cmat + upd
        n_s[hh] = dec * nvec + jnp.sum(kk, axis=0, keepdims=True)
        m_s[hh] = m_new

    @pl.when(c == nc - 1)
    def _():
        c_out[0] = c_s[...]
        n_out[0] = n_s[...]
        m_out[0] = m_s[...]


def _mlstm(z, gates, gates_t, bias_row, bias_col, norm_w, c0, n0, m0, row_off, nb, t_len, L, hb, offs):
    _, nheads, dk, dv = c0.shape
    nc = t_len // L
    rb0 = row_off // L
    nhb = nheads // hb
    q_off, k_off, v_off, o_off = offs
    wq, wv = hb * dk, hb * dv

    def rowblk(b, c):
        return rb0 + b * nc + c

    in_specs = [
        pl.BlockSpec((L, wq), lambda b, h, c: (rowblk(b, c), q_off // wq + h)),
        pl.BlockSpec((L, wq), lambda b, h, c: (rowblk(b, c), k_off // wq + h)),
        pl.BlockSpec((L, wv), lambda b, h, c: (rowblk(b, c), v_off // wv + h)),
        pl.BlockSpec((L, wv), lambda b, h, c: (rowblk(b, c), o_off // wv + h)),
        pl.BlockSpec((L, LANES), lambda b, h, c: (rowblk(b, c), 0)),
        pl.BlockSpec((1, 2 * nheads, L), lambda b, h, c: (b * nc + c, 0, 0)),
        pl.BlockSpec((1, LANES), lambda b, h, c: (0, 0)),
        pl.BlockSpec((2 * nheads, 1), lambda b, h, c: (0, 0)),
        pl.BlockSpec((1, wv), lambda b, h, c: (0, h)),
        pl.BlockSpec((1, hb, dk, dv), lambda b, h, c: (b, h, 0, 0)),
        pl.BlockSpec((1, hb, 1, dk), lambda b, h, c: (b, h, 0, 0)),
        pl.BlockSpec((1, hb, 1, 1), lambda b, h, c: (b, h, 0, 0)),
    ]
    state_bytes = hb * dk * dv * 4
    vmem = 2 * (2 * L * wq * 4 + 3 * L * wv * 4 + 2 * state_bytes) + state_bytes + 24 * L * max(L, dv) * 4
    return pl.pallas_call(
        functools.partial(_mlstm_kernel, L=L, hb=hb, dk=dk, dv=dv, nc=nc, nheads=nheads),
        grid=(nb, nhb, nc),
        in_specs=in_specs,
        out_specs=[pl.BlockSpec((L, wv), lambda b, h, c: (b * nc + c, h)),
                   pl.BlockSpec((1, hb, dk, dv), lambda b, h, c: (b, h, 0, 0)),
                   pl.BlockSpec((1, hb, 1, dk), lambda b, h, c: (b, h, 0, 0)),
                   pl.BlockSpec((1, hb, 1, 1), lambda b, h, c: (b, h, 0, 0))],
        out_shape=[jax.ShapeDtypeStruct((nb * t_len, nheads * dv), F32),
                   jax.ShapeDtypeStruct(c0.shape, F32),
                   jax.ShapeDtypeStruct(n0.shape, F32),
                   jax.ShapeDtypeStruct(m0.shape, F32)],
        scratch_shapes=[pltpu.VMEM((hb, dk, dv), F32), pltpu.VMEM((hb, 1, dk), F32),
                        pltpu.VMEM((hb, 1, 1), F32)],
        compiler_params=_cparams(("arbitrary", "arbitrary", "arbitrary"), vmem),
        name="mlstm",
    )(z, z, z, z, gates, gates_t, bias_row, bias_col, norm_w, c0, n0, m0)


def _merge_kernel(dp_ref, ds_ref, wp_ref, ps_ref, ga_ref, gb_ref, mop_ref, mos_ref, o_ref, *, gin, gout, npt):
    def run(d_ref, mo_ref):
        for gg in range(wp_ref.shape[0]):
            y = jnp.dot(d_ref[:, gg * gin:(gg + 1) * gin].astype(BF16), wp_ref[gg], preferred_element_type=F32)
            cs = slice(gg * gout, (gg + 1) * gout)
            pool_out = y * ps_ref[:, cs]
            merged = jax.nn.sigmoid(ga_ref[:, cs]) * pool_out + jax.nn.sigmoid(gb_ref[:, cs]) * mo_ref[:, cs]
            o_ref[:, cs] = merged.astype(BF16)

    i = pl.program_id(0)
    pl.when(i < npt)(lambda: run(dp_ref, mop_ref))
    pl.when(i >= npt)(lambda: run(ds_ref, mos_ref))


def _merge(d_p, d_s, w_pool_b, pool_scale, z, mo_p, mo_s, ga_off, gb_off, tok, tiles):
    cdim = d_p.shape[1]
    ng, gin, gout = w_pool_b.shape
    dm = ng * gout
    half = dm // 2
    gh = ng // 2
    tm = tiles.merge_tm
    npt = tok.n_prompt // tm
    nst = tok.n_sample // tm

    def prompt_spec(width):
        return pl.BlockSpec((tm, width), lambda i, j: (jnp.minimum(i, npt - 1), jnp.where(i < npt, j, 1)))

    def sample_spec(width):
        return pl.BlockSpec((tm, width), lambda i, j: (jnp.clip(i - npt, 0, nst - 1), jnp.where(i < npt, 0, j)))

    vmem = 2 * (2 * tm * cdim * 2 + gh * gin * gout * 2 + 4 * tm * half * 4 + tm * half * 2) + 6 * tm * gout * 4
    return pl.pallas_call(
        functools.partial(_merge_kernel, gin=gin, gout=gout, npt=npt),
        grid=(tok.n // tm, 2),
        in_specs=[prompt_spec(cdim // 2), sample_spec(cdim // 2),
                  pl.BlockSpec((gh, gin, gout), lambda i, j: (j, 0, 0)),
                  pl.BlockSpec((1, half), lambda i, j: (0, j)),
                  pl.BlockSpec((tm, half), lambda i, j: (i, ga_off // half + j)),
                  pl.BlockSpec((tm, half), lambda i, j: (i, gb_off // half + j)),
                  prompt_spec(half), sample_spec(half)],
        out_specs=pl.BlockSpec((tm, half), lambda i, j: (i, j)),
        out_shape=jax.ShapeDtypeStruct((tok.n, dm), BF16),
        compiler_params=_cparams(("arbitrary", "arbitrary"), vmem),
        name="merge",
    )(d_p, d_s, w_pool_b, pool_scale.reshape(1, dm), z, z, mo_p, mo_s)


def _ffn_norm_kernel(xp_ref, xs_ref, acc_ref, g1_ref, sc_ref, sh_ref, nw_ref, wr_ref,
                     x1_ref, up_ref, re_ref, rw_ref, ub_ref, *, tok, tm):
    i = pl.program_id(0)
    d = x1_ref.shape[1]
    hd = d // 2

    def run(x_ref):
        def slab(s, carry):
            r = pl.multiple_of(s * BF16_ROWS, BF16_ROWS)
            rows = pl.ds(r, BF16_ROWS)
            row0 = i * tm + r
            x1 = x_ref[rows, :] + _mod_rows(g1_ref, row0, tok, d) * acc_ref[rows, :]
            x1_ref[rows, :] = x1
            u = _rms(x1, nw_ref[...]) * (1.0 + _mod_rows(sc_ref, row0, tok, d)) + _mod_rows(sh_ref, row0, tok, d)
            ub = u.astype(BF16)
            ub_ref[rows, :] = ub
            bits = lax.bitcast_convert_type(ub.astype(F32), U32)
            up_ref[rows, :] = (bits[:, hd:] & jnp.uint32(0xFFFF0000)) | (bits[:, :hd] >> 16)
            return carry
        lax.fori_loop(0, tm // BF16_ROWS, slab, 0)

    npt = tok.n_prompt // tm
    pl.when(i < npt)(lambda: run(xp_ref))
    pl.when(i >= npt)(lambda: run(xs_ref))

    lg = jnp.dot(ub_ref[...], wr_ref[...], preferred_element_type=F32)
    lane = lax.broadcasted_iota(I32, lg.shape, 1)
    lanef = lane.astype(F32)
    big = float(LANES)
    is_g = lane < N_GROUPS
    gl = jnp.where(is_g, lg, -jnp.inf)
    gmax = jnp.max(gl, axis=1, keepdims=True)
    grp = jnp.min(jnp.where(gl == gmax, lanef, big), axis=1, keepdims=True)
    p_grp = 1.0 / jnp.sum(jnp.where(is_g, jnp.exp(lg - gmax), 0.0), axis=1, keepdims=True)
    eidx = lane - N_GROUPS
    in_grp = (eidx >= 0) & (eidx < N_EXPERTS) & ((eidx >> EXP_SHIFT).astype(F32) == grp)
    el = jnp.where(in_grp, lg, -jnp.inf)
    v1 = jnp.max(el, axis=1, keepdims=True)
    i1 = jnp.min(jnp.where(el == v1, lanef, big), axis=1, keepdims=True)
    el2 = jnp.where(lanef == i1, -jnp.inf, el)
    v2 = jnp.max(el2, axis=1, keepdims=True)
    i2 = jnp.min(jnp.where(el2 == v2, lanef, big), axis=1, keepdims=True)
    e2 = jnp.exp(v2 - v1)
    w1 = (1.0 / (1.0 + e2)) * p_grp
    w2 = (e2 / (1.0 + e2)) * p_grp
    re_ref[...] = jnp.where(lane == 0, i1 - N_GROUPS, jnp.where(lane == 1, i2 - N_GROUPS, 0.0))
    rw_ref[...] = jnp.where(lane == 0, w1, jnp.where(lane == 1, w2, 0.0))


def _ffn_norm(xp, xs, acc, mod, norm_w, w_router, tok, tiles):
    d = xp.shape[1]
    g = mod.shape[0]
    tm = tiles.row_tm
    xp_spec, xs_spec = _dual_specs(tok, tm, d)
    vmem = (2 * (3 * tm * d * 4 + 3 * g * d * 4 + d * LANES * 2 + tm * d * 4 + tm * d * 2 + 2 * tm * LANES * 4)
            + tm * d * 2 + 16 * tm * LANES * 4)
    return pl.pallas_call(
        functools.partial(_ffn_norm_kernel, tok=tok, tm=tm),
        grid=(tok.n // tm,),
        in_specs=[xp_spec, xs_spec,
                  pl.BlockSpec((tm, d), lambda i: (i, 0)),
                  pl.BlockSpec((g, d), lambda i: (0, 2)),
                  pl.BlockSpec((g, d), lambda i: (0, 4)),
                  pl.BlockSpec((g, d), lambda i: (0, 3)),
                  pl.BlockSpec((1, d), lambda i: (0, 0)),
                  pl.BlockSpec((d, LANES), lambda i: (0, 0))],
        out_specs=[pl.BlockSpec((tm, d), lambda i: (i, 0)),
                   pl.BlockSpec((tm, d // 2), lambda i: (i, 0)),
                   pl.BlockSpec((tm, LANES), lambda i: (i, 0)),
                   pl.BlockSpec((tm, LANES), lambda i: (i, 0))],
        out_shape=[jax.ShapeDtypeStruct((tok.n, d), F32),
                   jax.ShapeDtypeStruct((tok.n, d // 2), U32),
                   jax.ShapeDtypeStruct((tok.n, LANES), F32),
                   jax.ShapeDtypeStruct((tok.n, LANES), F32)],
        scratch_shapes=[pltpu.VMEM((tm, d), BF16)],
        compiler_params=_cparams(("arbitrary",), vmem),
        name="ffn_norm",
    )(xp, xs, acc, mod, mod, mod, norm_w.reshape(1, d), w_router)


def _rank_kernel(re_ref, rank_ref, cnt_ref, carry_ref, *, tr):
    j = pl.program_id(0)
    i = pl.program_id(1)

    @pl.when((j == 0) & (i == 0))
    def _():
        carry_ref[...] = jnp.zeros_like(carry_ref)

    re = re_ref[...]
    lane = lax.broadcasted_iota(I32, re.shape, 1)
    e_col = jnp.sum(jnp.where(lane == j, re, 0.0), axis=1, keepdims=True)
    onehot = lane.astype(F32) == e_col
    rowi = lax.broadcasted_iota(I32, (tr, tr), 0)
    coli = lax.broadcasted_iota(I32, (tr, tr), 1)
    tri = jnp.where(rowi >= coli, 1.0, 0.0).astype(BF16)
    prefix = jnp.dot(tri, jnp.where(onehot, 1.0, 0.0).astype(BF16), preferred_element_type=F32)
    carry = carry_ref[0:1, :]
    rank = jnp.sum(jnp.where(onehot, prefix - 1.0 + carry, 0.0), axis=1, keepdims=True)
    rank_ref[...] = jnp.broadcast_to(rank, rank_ref.shape)
    new_carry = carry + prefix[tr - 1:tr, :]
    carry_ref[...] = jnp.broadcast_to(new_carry, carry_ref.shape)
    cnt_ref[...] = jnp.broadcast_to(new_carry, cnt_ref.shape)


def _rank(re, tiles):
    n = re.shape[0]
    tr = tiles.rank_tr
    nt = n // tr
    return pl.pallas_call(
        functools.partial(_rank_kernel, tr=tr),
        grid=(TOP_K, nt),
        in_specs=[pl.BlockSpec((tr, LANES), lambda j, i: (i, 0))],
        out_specs=[pl.BlockSpec((tr, LANES), lambda j, i: (j * nt + i, 0)),
                   pl.BlockSpec((SUBLANES, LANES), lambda j, i: (0, 0))],
        out_shape=[jax.ShapeDtypeStruct((TOP_K * n, LANES), F32),
                   jax.ShapeDtypeStruct((SUBLANES, LANES), F32)],
        scratch_shapes=[pltpu.VMEM((SUBLANES, LANES), F32)],
        compiler_params=_cparams(("arbitrary", "arbitrary"), 4 * MIB),
        name="rank",
    )(re)


def _invert_kernel(dest_ref, inv_ref, *, n_assign, n_rows):
    def clear(p, carry):
        inv_ref[p] = -1
        return carry
    lax.fori_loop(0, n_rows, clear, 0, unroll=8)

    def put(a, carry):
        inv_ref[dest_ref[a]] = a // TOP_K
        return carry
    lax.fori_loop(0, n_assign, put, 0, unroll=8)


def _invert(dest_flat, n_rows):
    return pl.pallas_call(
        functools.partial(_invert_kernel, n_assign=dest_flat.shape[0], n_rows=n_rows),
        in_specs=[pl.BlockSpec(memory_space=pltpu.SMEM)],
        out_specs=pl.BlockSpec(memory_space=pltpu.SMEM),
        out_shape=jax.ShapeDtypeStruct((n_rows,), I32),
        name="invert",
    )(dest_flat)


def _gather_kernel(nused_ref, inv_ref, src_ref, o_ref, sem, *, bm):
    b = pl.program_id(0)

    def row_copy(t, r):
        return pltpu.make_async_copy(src_ref.at[pl.ds(t, 1)], o_ref.at[pl.ds(r, 1)], sem.at[0])

    @pl.when(b < nused_ref[0])
    def _():
        def issue(r, carry):
            t = inv_ref[r]

            @pl.when(t >= 0)
            def _():
                row_copy(t, r).start()

            @pl.when(t < 0)
            def _():
                o_ref[pl.ds(r, 1), :] = jnp.zeros((1, o_ref.shape[1]), o_ref.dtype)
            return carry
        lax.fori_loop(0, bm, issue, 0)

        def drain(r, carry):
            @pl.when(inv_ref[r] >= 0)
            def _():
                row_copy(0, r).wait()
            return carry
        lax.fori_loop(0, bm, drain, 0)


def _gather(inv, n_used, src, tiles):
    n_rows = inv.shape[0]
    width = src.shape[1]
    bm = tiles.moe_bm
    return pl.pallas_call(
        functools.partial(_gather_kernel, bm=bm),
        grid_spec=pltpu.PrefetchScalarGridSpec(
            num_scalar_prefetch=1,
            grid=(n_rows // bm,),
            in_specs=[pl.BlockSpec((bm,), lambda b, nu: (jnp.minimum(b, nu[0] - 1),), memory_space=pltpu.SMEM),
                      pl.BlockSpec(memory_space=pl.ANY)],
            out_specs=pl.BlockSpec((bm, width), lambda b, nu: (jnp.minimum(b, nu[0] - 1), 0)),
            scratch_shapes=[pltpu.SemaphoreType.DMA((1,))]),
        out_shape=jax.ShapeDtypeStruct((n_rows, width), src.dtype),
        compiler_params=_cparams(("arbitrary",), 4 * bm * width * 4),
        name="gather",
    )(n_used, inv, src)


def _unpack_pair(words):
    lo = lax.bitcast_convert_type(words << 16, F32).astype(BF16)
    hi = lax.bitcast_convert_type(words & jnp.uint32(0xFFFF0000), F32).astype(BF16)
    return lo, hi


def _moe_up_kernel(blk_ref, col_ref, exp_ref, first_ref, valid_ref, xs_ref, wg_ref, wu_ref, h_ref, wgb, wub):
    w = pl.program_id(0)
    hd = xs_ref.shape[1]

    @pl.when(first_ref[w] == 1)
    def _():
        wgb[...] = wg_ref[...].astype(BF16)
        wub[...] = wu_ref[...].astype(BF16)

    @pl.when(valid_ref[w] == 1)
    def _():
        lo, hi = _unpack_pair(xs_ref[...])
        g = (jnp.dot(lo, wgb[:hd, :], preferred_element_type=F32)
             + jnp.dot(hi, wgb[hd:, :], preferred_element_type=F32))
        u = (jnp.dot(lo, wub[:hd, :], preferred_element_type=F32)
             + jnp.dot(hi, wub[hd:, :], preferred_element_type=F32))
        h_ref[...] = ((g * jax.nn.sigmoid(g)) * u).astype(BF16)


def _moe_up(plan, xs, w_gate, w_up, tiles):
    p_rows, hd = xs.shape
    _, d, de = w_gate.shape
    bm, cw = tiles.moe_bm, tiles.moe_cw
    n_items = plan["up"][0].shape[0]
    vmem = 2 * (bm * hd * 4 + 2 * d * cw * 4 + bm * cw * 2) + 2 * d * cw * 2 + 8 * bm * cw * 4 + 2 * bm * d * 2
    return pl.pallas_call(
        _moe_up_kernel,
        grid_spec=pltpu.PrefetchScalarGridSpec(
            num_scalar_prefetch=5,
            grid=(n_items,),
            in_specs=[pl.BlockSpec((bm, hd), lambda w, blk, col, ex, fi, va: (blk[w], 0)),
                      pl.BlockSpec((None, d, cw), lambda w, blk, col, ex, fi, va: (ex[w], 0, col[w])),
                      pl.BlockSpec((None, d, cw), lambda w, blk, col, ex, fi, va: (ex[w], 0, col[w]))],
            out_specs=pl.BlockSpec((bm, cw), lambda w, blk, col, ex, fi, va: (blk[w], col[w])),
            scratch_shapes=[pltpu.VMEM((d, cw), BF16), pltpu.VMEM((d, cw), BF16)]),
        out_shape=jax.ShapeDtypeStruct((p_rows, de), BF16),
        compiler_params=_cparams(("arbitrary",), vmem),
        name="moe_up",
    )(*plan["up"], xs, w_gate, w_up)


def _moe_down_kernel(blk_ref, col_ref, exp_ref, first_ref, valid_ref, h_ref, wd_ref, y_ref, wdb):
    w = pl.program_id(0)

    @pl.when(first_ref[w] == 1)
    def _():
        wdb[...] = wd_ref[...].astype(BF16)

    @pl.when(valid_ref[w] == 1)
    def _():
        y_ref[...] = jnp.dot(h_ref[...], wdb[...], preferred_element_type=F32)


def _moe_down(plan, hdn, w_down, tiles):
    p_rows, de = hdn.shape
    d = w_down.shape[2]
    bm, dcw = tiles.moe_bm, tiles.moe_dcw
    n_items = plan["down"][0].shape[0]
    vmem = 2 * (bm * de * 2 + de * dcw * 4 + bm * dcw * 4) + de * dcw * 2 + 2 * bm * dcw * 4
    return pl.pallas_call(
        _moe_down_kernel,
        grid_spec=pltpu.PrefetchScalarGridSpec(
            num_scalar_prefetch=5,
            grid=(n_items,),
            in_specs=[pl.BlockSpec((bm, de), lambda w, blk, col, ex, fi, va: (blk[w], 0)),
                      pl.BlockSpec((None, de, dcw), lambda w, blk, col, ex, fi, va: (ex[w], 0, col[w]))],
            out_specs=pl.BlockSpec((bm, dcw), lambda w, blk, col, ex, fi, va: (blk[w], col[w])),
            scratch_shapes=[pltpu.VMEM((de, dcw), BF16)]),
        out_shape=jax.ShapeDtypeStruct((p_rows, d), F32),
        compiler_params=_cparams(("arbitrary",), vmem),
        name="moe_down",
    )(*plan["down"], hdn, w_down)


def _work_items(nb_e, blk0_e, n_cols, n_items):
    items_e = nb_e * n_cols
    cum = jnp.cumsum(items_e)
    total = cum[-1]
    w = jnp.minimum(jnp.arange(n_items, dtype=I32), total - 1)
    own = (w[:, None] >= (cum - items_e)[None, :]) & (w[:, None] < cum[None, :])

    def pick(v):
        return jnp.sum(jnp.where(own, v[None, :], 0), axis=1)

    e = pick(jnp.arange(N_EXPERTS, dtype=I32))
    r = w - pick(cum - items_e)
    nb = jnp.maximum(pick(nb_e), 1)
    col = r // nb
    blk = pick(blk0_e) + r % nb
    valid = (jnp.arange(n_items, dtype=I32) < total).astype(I32)
    key = e * n_cols + col
    first = jnp.concatenate([jnp.ones((1,), I32), (key[1:] != key[:-1]).astype(I32)]) * valid
    return blk.astype(I32), col.astype(I32), e, first.astype(I32), valid


def _dispatch_plan(re, rank_out, cnt_out, n, n_blocks, d_expert, d_model, tiles):
    bm = tiles.moe_bm
    eid = re[:, :TOP_K].astype(I32)
    rank = rank_out[:, 0].reshape(TOP_K, n).T.astype(I32)
    counts = cnt_out[0, :N_EXPERTS].astype(I32)
    nb_e = (counts + bm - 1) // bm
    blk0_e = jnp.cumsum(nb_e) - nb_e
    row0 = jnp.sum(jnp.where(eid[:, :, None] == jnp.arange(N_EXPERTS, dtype=I32), blk0_e * bm, 0), axis=2)
    dest = row0 + rank
    plan = {
        "dest": dest.reshape(-1).astype(I32),
        "n_used": jnp.sum(nb_e).reshape(1).astype(I32),
        "up": _work_items(nb_e, blk0_e, d_expert // tiles.moe_cw, n_blocks * (d_expert // tiles.moe_cw)),
        "down": _work_items(nb_e, blk0_e, d_model // tiles.moe_dcw, n_blocks * (d_model // tiles.moe_dcw)),
    }
    return plan


def _combine_kernel(dest_ref, ys_ref, x1_ref, rw_ref, g2_ref, fw_ref, yp_ref, ysm_ref, ybuf, sem, *, tok, tm):
    i = pl.program_id(0)
    d = x1_ref.shape[1]
    base = i * tm

    def row_copy(r, j):
        return pltpu.make_async_copy(ys_ref.at[pl.ds(dest_ref[(base + r) * TOP_K + j], 1)],
                                     ybuf.at[j, pl.ds(r, 1)], sem.at[0])

    def issue(r, carry):
        for j in range(TOP_K):
            row_copy(r, j).start()
        return carry
    lax.fori_loop(0, tm, issue, 0)

    def drain(r, carry):
        for j in range(TOP_K):
            row_copy(r, j).wait()
        return carry
    lax.fori_loop(0, tm, drain, 0)

    def run(out_ref):
        def slab(s, carry):
            r = pl.multiple_of(s * SUBLANES, SUBLANES)
            rows = pl.ds(r, SUBLANES)
            wts = rw_ref[rows, :]
            ff = wts[:, 0:1] * ybuf[0, rows, :] + wts[:, 1:2] * ybuf[1, rows, :]
            g2 = jnp.broadcast_to(g2_ref[pl.ds(_batch_of(base + r, tok), 1), :], (SUBLANES, d))
            out_ref[rows, :] = _rms(x1_ref[rows, :] + g2 * ff, fw_ref[...])
            return carry
        lax.fori_loop(0, tm // SUBLANES, slab, 0)

    npt = tok.n_prompt // tm
    pl.when(i < npt)(lambda: run(yp_ref))
    pl.when(i >= npt)(lambda: run(ysm_ref))


def _combine(dest_flat, ys, x1, rw, mod, final_w, tok, tiles):
    d = x1.shape[1]
    g = mod.shape[0]
    tm = tiles.comb_tm
    npt = tok.n_prompt // tm
    nst = tok.n_sample // tm
    vmem = 2 * (tm * d * 4 + tm * LANES * 4 + g * d * 4 + 2 * tm * d * 4) + TOP_K * tm * d * 4
    return pl.pallas_call(
        functools.partial(_combine_kernel, tok=tok, tm=tm),
        grid_spec=pltpu.PrefetchScalarGridSpec(
            num_scalar_prefetch=1,
            grid=(tok.n // tm,),
            in_specs=[pl.BlockSpec(memory_space=pl.ANY),
                      pl.BlockSpec((tm, d), lambda i, dest: (i, 0)),
                      pl.BlockSpec((tm, LANES), lambda i, dest: (i, 0)),
                      pl.BlockSpec((g, d), lambda i, dest: (0, 5)),
                      pl.BlockSpec((1, d), lambda i, dest: (0, 0))],
            out_specs=[pl.BlockSpec((tm, d), lambda i, dest: (jnp.minimum(i, npt - 1), 0)),
                       pl.BlockSpec((tm, d), lambda i, dest: (jnp.clip(i - npt, 0, nst - 1), 0))],
            scratch_shapes=[pltpu.VMEM((TOP_K, tm, d), F32), pltpu.SemaphoreType.DMA((1,))]),
        out_shape=[jax.ShapeDtypeStruct((tok.n_prompt, d), F32),
                   jax.ShapeDtypeStruct((tok.n_sample, d), F32)],
        compiler_params=_cparams(("arbitrary",), vmem),
        name="combine",
    )(dest_flat, ys, x1, rw, mod, final_w.reshape(1, d))


def _gate_layouts(gates, n_rows, row_off, L, n_gate):
    g = gates[row_off:row_off + n_rows, :n_gate]
    return g.reshape(n_rows // L, L, n_gate).transpose(0, 2, 1)


def _layer(xp, xs, c_all, pool_s, c_s, n_s, m_s, lw, tok, tiles):
    (w_ada, b_ada, norm_mix_w, w_in, b_igate, b_fgate, w_pool, pool_scale, mlstm_norm_w, w_out, norm_ffn_w,
     w_router_group, w_router_expert, w_exp_gate, w_exp_up, w_exp_down) = lw
    d = xp.shape[1]
    nheads, dk, dv = c_s.shape[1:]
    pool_in = pool_s.shape[2]
    mqk, mv = nheads * dk, nheads * dv
    offs = (pool_in, pool_in + mqk, pool_in + 2 * mqk, pool_in + 2 * mqk + mv)
    ga_off = offs[3] + mv
    gb_off = ga_off + d
    n_main = gb_off + d
    n_gate = 2 * nheads
    n = tok.n

    mod = _ada(c_all, w_ada, b_ada, tiles)
    u, gates = _prenorm(xp, xs, mod, norm_mix_w, w_in, n_main, n_gate, tok, tiles)
    z = _matmul(u, w_in, n_main, tiles)

    zeros_pool = jnp.zeros((tok.nb_p, POOL_BUF, pool_in), F32)
    d_p, pool_new_p = _pool(z, zeros_pool, 0, tok.nb_p, tok.rows_p, 1, tiles.pool_tt, 0)
    d_s, pool_new_s = _pool(z, pool_s, tok.n_prompt, tok.nb_s, tok.rows_s, tiles.pool_bt, tok.rows_s, PAST_LEN)

    bias = jnp.concatenate([b_igate, b_fgate])
    bias_row = jnp.pad(bias, (0, LANES - n_gate)).reshape(1, LANES)
    bias_col = bias.reshape(n_gate, 1)
    nw = mlstm_norm_w.reshape(1, mv)
    lp = tiles.mlstm_l
    zc = jnp.zeros((tok.nb_p, nheads, dk, dv), F32)
    zn = jnp.zeros((tok.nb_p, nheads, 1, dk), F32)
    zm = jnp.zeros((tok.nb_p, nheads, 1, 1), F32)
    mo_p, c_p, n_p, m_p = _mlstm(z, gates, _gate_layouts(gates, tok.n_prompt, 0, lp, n_gate), bias_row, bias_col,
                                 nw, zc, zn, zm, 0, tok.nb_p, tok.rows_p, lp, 1, offs)
    ls = tok.rows_s
    mo_s, c_n, n_n, m_n = _mlstm(z, gates, _gate_layouts(gates, tok.n_sample, tok.n_prompt, ls, n_gate), bias_row,
                                 bias_col, nw, c_s, n_s.reshape(tok.nb_s, nheads, 1, dk),
                                 m_s.reshape(tok.nb_s, nheads, 1, 1), tok.n_prompt, tok.nb_s, ls, ls,
                                 tiles.mlstm_hb_s, offs)

    merged = _merge(d_p, d_s, w_pool.astype(BF16), pool_scale, z, mo_p, mo_s, ga_off, gb_off, tok, tiles)
    acc = _matmul(merged, w_out, d, tiles)

    w_router = jnp.pad(jnp.concatenate([w_router_group, w_router_expert], axis=1),
                       ((0, 0), (0, LANES - N_GROUPS - N_EXPERTS))).astype(BF16)
    x1, u2p, re, rw = _ffn_norm(xp, xs, acc, mod, norm_ffn_w, w_router, tok, tiles)
    rank_out, cnt_out = _rank(re, tiles)
    d_expert = w_exp_gate.shape[2]
    n_blocks = -(-(n * TOP_K) // tiles.moe_bm) + N_EXPERTS
    plan = _dispatch_plan(re, rank_out, cnt_out, n, n_blocks, d_expert, d, tiles)
    inv = _invert(plan["dest"], n_blocks * tiles.moe_bm)
    xsort = _gather(inv, plan["n_used"], u2p, tiles)
    hdn = _moe_up(plan, xsort, w_exp_gate, w_exp_up, tiles)
    ysort = _moe_down(plan, hdn, w_exp_down, tiles)
    states = (pool_new_p, c_p, n_p.reshape(tok.nb_p, nheads, dk), m_p.reshape(tok.nb_p, nheads),
              pool_new_s, c_n, n_n.reshape(tok.nb_s, nheads, dk), m_n.reshape(tok.nb_s, nheads))
    return (plan["dest"], ysort, x1, rw, mod), states


def _forward(x_prompt, x_sample, c_prompt, c_sample, state_pool, state_mlstm_C, state_mlstm_n, state_mlstm_m,
             w_ada, b_ada, norm_mix_w, w_in, b_igate, b_fgate, w_pool, pool_scale, mlstm_norm_w, w_out,
             norm_ffn_w, w_router_group, w_router_expert, w_exp_gate, w_exp_up, w_exp_down, final_norm_w,
             tiles=Tiles()):
    nb_p, rows_p, d = x_prompt.shape
    nb_s, rows_s, _ = x_sample.shape
    depth = w_ada.shape[0]
    assert depth == 1, "the merged-token pipeline is written for a single layer"
    tok = Tok(nb_p * rows_p, nb_s * rows_s, rows_p, rows_s, nb_p, nb_s)
    for tm in (tiles.row_tm, tiles.mm_tm, tiles.merge_tm, tiles.rank_tr, tiles.comb_tm):
        assert tok.n_prompt % tm == 0 and tok.n_sample % tm == 0
    xp = x_prompt.reshape(tok.n_prompt, d)
    xs = x_sample.reshape(tok.n_sample, d)
    g = nb_p + nb_s
    g_pad = -(-g // SUBLANES) * SUBLANES
    c_all = jnp.pad(jnp.concatenate([c_prompt, c_sample], axis=0), ((0, g_pad - g), (0, 0)))
    lw = (w_ada[0], b_ada[0], norm_mix_w[0], w_in[0], b_igate[0], b_fgate[0], w_pool[0], pool_scale[0],
          mlstm_norm_w[0], w_out[0], norm_ffn_w[0], w_router_group[0], w_router_expert[0], w_exp_gate[0],
          w_exp_up[0], w_exp_down[0])
    (dest, ysort, x1, rw, mod), st = _layer(xp, xs, c_all, state_pool[0], state_mlstm_C[0], state_mlstm_n[0],
                                            state_mlstm_m[0], lw, tok, tiles)
    y_p, y_s = _combine(dest, ysort, x1, rw, mod, final_norm_w, tok, tiles)
    return (y_p.reshape(x_prompt.shape), y_s.reshape(x_sample.shape)) + tuple(s[None] for s in st)


def kernel(x_prompt, x_sample, c_prompt, c_sample, state_pool, state_mlstm_C, state_mlstm_n, state_mlstm_m,
           w_ada, b_ada, norm_mix_w, w_in, b_igate, b_fgate, w_pool, pool_scale, mlstm_norm_w, w_out,
           norm_ffn_w, w_router_group, w_router_expert, w_exp_gate, w_exp_up, w_exp_down, final_norm_w):
    return _forward(x_prompt, x_sample, c_prompt, c_sample, state_pool, state_mlstm_C, state_mlstm_n,
                    state_mlstm_m, w_ada, b_ada, norm_mix_w, w_in, b_igate, b_fgate, w_pool, pool_scale,
                    mlstm_norm_w, w_out, norm_ffn_w, w_router_group, w_router_expert, w_exp_gate, w_exp_up,
                    w_exp_down, final_norm_w)
```

```python
import functools
from typing import NamedTuple

import jax
import jax.numpy as jnp
from jax import lax
from jax.experimental import pallas as pl
from jax.experimental.pallas import tpu as pltpu

F32, BF16, I32, U32 = jnp.float32, jnp.bfloat16, jnp.int32, jnp.uint32

EPS = 1e-6
PAST_LEN = 16384
POOL_WINDOWS = (2, 4, 8, 16)
POOL_BUF = 15
N_GROUPS = 8
EXP_PER_GROUP = 8
N_EXPERTS = N_GROUPS * EXP_PER_GROUP
EXP_SHIFT = EXP_PER_GROUP.bit_length() - 1
TOP_K = 2

V7X_VMEM_BYTES = 64 * 2**20
LANES = 128
SUBLANES = 8
BF16_ROWS = 16
MIB = 2**20


class Tok(NamedTuple):
    n_prompt: int
    n_sample: int
    rows_p: int
    rows_s: int
    nb_p: int
    nb_s: int

    @property
    def n(self):
        return self.n_prompt + self.n_sample


class Tiles(NamedTuple):
    ada_tn: int = 512
    row_tm: int = 256
    mm_tm: int = 512
    mm_tn: int = 1024
    pool_tt: int = 256
    pool_bt: int = 16
    mlstm_l: int = 256
    mlstm_hb_p: int = 2
    mlstm_hb_s: int = 4
    merge_tm: int = 512
    rank_tr: int = 256
    moe_bm: int = 256
    moe_cw: int = 512
    moe_dcw: int = 2048
    comb_tm: int = 256


def _cparams(sem, vmem_bytes):
    limit = int(min(max(vmem_bytes * 5 // 4 + 2 * MIB, 16 * MIB), V7X_VMEM_BYTES - 6 * MIB))
    return pltpu.CompilerParams(dimension_semantics=sem, vmem_limit_bytes=limit)


def _batch_of(row, tok):
    return jnp.where(row < tok.n_prompt, row // tok.rows_p,
                     tok.nb_p + (row - tok.n_prompt) // tok.rows_s)


def _mod_rows(ref, row0, tok, d):
    top = jnp.broadcast_to(ref[pl.ds(_batch_of(row0, tok), 1), :], (SUBLANES, d))
    bot = jnp.broadcast_to(ref[pl.ds(_batch_of(row0 + SUBLANES, tok), 1), :], (SUBLANES, d))
    return jnp.concatenate([top, bot], axis=0)


def _rms(x, w):
    r = lax.rsqrt(jnp.mean(x * x, axis=-1, keepdims=True) + EPS)
    return (x * r) * w


def _log_sigmoid(x):
    return jnp.minimum(x, 0.0) - jnp.log1p(jnp.exp(-jnp.abs(x)))


def _ada_kernel(c_ref, w_ref, b_ref, o_ref):
    c = c_ref[...]
    a = (c * jax.nn.sigmoid(c)).astype(BF16)
    o_ref[...] = jnp.dot(a, w_ref[...].astype(BF16), preferred_element_type=F32) + b_ref[...]


def _ada(c_all, w_ada, b_ada, tiles):
    g, d = c_all.shape
    n = w_ada.shape[1]
    tn = tiles.ada_tn
    vmem = 2 * (g * d * 4 + d * tn * 4 + g * tn * 4) + d * tn * 2
    return pl.pallas_call(
        _ada_kernel,
        grid=(n // tn,),
        in_specs=[pl.BlockSpec((g, d), lambda j: (0, 0)),
                  pl.BlockSpec((d, tn), lambda j: (0, j)),
                  pl.BlockSpec((1, tn), lambda j: (0, j))],
        out_specs=pl.BlockSpec((g, tn), lambda j: (0, j)),
        out_shape=jax.ShapeDtypeStruct((g, n), F32),
        compiler_params=_cparams(("arbitrary",), vmem),
        name="ada",
    )(c_all, w_ada, b_ada.reshape(1, n))


def _dual_specs(tok, tm, d):
    npt = tok.n_prompt // tm
    nst = tok.n_sample // tm
    return (pl.BlockSpec((tm, d), lambda i: (jnp.minimum(i, npt - 1), 0)),
            pl.BlockSpec((tm, d), lambda i: (jnp.clip(i - npt, 0, nst - 1), 0)))


def _prenorm_kernel(xp_ref, xs_ref, sc_ref, sh_ref, nw_ref, wg_ref, u_ref, gates_ref, wgb_ref, *, tok, tm, n_gate):
    i = pl.program_id(0)
    d = u_ref.shape[1]

    @pl.when(i == 0)
    def _():
        wgb_ref[0:n_gate, :] = wg_ref[...].astype(BF16)
        wgb_ref[n_gate:, :] = jnp.zeros((LANES - n_gate, d), BF16)

    def run(x_ref):
        def slab(s, carry):
            r = pl.multiple_of(s * BF16_ROWS, BF16_ROWS)
            row0 = i * tm + r
            xn = _rms(x_ref[pl.ds(r, BF16_ROWS), :], nw_ref[...])
            u = xn * (1.0 + _mod_rows(sc_ref, row0, tok, d)) + _mod_rows(sh_ref, row0, tok, d)
            u_ref[pl.ds(r, BF16_ROWS), :] = u.astype(BF16)
            return carry
        lax.fori_loop(0, tm // BF16_ROWS, slab, 0)

    npt = tok.n_prompt // tm
    pl.when(i < npt)(lambda: run(xp_ref))
    pl.when(i >= npt)(lambda: run(xs_ref))
    gates_ref[...] = lax.dot_general(u_ref[...], wgb_ref[...], (((1,), (1,)), ((), ())),
                                     preferred_element_type=F32)


def _prenorm(xp, xs, mod, norm_w, w_in_t, n_main, n_gate, tok, tiles):
    d = xp.shape[1]
    g = mod.shape[0]
    tm = tiles.row_tm
    assert n_main % n_gate == 0 and n_gate % BF16_ROWS == 0 and n_gate <= LANES
    xp_spec, xs_spec = _dual_specs(tok, tm, d)
    vmem = 2 * (2 * tm * d * 4 + 2 * g * d * 4 + n_gate * d * 4 + tm * d * 2 + tm * LANES * 4) + d * LANES * 2
    return pl.pallas_call(
        functools.partial(_prenorm_kernel, tok=tok, tm=tm, n_gate=n_gate),
        grid=(tok.n // tm,),
        in_specs=[xp_spec, xs_spec,
                  pl.BlockSpec((g, d), lambda i: (0, 1)),
                  pl.BlockSpec((g, d), lambda i: (0, 0)),
                  pl.BlockSpec((1, d), lambda i: (0, 0)),
                  pl.BlockSpec((n_gate, d), lambda i: (n_main // n_gate, 0))],
        out_specs=[pl.BlockSpec((tm, d), lambda i: (i, 0)),
                   pl.BlockSpec((tm, LANES), lambda i: (i, 0))],
        out_shape=[jax.ShapeDtypeStruct((tok.n, d), BF16),
                   jax.ShapeDtypeStruct((tok.n, LANES), F32)],
        scratch_shapes=[pltpu.VMEM((LANES, d), BF16)],
        compiler_params=_cparams(("arbitrary",), vmem),
        name="prenorm",
    )(xp, xs, mod, mod, norm_w.reshape(1, d), w_in_t)


def _matmul_kernel(a_ref, w_ref, o_ref, wb_ref, *, w_is_nk):
    @pl.when(pl.program_id(1) == 0)
    def _():
        wb_ref[...] = w_ref[...].astype(BF16)
    contract_w = 1 if w_is_nk else 0
    o_ref[...] = lax.dot_general(a_ref[...], wb_ref[...], (((1,), (contract_w,)), ((), ())),
                                 preferred_element_type=F32)


def _matmul(a, w, n_cols, w_is_nk, tiles):
    m, k = a.shape
    tm, tn = tiles.mm_tm, tiles.mm_tn
    w_block = (tn, k) if w_is_nk else (k, tn)
    w_index = (lambda j, i: (j, 0)) if w_is_nk else (lambda j, i: (0, j))
    vmem = 2 * (tm * k * 2 + k * tn * 4 + tm * tn * 4) + k * tn * 2
    return pl.pallas_call(
        functools.partial(_matmul_kernel, w_is_nk=w_is_nk),
        grid=(n_cols // tn, m // tm),
        in_specs=[pl.BlockSpec((tm, k), lambda j, i: (i, 0)),
                  pl.BlockSpec(w_block, w_index)],
        out_specs=pl.BlockSpec((tm, tn), lambda j, i: (i, j)),
        out_shape=jax.ShapeDtypeStruct((m, n_cols), F32),
        scratch_shapes=[pltpu.VMEM(w_block, BF16)],
        compiler_params=_cparams(("arbitrary", "arbitrary"), vmem),
        name="matmul",
    )(a, w)


def _pool_kernel(p_ref, st_ref, d_ref, new_ref, ext_ref, *, bt, tt, gin, pos0, nt):
    t = pl.program_id(1)
    cdim = ext_ref.shape[2]
    halo = POOL_BUF + 1

    @pl.when(t == 0)
    def _():
        ext_ref[:, 1:halo, :] = st_ref[...]

    @pl.when(t > 0)
    def _():
        ext_ref[:, 0:halo, :] = ext_ref[:, tt:tt + halo, :]

    ext_ref[:, halo:halo + tt, :] = p_ref[...].reshape(bt, tt, cdim)
    pos = pos0 + t * tt + lax.broadcasted_iota(I32, (bt, tt, gin), 1)
    for g, w in enumerate(POOL_WINDOWS):
        cs = slice(g * gin, (g + 1) * gin)
        cur = ext_ref[:, halo:halo + tt, cs]
        acc = cur
        for j in range(1, w):
            acc = acc + ext_ref[:, halo - j:halo - j + tt, cs]
        cnt = jnp.minimum(pos + 1, w).astype(F32)
        d_ref[:, cs] = (acc / cnt - cur).reshape(bt * tt, gin)

    @pl.when(t == nt - 1)
    def _():
        new_ref[...] = ext_ref[:, tt + 1:tt + halo, :]


def _pool(z, state, row_off, nb, t_len, bt, tt, pos0):
    cdim = state.shape[2]
    gin = cdim // len(POOL_WINDOWS)
    nt = t_len // tt
    rows = bt * tt
    rb0 = row_off // rows
    vmem = 2 * (2 * rows * cdim * 4 + 2 * bt * 16 * cdim * 4) + bt * (tt + 16) * cdim * 4 + 8 * rows * gin * 4
    return pl.pallas_call(
        functools.partial(_pool_kernel, bt=bt, tt=tt, gin=gin, pos0=pos0, nt=nt),
        grid=(nb // bt, nt),
        in_specs=[pl.BlockSpec((rows, cdim), lambda b, t: (rb0 + b * nt + t, 0)),
                  pl.BlockSpec((bt, POOL_BUF, cdim), lambda b, t: (b, 0, 0))],
        out_specs=[pl.BlockSpec((rows, cdim), lambda b, t: (b * nt + t, 0)),
                   pl.BlockSpec((bt, POOL_BUF, cdim), lambda b, t: (b, 0, 0))],
        out_shape=[jax.ShapeDtypeStruct((nb * t_len, cdim), F32),
                   jax.ShapeDtypeStruct((nb, POOL_BUF, cdim), F32)],
        scratch_shapes=[pltpu.VMEM((bt, tt + POOL_BUF + 1, cdim), F32)],
        compiler_params=_cparams(("arbitrary", "arbitrary"), vmem),
        name="pool",
    )(z, state)


def _mlstm_kernel(q_ref, k_ref, v_ref, o_ref, gc_ref, gr_ref, brow_ref, bcol_ref, nw_ref, c0_ref, n0_ref, m0_ref,
                  mo_ref, c_out, n_out, m_out, c_s, n_s, m_s, *, L, hb, dk, dv, nc, nheads):
    hblk = pl.program_id(1)
    c = pl.program_id(2)

    @pl.when(c == 0)
    def _():
        c_s[...] = c0_ref[0]
        n_s[...] = n0_ref[0]
        m_s[...] = m0_ref[0]

    gcb = gc_ref[...] + brow_ref[...]
    grb = gr_ref[0] + bcol_ref[...]
    lane = lax.broadcasted_iota(I32, gcb.shape, 1)
    sub = lax.broadcasted_iota(I32, grb.shape, 0)
    rowi = lax.broadcasted_iota(I32, (L, L), 0)
    coli = lax.broadcasted_iota(I32, (L, L), 1)
    tri = rowi >= coli
    scale = dk ** -0.5

    heads = range(hb)
    if hb == nheads:
        li_col = [gcb[:, hh:hh + 1] for hh in heads]
        lf_col = [gcb[:, nheads + hh:nheads + hh + 1] for hh in heads]
        li_row = [grb[hh:hh + 1, :] for hh in heads]
        lf_row = [grb[nheads + hh:nheads + hh + 1, :] for hh in heads]
    else:
        hs_dyn = [hblk * hb + hh for hh in heads]
        li_col = [jnp.sum(jnp.where(lane == h, gcb, 0.0), axis=1, keepdims=True) for h in hs_dyn]
        lf_col = [jnp.sum(jnp.where(lane == h + nheads, gcb, 0.0), axis=1, keepdims=True) for h in hs_dyn]
        li_row = [jnp.sum(jnp.where(sub == h, grb, 0.0), axis=0, keepdims=True) for h in hs_dyn]
        lf_row = [jnp.sum(jnp.where(sub == h + nheads, grb, 0.0), axis=0, keepdims=True) for h in hs_dyn]
    lf_col = [_log_sigmoid(x) for x in lf_col]
    lf_row = [_log_sigmoid(x) for x in lf_row]
    b_col = [jnp.sum(jnp.where(tri, x, 0.0), axis=1, keepdims=True) for x in lf_row]
    b_row = [jnp.sum(jnp.where(rowi <= coli, x, 0.0), axis=0, keepdims=True) for x in lf_col]

    q = [q_ref[:, hh * dk:(hh + 1) * dk] * scale for hh in heads]
    k = [k_ref[:, hh * dk:(hh + 1) * dk] for hh in heads]
    qb = [x.astype(BF16) for x in q]
    kb = [x.astype(BF16) for x in k]
    vb = [v_ref[:, hh * dv:(hh + 1) * dv].astype(BF16) for hh in heads]
    cmat = [c_s[hh] for hh in heads]
    nvec = [n_s[hh] for hh in heads]
    m_prev = [m_s[hh] for hh in heads]

    dmat = [jnp.where(tri, b_col[i] - b_row[i] + li_row[i], -jnp.inf) for i in heads]
    inter = [b_col[i] + m_prev[i] for i in heads]
    m_t = [jnp.maximum(inter[i], jnp.max(dmat[i], axis=1, keepdims=True)) for i in heads]
    dw = [jnp.exp(dmat[i] - m_t[i]) for i in heads]
    iw = [jnp.exp(inter[i] - m_t[i]) for i in heads]
    qk = [lax.dot_general(qb[i], kb[i], (((1,), (1,)), ((), ())), preferred_element_type=F32) for i in heads]
    qc = [jnp.dot(qb[i], cmat[i].astype(BF16), preferred_element_type=F32) for i in heads]
    s = [qk[i] * dw[i] for i in heads]
    num = [jnp.dot(s[i].astype(BF16), vb[i], preferred_element_type=F32) + iw[i] * qc[i] for i in heads]
    den = [jnp.sum(s[i], axis=1, keepdims=True) + iw[i] * jnp.sum(q[i] * nvec[i], axis=1, keepdims=True)
           for i in heads]
    hval = [num[i] / jnp.maximum(jnp.abs(den[i]), jnp.exp(-m_t[i])) for i in heads]
    for i in heads:
        hs = slice(i * dv, (i + 1) * dv)
        mo_ref[:, hs] = jax.nn.sigmoid(o_ref[:, hs]) * _rms(hval[i], nw_ref[:, hs])

    b_last = [x[L - 1:L, :] for x in b_col]
    dl_col = [b_last[i] - b_col[i] + li_col[i] for i in heads]
    dl_row = [b_last[i] - b_row[i] + li_row[i] for i in heads]
    m_new = [jnp.maximum(b_last[i] + m_prev[i], jnp.max(dl_row[i], axis=1, keepdims=True)) for i in heads]
    dec = [jnp.exp(b_last[i] + m_prev[i] - m_new[i]) for i in heads]
    kk = [jnp.exp(dl_col[i] - m_new[i]) * k[i] for i in heads]
    upd = [lax.dot_general(kk[i].astype(BF16), vb[i], (((0,), (0,)), ((), ())), preferred_element_type=F32)
           for i in heads]
    for i in heads:
        c_s[i] = dec[i] * cmat[i] + upd[i]
        n_s[i] = dec[i] * nvec[i] + jnp.sum(kk[i], axis=0, keepdims=True)
        m_s[i] = m_new[i]

    @pl.when(c == nc - 1)
    def _():
        c_out[0] = c_s[...]
        n_out[0] = n_s[...]
        m_out[0] = m_s[...]


def _mlstm(z, gates, gates_t, bias_row, bias_col, norm_w, c0, n0, m0, row_off, nb, t_len, L, hb, offs):
    _, nheads, dk, dv = c0.shape
    nc = t_len // L
    rb0 = row_off // L
    nhb = nheads // hb
    q_off, k_off, v_off, o_off = offs
    wq, wv = hb * dk, hb * dv

    def rowblk(b, c):
        return rb0 + b * nc + c

    in_specs = [
        pl.BlockSpec((L, wq), lambda b, h, c: (rowblk(b, c), q_off // wq + h)),
        pl.BlockSpec((L, wq), lambda b, h, c: (rowblk(b, c), k_off // wq + h)),
        pl.BlockSpec((L, wv), lambda b, h, c: (rowblk(b, c), v_off // wv + h)),
        pl.BlockSpec((L, wv), lambda b, h, c: (rowblk(b, c), o_off // wv + h)),
        pl.BlockSpec((L, LANES), lambda b, h, c: (rowblk(b, c), 0)),
        pl.BlockSpec((1, 2 * nheads, L), lambda b, h, c: (b * nc + c, 0, 0)),
        pl.BlockSpec((1, LANES), lambda b, h, c: (0, 0)),
        pl.BlockSpec((2 * nheads, 1), lambda b, h, c: (0, 0)),
        pl.BlockSpec((1, wv), lambda b, h, c: (0, h)),
        pl.BlockSpec((1, hb, dk, dv), lambda b, h, c: (b, h, 0, 0)),
        pl.BlockSpec((1, hb, 1, dk), lambda b, h, c: (b, h, 0, 0)),
        pl.BlockSpec((1, hb, 1, 1), lambda b, h, c: (b, h, 0, 0)),
    ]
    state_bytes = hb * dk * dv * 4
    vmem = 2 * (2 * L * wq * 4 + 3 * L * wv * 4 + 2 * state_bytes) + state_bytes + 24 * L * max(L, dv) * 4
    return pl.pallas_call(
        functools.partial(_mlstm_kernel, L=L, hb=hb, dk=dk, dv=dv, nc=nc, nheads=nheads),
        grid=(nb, nhb, nc),
        in_specs=in_specs,
        out_specs=[pl.BlockSpec((L, wv), lambda b, h, c: (b * nc + c, h)),
                   pl.BlockSpec((1, hb, dk, dv), lambda b, h, c: (b, h, 0, 0)),
                   pl.BlockSpec((1, hb, 1, dk), lambda b, h, c: (b, h, 0, 0)),
                   pl.BlockSpec((1, hb, 1, 1), lambda b, h, c: (b, h, 0, 0))],
        out_shape=[jax.ShapeDtypeStruct((nb * t_len, nheads * dv), F32),
                   jax.ShapeDtypeStruct(c0.shape, F32),
                   jax.ShapeDtypeStruct(n0.shape, F32),
                   jax.ShapeDtypeStruct(m0.shape, F32)],
        scratch_shapes=[pltpu.VMEM((hb, dk, dv), F32), pltpu.VMEM((hb, 1, dk), F32),
                        pltpu.VMEM((hb, 1, 1), F32)],
        compiler_params=_cparams(("arbitrary", "arbitrary", "arbitrary"), vmem),
        name="mlstm",
    )(z, z, z, z, gates, gates_t, bias_row, bias_col, norm_w, c0, n0, m0)


def _merge_kernel(dp_ref, ds_ref, wp_ref, ps_ref, ga_ref, gb_ref, mop_ref, mos_ref, o_ref, *, gin, gout, npt):
    def run(d_ref, mo_ref):
        for gg in range(wp_ref.shape[0]):
            y = jnp.dot(d_ref[:, gg * gin:(gg + 1) * gin].astype(BF16), wp_ref[gg], preferred_element_type=F32)
            cs = slice(gg * gout, (gg + 1) * gout)
            pool_out = y * ps_ref[:, cs]
            merged = jax.nn.sigmoid(ga_ref[:, cs]) * pool_out + jax.nn.sigmoid(gb_ref[:, cs]) * mo_ref[:, cs]
            o_ref[:, cs] = merged.astype(BF16)

    i = pl.program_id(0)
    pl.when(i < npt)(lambda: run(dp_ref, mop_ref))
    pl.when(i >= npt)(lambda: run(ds_ref, mos_ref))


def _merge(d_p, d_s, w_pool_b, pool_scale, z, mo_p, mo_s, ga_off, gb_off, tok, tiles):
    cdim = d_p.shape[1]
    ng, gin, gout = w_pool_b.shape
    dm = ng * gout
    half = dm // 2
    gh = ng // 2
    tm = tiles.merge_tm
    npt = tok.n_prompt // tm
    nst = tok.n_sample // tm

    def prompt_spec(width):
        return pl.BlockSpec((tm, width), lambda i, j: (jnp.minimum(i, npt - 1), jnp.where(i < npt, j, 1)))

    def sample_spec(width):
        return pl.BlockSpec((tm, width), lambda i, j: (jnp.clip(i - npt, 0, nst - 1), jnp.where(i < npt, 0, j)))

    vmem = 2 * (2 * tm * cdim * 2 + gh * gin * gout * 2 + 4 * tm * half * 4 + tm * half * 2) + 6 * tm * gout * 4
    return pl.pallas_call(
        functools.partial(_merge_kernel, gin=gin, gout=gout, npt=npt),
        grid=(tok.n // tm, 2),
        in_specs=[prompt_spec(cdim // 2), sample_spec(cdim // 2),
                  pl.BlockSpec((gh, gin, gout), lambda i, j: (j, 0, 0)),
                  pl.BlockSpec((1, half), lambda i, j: (0, j)),
                  pl.BlockSpec((tm, half), lambda i, j: (i, ga_off // half + j)),
                  pl.BlockSpec((tm, half), lambda i, j: (i, gb_off // half + j)),
                  prompt_spec(half), sample_spec(half)],
        out_specs=pl.BlockSpec((tm, half), lambda i, j: (i, j)),
        out_shape=jax.ShapeDtypeStruct((tok.n, dm), BF16),
        compiler_params=_cparams(("arbitrary", "arbitrary"), vmem),
        name="merge",
    )(d_p, d_s, w_pool_b, pool_scale.reshape(1, dm), z, z, mo_p, mo_s)


def _ffn_norm_kernel(xp_ref, xs_ref, acc_ref, g1_ref, sc_ref, sh_ref, nw_ref, wr_ref,
                     x1_ref, up_ref, re_ref, rw_ref, ub_ref, *, tok, tm):
    i = pl.program_id(0)
    d = x1_ref.shape[1]
    hd = d // 2

    def run(x_ref):
        def slab(s, carry):
            r = pl.multiple_of(s * BF16_ROWS, BF16_ROWS)
            rows = pl.ds(r, BF16_ROWS)
            row0 = i * tm + r
            x1 = x_ref[rows, :] + _mod_rows(g1_ref, row0, tok, d) * acc_ref[rows, :]
            x1_ref[rows, :] = x1
            u = _rms(x1, nw_ref[...]) * (1.0 + _mod_rows(sc_ref, row0, tok, d)) + _mod_rows(sh_ref, row0, tok, d)
            ub = u.astype(BF16)
            ub_ref[rows, :] = ub
            bits = lax.bitcast_convert_type(ub.astype(F32), U32)
            up_ref[rows, :] = (bits[:, hd:] & jnp.uint32(0xFFFF0000)) | (bits[:, :hd] >> 16)
            return carry
        lax.fori_loop(0, tm // BF16_ROWS, slab, 0)

    npt = tok.n_prompt // tm
    pl.when(i < npt)(lambda: run(xp_ref))
    pl.when(i >= npt)(lambda: run(xs_ref))

    lg = jnp.dot(ub_ref[...], wr_ref[...], preferred_element_type=F32)
    lane = lax.broadcasted_iota(I32, lg.shape, 1)
    lanef = lane.astype(F32)
    big = float(LANES)
    is_g = lane < N_GROUPS
    gl = jnp.where(is_g, lg, -jnp.inf)
    gmax = jnp.max(gl, axis=1, keepdims=True)
    grp = jnp.min(jnp.where(gl == gmax, lanef, big), axis=1, keepdims=True)
    p_grp = 1.0 / jnp.sum(jnp.where(is_g, jnp.exp(lg - gmax), 0.0), axis=1, keepdims=True)
    eidx = lane - N_GROUPS
    in_grp = (eidx >= 0) & (eidx < N_EXPERTS) & ((eidx >> EXP_SHIFT).astype(F32) == grp)
    el = jnp.where(in_grp, lg, -jnp.inf)
    v1 = jnp.max(el, axis=1, keepdims=True)
    i1 = jnp.min(jnp.where(el == v1, lanef, big), axis=1, keepdims=True)
    el2 = jnp.where(lanef == i1, -jnp.inf, el)
    v2 = jnp.max(el2, axis=1, keepdims=True)
    i2 = jnp.min(jnp.where(el2 == v2, lanef, big), axis=1, keepdims=True)
    e2 = jnp.exp(v2 - v1)
    w1 = (1.0 / (1.0 + e2)) * p_grp
    w2 = (e2 / (1.0 + e2)) * p_grp
    re_ref[...] = jnp.where(lane == 0, i1 - N_GROUPS, jnp.where(lane == 1, i2 - N_GROUPS, 0.0))
    rw_ref[...] = jnp.where(lane == 0, w1, jnp.where(lane == 1, w2, 0.0))


def _ffn_norm(xp, xs, acc, mod, norm_w, w_router, tok, tiles):
    d = xp.shape[1]
    g = mod.shape[0]
    tm = tiles.row_tm
    xp_spec, xs_spec = _dual_specs(tok, tm, d)
    vmem = (2 * (3 * tm * d * 4 + 3 * g * d * 4 + d * LANES * 2 + tm * d * 4 + tm * d * 2 + 2 * tm * LANES * 4)
            + tm * d * 2 + 16 * tm * LANES * 4)
    return pl.pallas_call(
        functools.partial(_ffn_norm_kernel, tok=tok, tm=tm),
        grid=(tok.n // tm,),
        in_specs=[xp_spec, xs_spec,
                  pl.BlockSpec((tm, d), lambda i: (i, 0)),
                  pl.BlockSpec((g, d), lambda i: (0, 2)),
                  pl.BlockSpec((g, d), lambda i: (0, 4)),
                  pl.BlockSpec((g, d), lambda i: (0, 3)),
                  pl.BlockSpec((1, d), lambda i: (0, 0)),
                  pl.BlockSpec((d, LANES), lambda i: (0, 0))],
        out_specs=[pl.BlockSpec((tm, d), lambda i: (i, 0)),
                   pl.BlockSpec((tm, d // 2), lambda i: (i, 0)),
                   pl.BlockSpec((tm, LANES), lambda i: (i, 0)),
                   pl.BlockSpec((tm, LANES), lambda i: (i, 0))],
        out_shape=[jax.ShapeDtypeStruct((tok.n, d), F32),
                   jax.ShapeDtypeStruct((tok.n, d // 2), U32),
                   jax.ShapeDtypeStruct((tok.n, LANES), F32),
                   jax.ShapeDtypeStruct((tok.n, LANES), F32)],
        scratch_shapes=[pltpu.VMEM((tm, d), BF16)],
        compiler_params=_cparams(("arbitrary",), vmem),
        name="ffn_norm",
    )(xp, xs, acc, mod, mod, mod, norm_w.reshape(1, d), w_router)


def _rank_kernel(re_ref, rank_ref, cnt_ref, carry_ref, *, tr):
    j = pl.program_id(0)
    i = pl.program_id(1)

    @pl.when((j == 0) & (i == 0))
    def _():
        carry_ref[...] = jnp.zeros_like(carry_ref)

    re = re_ref[...]
    lane = lax.broadcasted_iota(I32, re.shape, 1)
    e_col = jnp.sum(jnp.where(lane == j, re, 0.0), axis=1, keepdims=True)
    onehot = lane.astype(F32) == e_col
    rowi = lax.broadcasted_iota(I32, (tr, tr), 0)
    coli = lax.broadcasted_iota(I32, (tr, tr), 1)
    tri = jnp.where(rowi >= coli, 1.0, 0.0).astype(BF16)
    prefix = jnp.dot(tri, jnp.where(onehot, 1.0, 0.0).astype(BF16), preferred_element_type=F32)
    carry = carry_ref[0:1, :]
    rank = jnp.sum(jnp.where(onehot, prefix - 1.0 + carry, 0.0), axis=1, keepdims=True)
    rank_ref[...] = jnp.broadcast_to(rank, rank_ref.shape)
    new_carry = carry + prefix[tr - 1:tr, :]
    carry_ref[...] = jnp.broadcast_to(new_carry, carry_ref.shape)
    cnt_ref[...] = jnp.broadcast_to(new_carry, cnt_ref.shape)


def _rank(re, tiles):
    n = re.shape[0]
    tr = tiles.rank_tr
    nt = n // tr
    return pl.pallas_call(
        functools.partial(_rank_kernel, tr=tr),
        grid=(TOP_K, nt),
        in_specs=[pl.BlockSpec((tr, LANES), lambda j, i: (i, 0))],
        out_specs=[pl.BlockSpec((tr, LANES), lambda j, i: (j * nt + i, 0)),
                   pl.BlockSpec((SUBLANES, LANES), lambda j, i: (0, 0))],
        out_shape=[jax.ShapeDtypeStruct((TOP_K * n, LANES), F32),
                   jax.ShapeDtypeStruct((SUBLANES, LANES), F32)],
        scratch_shapes=[pltpu.VMEM((SUBLANES, LANES), F32)],
        compiler_params=_cparams(("arbitrary", "arbitrary"), 4 * MIB),
        name="rank",
    )(re)


def _invert_kernel(dest_ref, inv_ref, *, n_assign, n_rows):
    def clear(p, carry):
        inv_ref[p] = 0
        return carry
    lax.fori_loop(0, n_rows, clear, 0, unroll=16)

    def put(a, carry):
        inv_ref[dest_ref[a]] = a // TOP_K
        return carry
    lax.fori_loop(0, n_assign, put, 0, unroll=8)


def _invert(dest_flat, n_rows):
    return pl.pallas_call(
        functools.partial(_invert_kernel, n_assign=dest_flat.shape[0], n_rows=n_rows),
        in_specs=[pl.BlockSpec(memory_space=pltpu.SMEM)],
        out_specs=pl.BlockSpec(memory_space=pltpu.SMEM),
        out_shape=jax.ShapeDtypeStruct((n_rows,), I32),
        name="invert",
    )(dest_flat)


def _gather_kernel(nused_ref, inv_ref, src_ref, o_ref, buf, sems, *, bm):
    b = pl.program_id(0)
    n_used = nused_ref[0]

    def issue(blk, slot):
        def body(r, carry):
            pltpu.make_async_copy(src_ref.at[pl.ds(inv_ref[blk * bm + r], 1)], buf.at[slot, pl.ds(r, 1)],
                                  sems.at[slot]).start()
            return carry
        lax.fori_loop(0, bm, body, 0, unroll=8)

    @pl.when(b == 0)
    def _():
        issue(0, 0)

    @pl.when(b + 1 < n_used)
    def _():
        issue(b + 1, (b + 1) % 2)

    @pl.when(b < n_used)
    def _():
        slot = b % 2
        pltpu.make_async_copy(src_ref.at[pl.ds(0, bm)], buf.at[slot], sems.at[slot]).wait()
        o_ref[...] = buf[slot]


def _gather(inv, n_used, src, tiles):
    n_rows = inv.shape[0]
    width = src.shape[1]
    bm = tiles.moe_bm
    return pl.pallas_call(
        functools.partial(_gather_kernel, bm=bm),
        grid_spec=pltpu.PrefetchScalarGridSpec(
            num_scalar_prefetch=2,
            grid=(n_rows // bm,),
            in_specs=[pl.BlockSpec(memory_space=pl.ANY)],
            out_specs=pl.BlockSpec((bm, width), lambda b, nu, inv: (jnp.minimum(b, nu[0] - 1), 0)),
            scratch_shapes=[pltpu.VMEM((2, bm, width), src.dtype), pltpu.SemaphoreType.DMA((2,))]),
        out_shape=jax.ShapeDtypeStruct((n_rows, width), src.dtype),
        compiler_params=_cparams(("arbitrary",), 4 * bm * width * 4),
        name="gather",
    )(n_used, inv, src)


def _unpack_pair(words):
    lo = lax.bitcast_convert_type(words << 16, F32).astype(BF16)
    hi = lax.bitcast_convert_type(words & jnp.uint32(0xFFFF0000), F32).astype(BF16)
    return lo, hi


def _expert_weights(w, plan_refs, w_hbm, stage, sems, cw):
    _, col_ref, exp_ref, first_ref, _, slot_ref, nxt_e_ref, nxt_c_ref = plan_refs

    def copies(e, c, slot):
        cols = pl.ds(pl.multiple_of(c * cw, cw), cw)
        return [pltpu.make_async_copy(w_hbm[k].at[e, :, cols], stage.at[slot, k], sems.at[slot, k])
                for k in range(len(w_hbm))]

    @pl.when(w == 0)
    def _():
        for cp in copies(exp_ref[0], col_ref[0], 0):
            cp.start()

    @pl.when(first_ref[w] == 1)
    def _():
        for cp in copies(exp_ref[w], col_ref[w], slot_ref[w]):
            cp.wait()

        @pl.when(nxt_e_ref[w] >= 0)
        def _():
            for cp in copies(nxt_e_ref[w], nxt_c_ref[w], 1 - slot_ref[w]):
                cp.start()


def _moe_up_kernel(*refs, cw):
    plan_refs = refs[:8]
    xs_ref, wg_hbm, wu_hbm, h_ref, stage, wgb, wub, sems = refs[8:]
    first_ref, valid_ref, slot_ref = plan_refs[3], plan_refs[4], plan_refs[5]
    w = pl.program_id(0)
    hd = xs_ref.shape[1]
    _expert_weights(w, plan_refs, (wg_hbm, wu_hbm), stage, sems, cw)

    @pl.when(first_ref[w] == 1)
    def _():
        wgb[...] = stage[slot_ref[w], 0].astype(BF16)
        wub[...] = stage[slot_ref[w], 1].astype(BF16)

    @pl.when(valid_ref[w] == 1)
    def _():
        lo, hi = _unpack_pair(xs_ref[...])
        g = (jnp.dot(lo, wgb[:hd, :], preferred_element_type=F32)
             + jnp.dot(hi, wgb[hd:, :], preferred_element_type=F32))
        u = (jnp.dot(lo, wub[:hd, :], preferred_element_type=F32)
             + jnp.dot(hi, wub[hd:, :], preferred_element_type=F32))
        h_ref[...] = ((g * jax.nn.sigmoid(g)) * u).astype(BF16)


def _moe_up(plan, xs, w_gate, w_up, tiles):
    p_rows, hd = xs.shape
    _, d, de = w_gate.shape
    bm, cw = tiles.moe_bm, tiles.moe_cw
    n_items = plan["up"][0].shape[0]
    vmem = 2 * (bm * hd * 4 + bm * cw * 2) + 4 * d * cw * 4 + 2 * d * cw * 2 + 8 * bm * cw * 4 + 2 * bm * d * 2

    def by_item(f):
        return lambda w, blk, col, ex, fi, va, sl, ne, nc: f(w, blk, col)

    return pl.pallas_call(
        functools.partial(_moe_up_kernel, cw=cw),
        grid_spec=pltpu.PrefetchScalarGridSpec(
            num_scalar_prefetch=8,
            grid=(n_items,),
            in_specs=[pl.BlockSpec((bm, hd), by_item(lambda w, blk, col: (blk[w], 0))),
                      pl.BlockSpec(memory_space=pl.ANY),
                      pl.BlockSpec(memory_space=pl.ANY)],
            out_specs=pl.BlockSpec((bm, cw), by_item(lambda w, blk, col: (blk[w], col[w]))),
            scratch_shapes=[pltpu.VMEM((2, 2, d, cw), F32), pltpu.VMEM((d, cw), BF16), pltpu.VMEM((d, cw), BF16),
                            pltpu.SemaphoreType.DMA((2, 2))]),
        out_shape=jax.ShapeDtypeStruct((p_rows, de), BF16),
        compiler_params=_cparams(("arbitrary",), vmem),
        name="moe_up",
    )(*plan["up"], xs, w_gate, w_up)


def _moe_down_kernel(*refs, dcw):
    plan_refs = refs[:8]
    h_ref, wd_hbm, y_ref, stage, wdb, sems = refs[8:]
    first_ref, valid_ref, slot_ref = plan_refs[3], plan_refs[4], plan_refs[5]
    w = pl.program_id(0)
    _expert_weights(w, plan_refs, (wd_hbm,), stage, sems, dcw)

    @pl.when(first_ref[w] == 1)
    def _():
        wdb[...] = stage[slot_ref[w], 0].astype(BF16)

    @pl.when(valid_ref[w] == 1)
    def _():
        y_ref[...] = jnp.dot(h_ref[...], wdb[...], preferred_element_type=F32)


def _moe_down(plan, hdn, w_down, tiles):
    p_rows, de = hdn.shape
    d = w_down.shape[2]
    bm, dcw = tiles.moe_bm, tiles.moe_dcw
    n_items = plan["down"][0].shape[0]
    vmem = 2 * (bm * de * 2 + bm * dcw * 4) + 2 * de * dcw * 4 + de * dcw * 2 + 2 * bm * dcw * 4

    def by_item(f):
        return lambda w, blk, col, ex, fi, va, sl, ne, nc: f(w, blk, col)

    return pl.pallas_call(
        functools.partial(_moe_down_kernel, dcw=dcw),
        grid_spec=pltpu.PrefetchScalarGridSpec(
            num_scalar_prefetch=8,
            grid=(n_items,),
            in_specs=[pl.BlockSpec((bm, de), by_item(lambda w, blk, col: (blk[w], 0))),
                      pl.BlockSpec(memory_space=pl.ANY)],
            out_specs=pl.BlockSpec((bm, dcw), by_item(lambda w, blk, col: (blk[w], col[w]))),
            scratch_shapes=[pltpu.VMEM((2, 1, de, dcw), F32), pltpu.VMEM((de, dcw), BF16),
                            pltpu.SemaphoreType.DMA((2, 1))]),
        out_shape=jax.ShapeDtypeStruct((p_rows, d), F32),
        compiler_params=_cparams(("arbitrary",), vmem),
        name="moe_down",
    )(*plan["down"], hdn, w_down)


def _work_items(nb_e, blk0_e, n_cols, n_items):
    items_e = nb_e * n_cols
    cum = jnp.cumsum(items_e)
    total = cum[-1]
    w = jnp.minimum(jnp.arange(n_items, dtype=I32), total - 1)
    own = (w[:, None] >= (cum - items_e)[None, :]) & (w[:, None] < cum[None, :])

    def pick(v):
        return jnp.sum(jnp.where(own, v[None, :], 0), axis=1)

    e = pick(jnp.arange(N_EXPERTS, dtype=I32))
    r = w - pick(cum - items_e)
    nb = jnp.maximum(pick(nb_e), 1)
    col = r // nb
    blk = pick(blk0_e) + r % nb
    valid = (jnp.arange(n_items, dtype=I32) < total).astype(I32)
    key = e * n_cols + col
    first = jnp.concatenate([jnp.ones((1,), I32), (key[1:] != key[:-1]).astype(I32)]) * valid
    slot = (jnp.cumsum(first) - 1) % 2
    idx = jnp.arange(n_items, dtype=I32)
    nxt = lax.cummin(jnp.where(first == 1, idx, n_items)[::-1])[::-1]
    nxt = jnp.concatenate([nxt[1:], jnp.full((1,), n_items, I32)])
    has_next = nxt < n_items
    pick_next = nxt[:, None] == idx[None, :]
    nxt_e = jnp.where(has_next, jnp.sum(jnp.where(pick_next, e[None, :], 0), axis=1), -1)
    nxt_c = jnp.where(has_next, jnp.sum(jnp.where(pick_next, col[None, :], 0), axis=1), 0)
    return (blk.astype(I32), col.astype(I32), e.astype(I32), first.astype(I32), valid, slot.astype(I32),
            nxt_e.astype(I32), nxt_c.astype(I32))


def _dispatch_plan(re, rank_out, cnt_out, n, n_blocks, d_expert, d_model, tiles):
    bm = tiles.moe_bm
    eid = re[:, :TOP_K].astype(I32)
    rank = rank_out[:, 0].reshape(TOP_K, n).T.astype(I32)
    counts = cnt_out[0, :N_EXPERTS].astype(I32)
    nb_e = (counts + bm - 1) // bm
    blk0_e = jnp.cumsum(nb_e) - nb_e
    row0 = jnp.sum(jnp.where(eid[:, :, None] == jnp.arange(N_EXPERTS, dtype=I32), blk0_e * bm, 0), axis=2)
    dest = row0 + rank
    plan = {
        "dest": dest.reshape(-1).astype(I32),
        "n_used": jnp.sum(nb_e).reshape(1).astype(I32),
        "up": _work_items(nb_e, blk0_e, d_expert // tiles.moe_cw, n_blocks * (d_expert // tiles.moe_cw)),
        "down": _work_items(nb_e, blk0_e, d_model // tiles.moe_dcw, n_blocks * (d_model // tiles.moe_dcw)),
    }
    return plan


def _combine_kernel(dest_ref, ys_ref, x1_ref, rw_ref, g2_ref, fw_ref, yp_ref, ysm_ref, ybuf, sems, *, tok, tm):
    i = pl.program_id(0)
    d = x1_ref.shape[1]
    base = i * tm
    nt = pl.num_programs(0)

    def issue(tile, slot):
        def body(r, carry):
            for j in range(TOP_K):
                pltpu.make_async_copy(ys_ref.at[pl.ds(dest_ref[(tile * tm + r) * TOP_K + j], 1)],
                                      ybuf.at[slot, j, pl.ds(r, 1)], sems.at[slot, j]).start()
            return carry
        lax.fori_loop(0, tm, body, 0, unroll=4)

    @pl.when(i == 0)
    def _():
        issue(0, 0)

    @pl.when(i + 1 < nt)
    def _():
        issue(i + 1, (i + 1) % 2)

    slot = i % 2
    for j in range(TOP_K):
        pltpu.make_async_copy(ys_ref.at[pl.ds(0, tm)], ybuf.at[slot, j], sems.at[slot, j]).wait()

    def run(out_ref):
        def slab(s, carry):
            r = pl.multiple_of(s * SUBLANES, SUBLANES)
            rows = pl.ds(r, SUBLANES)
            wts = rw_ref[rows, :]
            ff = wts[:, 0:1] * ybuf[slot, 0, rows, :] + wts[:, 1:2] * ybuf[slot, 1, rows, :]
            g2 = jnp.broadcast_to(g2_ref[pl.ds(_batch_of(base + r, tok), 1), :], (SUBLANES, d))
            out_ref[rows, :] = _rms(x1_ref[rows, :] + g2 * ff, fw_ref[...])
            return carry
        lax.fori_loop(0, tm // SUBLANES, slab, 0)

    npt = tok.n_prompt // tm
    pl.when(i < npt)(lambda: run(yp_ref))
    pl.when(i >= npt)(lambda: run(ysm_ref))


def _combine(dest_flat, ys, x1, rw, mod, final_w, tok, tiles):
    d = x1.shape[1]
    g = mod.shape[0]
    tm = tiles.comb_tm
    npt = tok.n_prompt // tm
    nst = tok.n_sample // tm
    vmem = 2 * (tm * d * 4 + tm * LANES * 4 + g * d * 4 + 2 * tm * d * 4) + 2 * TOP_K * tm * d * 4
    return pl.pallas_call(
        functools.partial(_combine_kernel, tok=tok, tm=tm),
        grid_spec=pltpu.PrefetchScalarGridSpec(
            num_scalar_prefetch=1,
            grid=(tok.n // tm,),
            in_specs=[pl.BlockSpec(memory_space=pl.ANY),
                      pl.BlockSpec((tm, d), lambda i, dest: (i, 0)),
                      pl.BlockSpec((tm, LANES), lambda i, dest: (i, 0)),
                      pl.BlockSpec((g, d), lambda i, dest: (0, 5)),
                      pl.BlockSpec((1, d), lambda i, dest: (0, 0))],
            out_specs=[pl.BlockSpec((tm, d), lambda i, dest: (jnp.minimum(i, npt - 1), 0)),
                       pl.BlockSpec((tm, d), lambda i, dest: (jnp.clip(i - npt, 0, nst - 1), 0))],
            scratch_shapes=[pltpu.VMEM((2, TOP_K, tm, d), F32), pltpu.SemaphoreType.DMA((2, TOP_K))]),
        out_shape=[jax.ShapeDtypeStruct((tok.n_prompt, d), F32),
                   jax.ShapeDtypeStruct((tok.n_sample, d), F32)],
        compiler_params=_cparams(("arbitrary",), vmem),
        name="combine",
    )(dest_flat, ys, x1, rw, mod, final_w.reshape(1, d))


def _gate_layouts(gates, n_rows, row_off, L, n_gate):
    g = gates[row_off:row_off + n_rows, :n_gate]
    return g.reshape(n_rows // L, L, n_gate).transpose(0, 2, 1)


def _layer(xp, xs, c_all, pool_s, c_s, n_s, m_s, lw, tok, tiles):
    (w_ada, b_ada, norm_mix_w, w_in, b_igate, b_fgate, w_pool, pool_scale, mlstm_norm_w, w_out, norm_ffn_w,
     w_router_group, w_router_expert, w_exp_gate, w_exp_up, w_exp_down) = lw
    d = xp.shape[1]
    nheads, dk, dv = c_s.shape[1:]
    pool_in = pool_s.shape[2]
    mqk, mv = nheads * dk, nheads * dv
    offs = (pool_in, pool_in + mqk, pool_in + 2 * mqk, pool_in + 2 * mqk + mv)
    ga_off = offs[3] + mv
    gb_off = ga_off + d
    n_main = gb_off + d
    n_gate = 2 * nheads
    n = tok.n

    mod = _ada(c_all, w_ada, b_ada, tiles)
    w_in_t = w_in.T
    u, gates = _prenorm(xp, xs, mod, norm_mix_w, w_in_t, n_main, n_gate, tok, tiles)
    z = _matmul(u, w_in_t, n_main, True, tiles)

    zeros_pool = jnp.zeros((tok.nb_p, POOL_BUF, pool_in), F32)
    d_p, pool_new_p = _pool(z, zeros_pool, 0, tok.nb_p, tok.rows_p, 1, tiles.pool_tt, 0)
    d_s, pool_new_s = _pool(z, pool_s, tok.n_prompt, tok.nb_s, tok.rows_s, tiles.pool_bt, tok.rows_s, PAST_LEN)

    bias = jnp.concatenate([b_igate, b_fgate])
    bias_row = jnp.pad(bias, (0, LANES - n_gate)).reshape(1, LANES)
    bias_col = bias.reshape(n_gate, 1)
    nw = mlstm_norm_w.reshape(1, mv)
    lp = tiles.mlstm_l
    zc = jnp.zeros((tok.nb_p, nheads, dk, dv), F32)
    zn = jnp.zeros((tok.nb_p, nheads, 1, dk), F32)
    zm = jnp.zeros((tok.nb_p, nheads, 1, 1), F32)
    mo_p, c_p, n_p, m_p = _mlstm(z, gates, _gate_layouts(gates, tok.n_prompt, 0, lp, n_gate), bias_row, bias_col,
                                 nw, zc, zn, zm, 0, tok.nb_p, tok.rows_p, lp, tiles.mlstm_hb_p, offs)
    ls = tok.rows_s
    mo_s, c_n, n_n, m_n = _mlstm(z, gates, _gate_layouts(gates, tok.n_sample, tok.n_prompt, ls, n_gate), bias_row,
                                 bias_col, nw, c_s, n_s.reshape(tok.nb_s, nheads, 1, dk),
                                 m_s.reshape(tok.nb_s, nheads, 1, 1), tok.n_prompt, tok.nb_s, ls, ls,
                                 tiles.mlstm_hb_s, offs)

    merged = _merge(d_p, d_s, w_pool.astype(BF16), pool_scale, z, mo_p, mo_s, ga_off, gb_off, tok, tiles)
    acc = _matmul(merged, w_out, d, False, tiles)

    w_router = jnp.pad(jnp.concatenate([w_router_group, w_router_expert], axis=1),
                       ((0, 0), (0, LANES - N_GROUPS - N_EXPERTS))).astype(BF16)
    x1, u2p, re, rw = _ffn_norm(xp, xs, acc, mod, norm_ffn_w, w_router, tok, tiles)
    rank_out, cnt_out = _rank(re, tiles)
    d_expert = w_exp_gate.shape[2]
    n_blocks = -(-(n * TOP_K) // tiles.moe_bm) + N_EXPERTS
    plan = _dispatch_plan(re, rank_out, cnt_out, n, n_blocks, d_expert, d, tiles)
    inv = _invert(plan["dest"], n_blocks * tiles.moe_bm)
    xsort = _gather(inv, plan["n_used"], u2p, tiles)
    hdn = _moe_up(plan, xsort, w_exp_gate, w_exp_up, tiles)
    ysort = _moe_down(plan, hdn, w_exp_down, tiles)
    states = (pool_new_p, c_p, n_p.reshape(tok.nb_p, nheads, dk), m_p.reshape(tok.nb_p, nheads),
              pool_new_s, c_n, n_n.reshape(tok.nb_s, nheads, dk), m_n.reshape(tok.nb_s, nheads))
    return (plan["dest"], ysort, x1, rw, mod), states


def _forward(x_prompt, x_sample, c_prompt, c_sample, state_pool, state_mlstm_C, state_mlstm_n, state_mlstm_m,
             w_ada, b_ada, norm_mix_w, w_in, b_igate, b_fgate, w_pool, pool_scale, mlstm_norm_w, w_out,
             norm_ffn_w, w_router_group, w_router_expert, w_exp_gate, w_exp_up, w_exp_down, final_norm_w,
             tiles=Tiles()):
    nb_p, rows_p, d = x_prompt.shape
    nb_s, rows_s, _ = x_sample.shape
    depth = w_ada.shape[0]
    assert depth == 1, "the merged-token pipeline is written for a single layer"
    tok = Tok(nb_p * rows_p, nb_s * rows_s, rows_p, rows_s, nb_p, nb_s)
    for tm in (tiles.row_tm, tiles.mm_tm, tiles.merge_tm, tiles.rank_tr, tiles.comb_tm):
        assert tok.n_prompt % tm == 0 and tok.n_sample % tm == 0
    xp = x_prompt.reshape(tok.n_prompt, d)
    xs = x_sample.reshape(tok.n_sample, d)
    g = nb_p + nb_s
    g_pad = -(-g // SUBLANES) * SUBLANES
    c_all = jnp.pad(jnp.concatenate([c_prompt, c_sample], axis=0), ((0, g_pad - g), (0, 0)))
    lw = (w_ada[0], b_ada[0], norm_mix_w[0], w_in[0], b_igate[0], b_fgate[0], w_pool[0], pool_scale[0],
          mlstm_norm_w[0], w_out[0], norm_ffn_w[0], w_router_group[0], w_router_expert[0], w_exp_gate[0],
          w_exp_up[0], w_exp_down[0])
    (dest, ysort, x1, rw, mod), st = _layer(xp, xs, c_all, state_pool[0], state_mlstm_C[0], state_mlstm_n[0],
                                            state_mlstm_m[0], lw, tok, tiles)
    y_p, y_s = _combine(dest, ysort, x1, rw, mod, final_norm_w, tok, tiles)
    return (y_p.reshape(x_prompt.shape), y_s.reshape(x_sample.shape)) + tuple(s[None] for s in st)


def kernel(x_prompt, x_sample, c_prompt, c_sample, state_pool, state_mlstm_C, state_mlstm_n, state_mlstm_m,
           w_ada, b_ada, norm_mix_w, w_in, b_igate, b_fgate, w_pool, pool_scale, mlstm_norm_w, w_out,
           norm_ffn_w, w_router_group, w_router_expert, w_exp_gate, w_exp_up, w_exp_down, final_norm_w):
    return _forward(x_prompt, x_sample, c_prompt, c_sample, state_pool, state_mlstm_C, state_mlstm_n,
                    state_mlstm_m, w_ada, b_ada, norm_mix_w, w_in, b_igate, b_fgate, w_pool, pool_scale,
                    mlstm_norm_w, w_out, norm_ffn_w, w_router_group, w_router_expert, w_exp_gate, w_exp_up,
                    w_exp_down, final_norm_w)
```

```python
import functools
from typing import NamedTuple

import jax
import jax.numpy as jnp
from jax import lax
from jax.experimental import pallas as pl
from jax.experimental.pallas import tpu as pltpu

F32, BF16, I32, U32 = jnp.float32, jnp.bfloat16, jnp.int32, jnp.uint32

EPS = 1e-6
PAST_LEN = 16384
POOL_WINDOWS = (2, 4, 8, 16)
POOL_BUF = 15
N_GROUPS = 8
EXP_PER_GROUP = 8
N_EXPERTS = N_GROUPS * EXP_PER_GROUP
EXP_SHIFT = EXP_PER_GROUP.bit_length() - 1
TOP_K = 2

V7X_VMEM_BYTES = 64 * 2**20
LANES = 128
SUBLANES = 8
BF16_ROWS = 16
MIB = 2**20


class Tok(NamedTuple):
    n_prompt: int
    n_sample: int
    rows_p: int
    rows_s: int
    nb_p: int
    nb_s: int

    @property
    def n(self):
        return self.n_prompt + self.n_sample


class Tiles(NamedTuple):
    ada_tn: int = 512
    row_tm: int = 256
    mm_tm: int = 512
    mm_tn: int = 1024
    op_tn: int = 512
    pool_tt: int = 256
    pool_bt: int = 16
    mlstm_l: int = 256
    mlstm_hb_p: int = 2
    mlstm_hb_s: int = 4
    merge_tm: int = 512
    rank_tr: int = 256
    moe_bm: int = 256
    moe_cw: int = 512
    moe_dcw: int = 2048
    comb_tm: int = 256


def _cparams(sem, vmem_bytes):
    limit = int(min(max(vmem_bytes * 5 // 4 + 2 * MIB, 16 * MIB), V7X_VMEM_BYTES - 6 * MIB))
    return pltpu.CompilerParams(dimension_semantics=sem, vmem_limit_bytes=limit)


def _batch_of(row, tok):
    return jnp.where(row < tok.n_prompt, row // tok.rows_p,
                     tok.nb_p + (row - tok.n_prompt) // tok.rows_s)


def _mod_rows(ref, row0, tok, d):
    top = jnp.broadcast_to(ref[pl.ds(_batch_of(row0, tok), 1), :], (SUBLANES, d))
    bot = jnp.broadcast_to(ref[pl.ds(_batch_of(row0 + SUBLANES, tok), 1), :], (SUBLANES, d))
    return jnp.concatenate([top, bot], axis=0)


def _rms(x, w):
    r = lax.rsqrt(jnp.mean(x * x, axis=-1, keepdims=True) + EPS)
    return (x * r) * w


def _log_sigmoid(x):
    return jnp.minimum(x, 0.0) - jnp.log1p(jnp.exp(-jnp.abs(x)))


def _ada_kernel(c_ref, w_ref, b_ref, o_ref):
    c = c_ref[...]
    a = (c * jax.nn.sigmoid(c)).astype(BF16)
    o_ref[...] = jnp.dot(a, w_ref[...].astype(BF16), preferred_element_type=F32) + b_ref[...]


def _ada(c_all, w_ada, b_ada, tiles):
    g, d = c_all.shape
    n = w_ada.shape[1]
    tn = tiles.ada_tn
    vmem = 2 * (g * d * 4 + d * tn * 4 + g * tn * 4) + d * tn * 2
    return pl.pallas_call(
        _ada_kernel,
        grid=(n // tn,),
        in_specs=[pl.BlockSpec((g, d), lambda j: (0, 0)),
                  pl.BlockSpec((d, tn), lambda j: (0, j)),
                  pl.BlockSpec((1, tn), lambda j: (0, j))],
        out_specs=pl.BlockSpec((g, tn), lambda j: (0, j)),
        out_shape=jax.ShapeDtypeStruct((g, n), F32),
        compiler_params=_cparams(("arbitrary",), vmem),
        name="ada",
    )(c_all, w_ada, b_ada.reshape(1, n))


def _dual_specs(tok, tm, d):
    npt = tok.n_prompt // tm
    nst = tok.n_sample // tm
    return (pl.BlockSpec((tm, d), lambda i: (jnp.minimum(i, npt - 1), 0)),
            pl.BlockSpec((tm, d), lambda i: (jnp.clip(i - npt, 0, nst - 1), 0)))


def _prenorm_kernel(xp_ref, xs_ref, sc_ref, sh_ref, nw_ref, wg_ref, u_ref, gates_ref, wgb_ref, *, tok, tm, n_gate):
    i = pl.program_id(0)
    d = u_ref.shape[1]

    @pl.when(i == 0)
    def _():
        wgb_ref[0:n_gate, :] = wg_ref[...].astype(BF16)
        wgb_ref[n_gate:, :] = jnp.zeros((LANES - n_gate, d), BF16)

    def run(x_ref):
        def slab(s, carry):
            r = pl.multiple_of(s * BF16_ROWS, BF16_ROWS)
            row0 = i * tm + r
            xn = _rms(x_ref[pl.ds(r, BF16_ROWS), :], nw_ref[...])
            u = xn * (1.0 + _mod_rows(sc_ref, row0, tok, d)) + _mod_rows(sh_ref, row0, tok, d)
            u_ref[pl.ds(r, BF16_ROWS), :] = u.astype(BF16)
            return carry
        lax.fori_loop(0, tm // BF16_ROWS, slab, 0)

    npt = tok.n_prompt // tm
    pl.when(i < npt)(lambda: run(xp_ref))
    pl.when(i >= npt)(lambda: run(xs_ref))
    gates_ref[...] = lax.dot_general(u_ref[...], wgb_ref[...], (((1,), (1,)), ((), ())),
                                     preferred_element_type=F32)


def _prenorm(xp, xs, mod, norm_w, w_in_t, n_main, n_gate, tok, tiles):
    d = xp.shape[1]
    g = mod.shape[0]
    tm = tiles.row_tm
    assert n_main % n_gate == 0 and n_gate % BF16_ROWS == 0 and n_gate <= LANES
    xp_spec, xs_spec = _dual_specs(tok, tm, d)
    vmem = 2 * (2 * tm * d * 4 + 2 * g * d * 4 + n_gate * d * 4 + tm * d * 2 + tm * LANES * 4) + d * LANES * 2
    return pl.pallas_call(
        functools.partial(_prenorm_kernel, tok=tok, tm=tm, n_gate=n_gate),
        grid=(tok.n // tm,),
        in_specs=[xp_spec, xs_spec,
                  pl.BlockSpec((g, d), lambda i: (0, 1)),
                  pl.BlockSpec((g, d), lambda i: (0, 0)),
                  pl.BlockSpec((1, d), lambda i: (0, 0)),
                  pl.BlockSpec((n_gate, d), lambda i: (n_main // n_gate, 0))],
        out_specs=[pl.BlockSpec((tm, d), lambda i: (i, 0)),
                   pl.BlockSpec((tm, LANES), lambda i: (i, 0))],
        out_shape=[jax.ShapeDtypeStruct((tok.n, d), BF16),
                   jax.ShapeDtypeStruct((tok.n, LANES), F32)],
        scratch_shapes=[pltpu.VMEM((LANES, d), BF16)],
        compiler_params=_cparams(("arbitrary",), vmem),
        name="prenorm",
    )(xp, xs, mod, mod, norm_w.reshape(1, d), w_in_t)


def _matmul_kernel(a_ref, w_ref, o_ref, wb_ref, *, w_is_nk):
    @pl.when(pl.program_id(1) == 0)
    def _():
        wb_ref[...] = w_ref[...].astype(BF16)
    contract_w = 1 if w_is_nk else 0
    o_ref[...] = lax.dot_general(a_ref[...], wb_ref[...], (((1,), (contract_w,)), ((), ())),
                                 preferred_element_type=F32)


def _matmul(a, w, n_cols, w_is_nk, tiles):
    m, k = a.shape
    tm, tn = tiles.mm_tm, tiles.mm_tn
    w_block = (tn, k) if w_is_nk else (k, tn)
    w_index = (lambda j, i: (j, 0)) if w_is_nk else (lambda j, i: (0, j))
    vmem = 2 * (tm * k * 2 + k * tn * 4 + tm * tn * 4) + k * tn * 2
    return pl.pallas_call(
        functools.partial(_matmul_kernel, w_is_nk=w_is_nk),
        grid=(n_cols // tn, m // tm),
        in_specs=[pl.BlockSpec((tm, k), lambda j, i: (i, 0)),
                  pl.BlockSpec(w_block, w_index)],
        out_specs=pl.BlockSpec((tm, tn), lambda j, i: (i, j)),
        out_shape=jax.ShapeDtypeStruct((m, n_cols), F32),
        scratch_shapes=[pltpu.VMEM(w_block, BF16)],
        compiler_params=_cparams(("arbitrary", "arbitrary"), vmem),
        name="matmul",
    )(a, w)


def _out_proj_kernel(a_ref, w_ref, xp_ref, xs_ref, g1_ref, o_ref, wb_ref, *, tok, tm):
    i = pl.program_id(1)
    npt = tok.n_prompt // tm

    @pl.when(i == 0)
    def _():
        wb_ref[...] = w_ref[...].astype(BF16)
    acc = jnp.dot(a_ref[...], wb_ref[...], preferred_element_type=F32)

    @pl.when(i < npt)
    def _():
        o_ref[...] = xp_ref[...] + g1_ref[pl.ds(i * tm // tok.rows_p, 1), :] * acc

    @pl.when(i >= npt)
    def _():
        o_ref[...] = acc

        def slab(s, carry):
            r = pl.multiple_of(s * SUBLANES, SUBLANES)
            rows = pl.ds(r, SUBLANES)
            g1 = g1_ref[pl.ds(_batch_of(i * tm + r, tok), 1), :]
            o_ref[rows, :] = xs_ref[rows, :] + g1 * o_ref[rows, :]
            return carry
        lax.fori_loop(0, tm // SUBLANES, slab, 0)


def _out_proj(a, w, xp, xs, mod, tok, tiles):
    m, k = a.shape
    d = w.shape[1]
    g = mod.shape[0]
    tm, tn = tiles.mm_tm, tiles.op_tn
    assert tok.rows_p % tm == 0 and tm % tok.rows_s == 0 and tok.rows_s % SUBLANES == 0
    npt = tok.n_prompt // tm
    nst = tok.n_sample // tm
    vmem = 2 * (tm * k * 2 + k * tn * 4 + 3 * tm * tn * 4 + g * tn * 4) + k * tn * 2 + 2 * tm * tn * 4
    return pl.pallas_call(
        functools.partial(_out_proj_kernel, tok=tok, tm=tm),
        grid=(d // tn, m // tm),
        in_specs=[pl.BlockSpec((tm, k), lambda j, i: (i, 0)),
                  pl.BlockSpec((k, tn), lambda j, i: (0, j)),
                  pl.BlockSpec((tm, tn), lambda j, i: (jnp.minimum(i, npt - 1), j)),
                  pl.BlockSpec((tm, tn), lambda j, i: (jnp.clip(i - npt, 0, nst - 1), j)),
                  pl.BlockSpec((g, tn), lambda j, i: (0, 2 * (d // tn) + j))],
        out_specs=pl.BlockSpec((tm, tn), lambda j, i: (i, j)),
        out_shape=jax.ShapeDtypeStruct((m, d), F32),
        scratch_shapes=[pltpu.VMEM((k, tn), BF16)],
        compiler_params=_cparams(("arbitrary", "arbitrary"), vmem),
        name="out_proj",
    )(a, w, xp, xs, mod)


def _pool_kernel(p_ref, st_ref, d_ref, new_ref, ext_ref, *, bt, tt, gin, pos0, nt):
    t = pl.program_id(1)
    cdim = ext_ref.shape[2]
    halo = POOL_BUF + 1

    @pl.when(t == 0)
    def _():
        ext_ref[:, 1:halo, :] = st_ref[...]

    @pl.when(t > 0)
    def _():
        ext_ref[:, 0:halo, :] = ext_ref[:, tt:tt + halo, :]

    ext_ref[:, halo:halo + tt, :] = p_ref[...].reshape(bt, tt, cdim)
    pos = pos0 + t * tt + lax.broadcasted_iota(I32, (bt, tt, gin), 1)
    for g, w in enumerate(POOL_WINDOWS):
        cs = slice(g * gin, (g + 1) * gin)
        cur = ext_ref[:, halo:halo + tt, cs]
        acc = cur
        for j in range(1, w):
            acc = acc + ext_ref[:, halo - j:halo - j + tt, cs]
        cnt = jnp.minimum(pos + 1, w).astype(F32)
        d_ref[:, cs] = (acc / cnt - cur).reshape(bt * tt, gin)

    @pl.when(t == nt - 1)
    def _():
        new_ref[...] = ext_ref[:, tt + 1:tt + halo, :]


def _pool(z, state, row_off, nb, t_len, bt, tt, pos0):
    cdim = state.shape[2]
    gin = cdim // len(POOL_WINDOWS)
    nt = t_len // tt
    rows = bt * tt
    rb0 = row_off // rows
    vmem = 2 * (2 * rows * cdim * 4 + 2 * bt * 16 * cdim * 4) + bt * (tt + 16) * cdim * 4 + 8 * rows * gin * 4
    return pl.pallas_call(
        functools.partial(_pool_kernel, bt=bt, tt=tt, gin=gin, pos0=pos0, nt=nt),
        grid=(nb // bt, nt),
        in_specs=[pl.BlockSpec((rows, cdim), lambda b, t: (rb0 + b * nt + t, 0)),
                  pl.BlockSpec((bt, POOL_BUF, cdim), lambda b, t: (b, 0, 0))],
        out_specs=[pl.BlockSpec((rows, cdim), lambda b, t: (b * nt + t, 0)),
                   pl.BlockSpec((bt, POOL_BUF, cdim), lambda b, t: (b, 0, 0))],
        out_shape=[jax.ShapeDtypeStruct((nb * t_len, cdim), F32),
                   jax.ShapeDtypeStruct((nb, POOL_BUF, cdim), F32)],
        scratch_shapes=[pltpu.VMEM((bt, tt + POOL_BUF + 1, cdim), F32)],
        compiler_params=_cparams(("arbitrary", "arbitrary"), vmem),
        name="pool",
    )(z, state)


def _mlstm_kernel(q_ref, k_ref, v_ref, o_ref, gc_ref, gr_ref, brow_ref, bcol_ref, nw_ref, c0_ref, n0_ref, m0_ref,
                  mo_ref, c_out, n_out, m_out, c_s, n_s, m_s, *, L, hb, dk, dv, nc, nheads):
    hblk = pl.program_id(1)
    c = pl.program_id(2)

    @pl.when(c == 0)
    def _():
        c_s[...] = c0_ref[0]
        n_s[...] = n0_ref[0]
        m_s[...] = m0_ref[0]

    gcb = gc_ref[...] + brow_ref[...]
    grb = gr_ref[0] + bcol_ref[...]
    lane = lax.broadcasted_iota(I32, gcb.shape, 1)
    sub = lax.broadcasted_iota(I32, grb.shape, 0)
    rowi = lax.broadcasted_iota(I32, (L, L), 0)
    coli = lax.broadcasted_iota(I32, (L, L), 1)
    tri = rowi >= coli
    scale = dk ** -0.5

    heads = range(hb)
    if hb == nheads:
        li_col = [gcb[:, hh:hh + 1] for hh in heads]
        lf_col = [gcb[:, nheads + hh:nheads + hh + 1] for hh in heads]
        li_row = [grb[hh:hh + 1, :] for hh in heads]
        lf_row = [grb[nheads + hh:nheads + hh + 1, :] for hh in heads]
    else:
        hs_dyn = [hblk * hb + hh for hh in heads]
        li_col = [jnp.sum(jnp.where(lane == h, gcb, 0.0), axis=1, keepdims=True) for h in hs_dyn]
        lf_col = [jnp.sum(jnp.where(lane == h + nheads, gcb, 0.0), axis=1, keepdims=True) for h in hs_dyn]
        li_row = [jnp.sum(jnp.where(sub == h, grb, 0.0), axis=0, keepdims=True) for h in hs_dyn]
        lf_row = [jnp.sum(jnp.where(sub == h + nheads, grb, 0.0), axis=0, keepdims=True) for h in hs_dyn]
    lf_col = [_log_sigmoid(x) for x in lf_col]
    lf_row = [_log_sigmoid(x) for x in lf_row]
    b_col = [jnp.sum(jnp.where(tri, x, 0.0), axis=1, keepdims=True) for x in lf_row]
    b_row = [jnp.sum(jnp.where(rowi <= coli, x, 0.0), axis=0, keepdims=True) for x in lf_col]

    q = [q_ref[:, hh * dk:(hh + 1) * dk] * scale for hh in heads]
    k = [k_ref[:, hh * dk:(hh + 1) * dk] for hh in heads]
    qb = [x.astype(BF16) for x in q]
    kb = [x.astype(BF16) for x in k]
    vb = [v_ref[:, hh * dv:(hh + 1) * dv].astype(BF16) for hh in heads]
    cmat = [c_s[hh] for hh in heads]
    nvec = [n_s[hh] for hh in heads]
    m_prev = [m_s[hh] for hh in heads]

    dmat = [jnp.where(tri, b_col[i] - b_row[i] + li_row[i], -jnp.inf) for i in heads]
    inter = [b_col[i] + m_prev[i] for i in heads]
    m_t = [jnp.maximum(inter[i], jnp.max(dmat[i], axis=1, keepdims=True)) for i in heads]
    dw = [jnp.exp(dmat[i] - m_t[i]) for i in heads]
    iw = [jnp.exp(inter[i] - m_t[i]) for i in heads]
    qk = [lax.dot_general(qb[i], kb[i], (((1,), (1,)), ((), ())), preferred_element_type=F32) for i in heads]
    qc = [jnp.dot(qb[i], cmat[i].astype(BF16), preferred_element_type=F32) for i in heads]
    s = [qk[i] * dw[i] for i in heads]
    num = [jnp.dot(s[i].astype(BF16), vb[i], preferred_element_type=F32) + iw[i] * qc[i] for i in heads]
    den = [jnp.sum(s[i], axis=1, keepdims=True) + iw[i] * jnp.sum(q[i] * nvec[i], axis=1, keepdims=True)
           for i in heads]
    hval = [num[i] / jnp.maximum(jnp.abs(den[i]), jnp.exp(-m_t[i])) for i in heads]
    for i in heads:
        hs = slice(i * dv, (i + 1) * dv)
        mo_ref[:, hs] = jax.nn.sigmoid(o_ref[:, hs]) * _rms(hval[i], nw_ref[:, hs])

    b_last = [x[L - 1:L, :] for x in b_col]
    dl_col = [b_last[i] - b_col[i] + li_col[i] for i in heads]
    dl_row = [b_last[i] - b_row[i] + li_row[i] for i in heads]
    m_new = [jnp.maximum(b_last[i] + m_prev[i], jnp.max(dl_row[i], axis=1, keepdims=True)) for i in heads]
    dec = [jnp.exp(b_last[i] + m_prev[i] - m_new[i]) for i in heads]
    kk = [jnp.exp(dl_col[i] - m_new[i]) * k[i] for i in heads]
    upd = [lax.dot_general(kk[i].astype(BF16), vb[i], (((0,), (0,)), ((), ())), preferred_element_type=F32)
           for i in heads]
    for i in heads:
        c_s[i] = dec[i] * cmat[i] + upd[i]
        n_s[i] = dec[i] * nvec[i] + jnp.sum(kk[i], axis=0, keepdims=True)
        m_s[i] = m_new[i]

    @pl.when(c == nc - 1)
    def _():
        c_out[0] = c_s[...]
        n_out[0] = n_s[...]
        m_out[0] = m_s[...]


def _mlstm(z, gates, gates_t, bias_row, bias_col, norm_w, c0, n0, m0, row_off, nb, t_len, L, hb, offs):
    _, nheads, dk, dv = c0.shape
    nc = t_len // L
    rb0 = row_off // L
    nhb = nheads // hb
    q_off, k_off, v_off, o_off = offs
    wq, wv = hb * dk, hb * dv

    def rowblk(b, c):
        return rb0 + b * nc + c

    in_specs = [
        pl.BlockSpec((L, wq), lambda b, h, c: (rowblk(b, c), q_off // wq + h)),
        pl.BlockSpec((L, wq), lambda b, h, c: (rowblk(b, c), k_off // wq + h)),
        pl.BlockSpec((L, wv), lambda b, h, c: (rowblk(b, c), v_off // wv + h)),
        pl.BlockSpec((L, wv), lambda b, h, c: (rowblk(b, c), o_off // wv + h)),
        pl.BlockSpec((L, LANES), lambda b, h, c: (rowblk(b, c), 0)),
        pl.BlockSpec((1, 2 * nheads, L), lambda b, h, c: (b * nc + c, 0, 0)),
        pl.BlockSpec((1, LANES), lambda b, h, c: (0, 0)),
        pl.BlockSpec((2 * nheads, 1), lambda b, h, c: (0, 0)),
        pl.BlockSpec((1, wv), lambda b, h, c: (0, h)),
        pl.BlockSpec((1, hb, dk, dv), lambda b, h, c: (b, h, 0, 0)),
        pl.BlockSpec((1, hb, 1, dk), lambda b, h, c: (b, h, 0, 0)),
        pl.BlockSpec((1, hb, 1, 1), lambda b, h, c: (b, h, 0, 0)),
    ]
    state_bytes = hb * dk * dv * 4
    vmem = 2 * (2 * L * wq * 4 + 3 * L * wv * 4 + 2 * state_bytes) + state_bytes + 24 * L * max(L, dv) * 4
    return pl.pallas_call(
        functools.partial(_mlstm_kernel, L=L, hb=hb, dk=dk, dv=dv, nc=nc, nheads=nheads),
        grid=(nb, nhb, nc),
        in_specs=in_specs,
        out_specs=[pl.BlockSpec((L, wv), lambda b, h, c: (b * nc + c, h)),
                   pl.BlockSpec((1, hb, dk, dv), lambda b, h, c: (b, h, 0, 0)),
                   pl.BlockSpec((1, hb, 1, dk), lambda b, h, c: (b, h, 0, 0)),
                   pl.BlockSpec((1, hb, 1, 1), lambda b, h, c: (b, h, 0, 0))],
        out_shape=[jax.ShapeDtypeStruct((nb * t_len, nheads * dv), F32),
                   jax.ShapeDtypeStruct(c0.shape, F32),
                   jax.ShapeDtypeStruct(n0.shape, F32),
                   jax.ShapeDtypeStruct(m0.shape, F32)],
        scratch_shapes=[pltpu.VMEM((hb, dk, dv), F32), pltpu.VMEM((hb, 1, dk), F32),
                        pltpu.VMEM((hb, 1, 1), F32)],
        compiler_params=_cparams(("arbitrary", "arbitrary", "arbitrary"), vmem),
        name="mlstm",
    )(z, z, z, z, gates, gates_t, bias_row, bias_col, norm_w, c0, n0, m0)


def _merge_kernel(dp_ref, ds_ref, wp_ref, ps_ref, ga_ref, gb_ref, mop_ref, mos_ref, o_ref, *, gin, gout, npt):
    def run(d_ref, mo_ref):
        for gg in range(wp_ref.shape[0]):
            y = jnp.dot(d_ref[:, gg * gin:(gg + 1) * gin].astype(BF16), wp_ref[gg], preferred_element_type=F32)
            cs = slice(gg * gout, (gg + 1) * gout)
            pool_out = y * ps_ref[:, cs]
            merged = jax.nn.sigmoid(ga_ref[:, cs]) * pool_out + jax.nn.sigmoid(gb_ref[:, cs]) * mo_ref[:, cs]
            o_ref[:, cs] = merged.astype(BF16)

    i = pl.program_id(0)
    pl.when(i < npt)(lambda: run(dp_ref, mop_ref))
    pl.when(i >= npt)(lambda: run(ds_ref, mos_ref))


def _merge(d_p, d_s, w_pool_b, pool_scale, z, mo_p, mo_s, ga_off, gb_off, tok, tiles):
    cdim = d_p.shape[1]
    ng, gin, gout = w_pool_b.shape
    dm = ng * gout
    half = dm // 2
    gh = ng // 2
    tm = tiles.merge_tm
    npt = tok.n_prompt // tm
    nst = tok.n_sample // tm

    def prompt_spec(width):
        return pl.BlockSpec((tm, width), lambda i, j: (jnp.minimum(i, npt - 1), jnp.where(i < npt, j, 1)))

    def sample_spec(width):
        return pl.BlockSpec((tm, width), lambda i, j: (jnp.clip(i - npt, 0, nst - 1), jnp.where(i < npt, 0, j)))

    vmem = 2 * (2 * tm * cdim * 2 + gh * gin * gout * 2 + 4 * tm * half * 4 + tm * half * 2) + 6 * tm * gout * 4
    return pl.pallas_call(
        functools.partial(_merge_kernel, gin=gin, gout=gout, npt=npt),
        grid=(tok.n // tm, 2),
        in_specs=[prompt_spec(cdim // 2), sample_spec(cdim // 2),
                  pl.BlockSpec((gh, gin, gout), lambda i, j: (j, 0, 0)),
                  pl.BlockSpec((1, half), lambda i, j: (0, j)),
                  pl.BlockSpec((tm, half), lambda i, j: (i, ga_off // half + j)),
                  pl.BlockSpec((tm, half), lambda i, j: (i, gb_off // half + j)),
                  prompt_spec(half), sample_spec(half)],
        out_specs=pl.BlockSpec((tm, half), lambda i, j: (i, j)),
        out_shape=jax.ShapeDtypeStruct((tok.n, dm), BF16),
        compiler_params=_cparams(("arbitrary", "arbitrary"), vmem),
        name="merge",
    )(d_p, d_s, w_pool_b, pool_scale.reshape(1, dm), z, z, mo_p, mo_s)


def _ffn_norm_kernel(x1_ref, sc_ref, sh_ref, nw_ref, wr_ref, up_ref, re_ref, rw_ref, ub_ref, *, tok, tm):
    i = pl.program_id(0)
    d = x1_ref.shape[1]
    hd = d // 2

    def slab(s, carry):
        r = pl.multiple_of(s * BF16_ROWS, BF16_ROWS)
        rows = pl.ds(r, BF16_ROWS)
        row0 = i * tm + r
        u = (_rms(x1_ref[rows, :], nw_ref[...]) * (1.0 + _mod_rows(sc_ref, row0, tok, d))
             + _mod_rows(sh_ref, row0, tok, d))
        ub = u.astype(BF16)
        ub_ref[rows, :] = ub
        bits = lax.bitcast_convert_type(ub.astype(F32), U32)
        words = (bits[:, hd:] & jnp.uint32(0xFFFF0000)) | (bits[:, :hd] >> 16)
        nch = hd // LANES
        for c in range(nch):
            up_ref[pl.ds(r * nch + c, BF16_ROWS, stride=nch), :] = words[:, c * LANES:(c + 1) * LANES]
        return carry
    lax.fori_loop(0, tm // BF16_ROWS, slab, 0)

    lg = jnp.dot(ub_ref[...], wr_ref[...], preferred_element_type=F32)
    lane = lax.broadcasted_iota(I32, lg.shape, 1)
    lanef = lane.astype(F32)
    big = float(LANES)
    is_g = lane < N_GROUPS
    gl = jnp.where(is_g, lg, -jnp.inf)
    gmax = jnp.max(gl, axis=1, keepdims=True)
    grp = jnp.min(jnp.where(gl == gmax, lanef, big), axis=1, keepdims=True)
    p_grp = 1.0 / jnp.sum(jnp.where(is_g, jnp.exp(lg - gmax), 0.0), axis=1, keepdims=True)
    eidx = lane - N_GROUPS
    in_grp = (eidx >= 0) & (eidx < N_EXPERTS) & ((eidx >> EXP_SHIFT).astype(F32) == grp)
    el = jnp.where(in_grp, lg, -jnp.inf)
    v1 = jnp.max(el, axis=1, keepdims=True)
    i1 = jnp.min(jnp.where(el == v1, lanef, big), axis=1, keepdims=True)
    el2 = jnp.where(lanef == i1, -jnp.inf, el)
    v2 = jnp.max(el2, axis=1, keepdims=True)
    i2 = jnp.min(jnp.where(el2 == v2, lanef, big), axis=1, keepdims=True)
    e2 = jnp.exp(v2 - v1)
    w1 = (1.0 / (1.0 + e2)) * p_grp
    w2 = (e2 / (1.0 + e2)) * p_grp
    re_ref[...] = jnp.where(lane == 0, i1 - N_GROUPS, jnp.where(lane == 1, i2 - N_GROUPS, 0.0))
    rw_ref[...] = jnp.where(lane == 0, w1, jnp.where(lane == 1, w2, 0.0))


def _ffn_norm(x1, mod, norm_w, w_router, tok, tiles):
    d = x1.shape[1]
    g = mod.shape[0]
    tm = tiles.row_tm
    vmem = (2 * (tm * d * 4 + 2 * g * d * 4 + d * LANES * 2 + tm * d * 2 + 2 * tm * LANES * 4)
            + tm * d * 2 + 16 * tm * LANES * 4)
    return pl.pallas_call(
        functools.partial(_ffn_norm_kernel, tok=tok, tm=tm),
        grid=(tok.n // tm,),
        in_specs=[pl.BlockSpec((tm, d), lambda i: (i, 0)),
                  pl.BlockSpec((g, d), lambda i: (0, 4)),
                  pl.BlockSpec((g, d), lambda i: (0, 3)),
                  pl.BlockSpec((1, d), lambda i: (0, 0)),
                  pl.BlockSpec((d, LANES), lambda i: (0, 0))],
        out_specs=[pl.BlockSpec((tm * (d // 2 // LANES), LANES), lambda i: (i, 0)),
                   pl.BlockSpec((tm, LANES), lambda i: (i, 0)),
                   pl.BlockSpec((tm, LANES), lambda i: (i, 0))],
        out_shape=[jax.ShapeDtypeStruct((tok.n * (d // 2 // LANES), LANES), U32),
                   jax.ShapeDtypeStruct((tok.n, LANES), F32),
                   jax.ShapeDtypeStruct((tok.n, LANES), F32)],
        scratch_shapes=[pltpu.VMEM((tm, d), BF16)],
        compiler_params=_cparams(("arbitrary",), vmem),
        name="ffn_norm",
    )(x1, mod, mod, norm_w.reshape(1, d), w_router)


def _rank_kernel(re_ref, rank_ref, cnt_ref, carry_ref, *, tr):
    j = pl.program_id(0)
    i = pl.program_id(1)

    @pl.when((j == 0) & (i == 0))
    def _():
        carry_ref[...] = jnp.zeros_like(carry_ref)

    re = re_ref[...]
    lane = lax.broadcasted_iota(I32, re.shape, 1)
    e_col = jnp.sum(jnp.where(lane == j, re, 0.0), axis=1, keepdims=True)
    onehot = lane.astype(F32) == e_col
    rowi = lax.broadcasted_iota(I32, (tr, tr), 0)
    coli = lax.broadcasted_iota(I32, (tr, tr), 1)
    tri = jnp.where(rowi >= coli, 1.0, 0.0).astype(BF16)
    prefix = jnp.dot(tri, jnp.where(onehot, 1.0, 0.0).astype(BF16), preferred_element_type=F32)
    carry = carry_ref[0:1, :]
    rank = jnp.sum(jnp.where(onehot, prefix - 1.0 + carry, 0.0), axis=1, keepdims=True)
    rank_ref[...] = jnp.broadcast_to(rank, rank_ref.shape)
    new_carry = carry + prefix[tr - 1:tr, :]
    carry_ref[...] = jnp.broadcast_to(new_carry, carry_ref.shape)
    cnt_ref[...] = jnp.broadcast_to(new_carry, cnt_ref.shape)


def _rank(re, tiles):
    n = re.shape[0]
    tr = tiles.rank_tr
    nt = n // tr
    return pl.pallas_call(
        functools.partial(_rank_kernel, tr=tr),
        grid=(TOP_K, nt),
        in_specs=[pl.BlockSpec((tr, LANES), lambda j, i: (i, 0))],
        out_specs=[pl.BlockSpec((tr, LANES), lambda j, i: (j * nt + i, 0)),
                   pl.BlockSpec((SUBLANES, LANES), lambda j, i: (0, 0))],
        out_shape=[jax.ShapeDtypeStruct((TOP_K * n, LANES), F32),
                   jax.ShapeDtypeStruct((SUBLANES, LANES), F32)],
        scratch_shapes=[pltpu.VMEM((SUBLANES, LANES), F32)],
        compiler_params=_cparams(("arbitrary", "arbitrary"), 4 * MIB),
        name="rank",
    )(re)


def _invert_kernel(dest_ref, inv_ref, *, n_assign, n_rows):
    def clear(p, carry):
        inv_ref[p] = 0
        return carry
    lax.fori_loop(0, n_rows, clear, 0, unroll=16)

    def put(a, carry):
        inv_ref[dest_ref[a]] = a // TOP_K
        return carry
    lax.fori_loop(0, n_assign, put, 0, unroll=8)


def _invert(dest_flat, n_rows):
    return pl.pallas_call(
        functools.partial(_invert_kernel, n_assign=dest_flat.shape[0], n_rows=n_rows),
        in_specs=[pl.BlockSpec(memory_space=pltpu.SMEM)],
        out_specs=pl.BlockSpec(memory_space=pltpu.SMEM),
        out_shape=jax.ShapeDtypeStruct((n_rows,), I32),
        name="invert",
    )(dest_flat)


def _gather_kernel(nused_ref, inv_ref, src_ref, o_ref, buf, sems, *, bm, nch):
    b = pl.program_id(0)
    n_used = nused_ref[0]

    def issue(blk, slot):
        def body(r2, carry):
            for par in range(2):
                r = r2 * 2 + par
                t = inv_ref[blk * bm + r]
                pltpu.make_async_copy(src_ref.at[pl.ds(pl.multiple_of(t * nch, nch), nch)],
                                      buf.at[slot, pl.ds(pl.multiple_of(r * nch, nch), nch)],
                                      sems.at[slot]).start(priority=par)
            return carry
        lax.fori_loop(0, bm // 2, body, 0, unroll=4)

    @pl.when(b == 0)
    def _():
        issue(0, 0)

    @pl.when(b + 1 < n_used)
    def _():
        issue(b + 1, (b + 1) % 2)

    @pl.when(b < n_used)
    def _():
        slot = b % 2
        pltpu.make_async_copy(src_ref.at[pl.ds(0, bm * nch)], buf.at[slot], sems.at[slot]).wait()
        o_ref[...] = buf[slot]


def _gather(inv, n_used, src, nch, tiles):
    n_rows = inv.shape[0]
    width = src.shape[1]
    rows = tiles.moe_bm * nch
    return pl.pallas_call(
        functools.partial(_gather_kernel, bm=tiles.moe_bm, nch=nch),
        grid_spec=pltpu.PrefetchScalarGridSpec(
            num_scalar_prefetch=2,
            grid=(n_rows // tiles.moe_bm,),
            in_specs=[pl.BlockSpec(memory_space=pl.ANY)],
            out_specs=pl.BlockSpec((rows, width), lambda b, nu, inv: (jnp.minimum(b, nu[0] - 1), 0)),
            scratch_shapes=[pltpu.VMEM((2, rows, width), src.dtype), pltpu.SemaphoreType.DMA((2,))]),
        out_shape=jax.ShapeDtypeStruct((n_rows * nch, width), src.dtype),
        compiler_params=_cparams(("arbitrary",), 4 * rows * width * 4),
        name="gather",
    )(n_used, inv, src)


def _unpack_pair(words):
    lo = lax.bitcast_convert_type(words << 16, F32).astype(BF16)
    hi = lax.bitcast_convert_type(words & jnp.uint32(0xFFFF0000), F32).astype(BF16)
    return lo, hi


def _expert_weights(w, plan_refs, w_hbm, stage, sems, cw):
    _, col_ref, exp_ref, first_ref, _, slot_ref, nxt_e_ref, nxt_c_ref = plan_refs

    def copies(e, c, slot):
        cols = pl.ds(pl.multiple_of(c * cw, cw), cw)
        return [pltpu.make_async_copy(w_hbm[k].at[e, :, cols], stage.at[slot, k], sems.at[slot, k])
                for k in range(len(w_hbm))]

    @pl.when(w == 0)
    def _():
        for cp in copies(exp_ref[0], col_ref[0], 0):
            cp.start()

    @pl.when(first_ref[w] == 1)
    def _():
        for cp in copies(exp_ref[w], col_ref[w], slot_ref[w]):
            cp.wait()

        @pl.when(nxt_e_ref[w] >= 0)
        def _():
            for cp in copies(nxt_e_ref[w], nxt_c_ref[w], 1 - slot_ref[w]):
                cp.start()


def _moe_up_kernel(*refs, cw, bm, nch):
    plan_refs = refs[:8]
    xs_ref, wg_hbm, wu_hbm, h_ref, stage, wgb, wub, sems = refs[8:]
    first_ref, valid_ref, slot_ref = plan_refs[3], plan_refs[4], plan_refs[5]
    w = pl.program_id(0)
    hd = nch * LANES
    _expert_weights(w, plan_refs, (wg_hbm, wu_hbm), stage, sems, cw)

    @pl.when(first_ref[w] == 1)
    def _():
        wgb[...] = stage[slot_ref[w], 0].astype(BF16)
        wub[...] = stage[slot_ref[w], 1].astype(BF16)

    @pl.when(valid_ref[w] == 1)
    def _():
        words = jnp.concatenate([xs_ref[pl.ds(c, bm, stride=nch), :] for c in range(nch)], axis=1)
        lo, hi = _unpack_pair(words)
        g = (jnp.dot(lo, wgb[:hd, :], preferred_element_type=F32)
             + jnp.dot(hi, wgb[hd:, :], preferred_element_type=F32))
        u = (jnp.dot(lo, wub[:hd, :], preferred_element_type=F32)
             + jnp.dot(hi, wub[hd:, :], preferred_element_type=F32))
        h_ref[...] = ((g * jax.nn.sigmoid(g)) * u).astype(BF16)


def _moe_up(plan, xs, nch, w_gate, w_up, tiles):
    p_rows = xs.shape[0] // nch
    hd = nch * LANES
    _, d, de = w_gate.shape
    bm, cw = tiles.moe_bm, tiles.moe_cw
    n_items = plan["up"][0].shape[0]
    vmem = 2 * (bm * hd * 4 + bm * cw * 2) + 4 * d * cw * 4 + 2 * d * cw * 2 + 8 * bm * cw * 4 + 2 * bm * d * 2

    def by_item(f):
        return lambda w, blk, col, ex, fi, va, sl, ne, nc: f(w, blk, col)

    return pl.pallas_call(
        functools.partial(_moe_up_kernel, cw=cw, bm=bm, nch=nch),
        grid_spec=pltpu.PrefetchScalarGridSpec(
            num_scalar_prefetch=8,
            grid=(n_items,),
            in_specs=[pl.BlockSpec((bm * nch, LANES), by_item(lambda w, blk, col: (blk[w], 0))),
                      pl.BlockSpec(memory_space=pl.ANY),
                      pl.BlockSpec(memory_space=pl.ANY)],
            out_specs=pl.BlockSpec((bm, cw), by_item(lambda w, blk, col: (blk[w], col[w]))),
            scratch_shapes=[pltpu.VMEM((2, 2, d, cw), F32), pltpu.VMEM((d, cw), BF16), pltpu.VMEM((d, cw), BF16),
                            pltpu.SemaphoreType.DMA((2, 2))]),
        out_shape=jax.ShapeDtypeStruct((p_rows, de), BF16),
        compiler_params=_cparams(("arbitrary",), vmem),
        name="moe_up",
    )(*plan["up"], xs, w_gate, w_up)


def _moe_down_kernel(*refs, dcw):
    plan_refs = refs[:8]
    h_ref, wd_hbm, y_ref, stage, wdb, sems = refs[8:]
    first_ref, valid_ref, slot_ref = plan_refs[3], plan_refs[4], plan_refs[5]
    w = pl.program_id(0)
    _expert_weights(w, plan_refs, (wd_hbm,), stage, sems, dcw)

    @pl.when(first_ref[w] == 1)
    def _():
        wdb[...] = stage[slot_ref[w], 0].astype(BF16)

    @pl.when(valid_ref[w] == 1)
    def _():
        y_ref[...] = jnp.dot(h_ref[...], wdb[...], preferred_element_type=F32)


def _moe_down(plan, hdn, w_down, tiles):
    p_rows, de = hdn.shape
    d = w_down.shape[2]
    bm, dcw = tiles.moe_bm, tiles.moe_dcw
    n_items = plan["down"][0].shape[0]
    vmem = 2 * (bm * de * 2 + bm * dcw * 4) + 2 * de * dcw * 4 + de * dcw * 2 + 2 * bm * dcw * 4

    def by_item(f):
        return lambda w, blk, col, ex, fi, va, sl, ne, nc: f(w, blk, col)

    return pl.pallas_call(
        functools.partial(_moe_down_kernel, dcw=dcw),
        grid_spec=pltpu.PrefetchScalarGridSpec(
            num_scalar_prefetch=8,
            grid=(n_items,),
            in_specs=[pl.BlockSpec((bm, de), by_item(lambda w, blk, col: (blk[w], 0))),
                      pl.BlockSpec(memory_space=pl.ANY)],
            out_specs=pl.BlockSpec((bm, dcw), by_item(lambda w, blk, col: (blk[w], col[w]))),
            scratch_shapes=[pltpu.VMEM((2, 1, de, dcw), F32), pltpu.VMEM((de, dcw), BF16),
                            pltpu.SemaphoreType.DMA((2, 1))]),
        out_shape=jax.ShapeDtypeStruct((p_rows, d), F32),
        compiler_params=_cparams(("arbitrary",), vmem),
        name="moe_down",
    )(*plan["down"], hdn, w_down)


def _work_items(nb_e, blk0_e, n_cols, n_items):
    items_e = nb_e * n_cols
    cum = jnp.cumsum(items_e)
    total = cum[-1]
    w = jnp.minimum(jnp.arange(n_items, dtype=I32), total - 1)
    own = (w[:, None] >= (cum - items_e)[None, :]) & (w[:, None] < cum[None, :])

    def pick(v):
        return jnp.sum(jnp.where(own, v[None, :], 0), axis=1)

    e = pick(jnp.arange(N_EXPERTS, dtype=I32))
    r = w - pick(cum - items_e)
    nb = jnp.maximum(pick(nb_e), 1)
    col = r // nb
    blk = pick(blk0_e) + r % nb
    valid = (jnp.arange(n_items, dtype=I32) < total).astype(I32)
    key = e * n_cols + col
    first = jnp.concatenate([jnp.ones((1,), I32), (key[1:] != key[:-1]).astype(I32)]) * valid
    slot = (jnp.cumsum(first) - 1) % 2
    idx = jnp.arange(n_items, dtype=I32)
    nxt = lax.cummin(jnp.where(first == 1, idx, n_items)[::-1])[::-1]
    nxt = jnp.concatenate([nxt[1:], jnp.full((1,), n_items, I32)])
    has_next = nxt < n_items
    pick_next = nxt[:, None] == idx[None, :]
    nxt_e = jnp.where(has_next, jnp.sum(jnp.where(pick_next, e[None, :], 0), axis=1), -1)
    nxt_c = jnp.where(has_next, jnp.sum(jnp.where(pick_next, col[None, :], 0), axis=1), 0)
    return (blk.astype(I32), col.astype(I32), e.astype(I32), first.astype(I32), valid, slot.astype(I32),
            nxt_e.astype(I32), nxt_c.astype(I32))


def _dispatch_plan(re, rank_out, cnt_out, n, n_blocks, d_expert, d_model, tiles):
    bm = tiles.moe_bm
    eid = re[:, :TOP_K].astype(I32)
    rank = rank_out[:, 0].reshape(TOP_K, n).T.astype(I32)
    counts = cnt_out[0, :N_EXPERTS].astype(I32)
    nb_e = (counts + bm - 1) // bm
    blk0_e = jnp.cumsum(nb_e) - nb_e
    row0 = jnp.sum(jnp.where(eid[:, :, None] == jnp.arange(N_EXPERTS, dtype=I32), blk0_e * bm, 0), axis=2)
    dest = row0 + rank
    plan = {
        "dest": dest.reshape(-1).astype(I32),
        "n_used": jnp.sum(nb_e).reshape(1).astype(I32),
        "up": _work_items(nb_e, blk0_e, d_expert // tiles.moe_cw, n_blocks * (d_expert // tiles.moe_cw)),
        "down": _work_items(nb_e, blk0_e, d_model // tiles.moe_dcw, n_blocks * (d_model // tiles.moe_dcw)),
    }
    return plan


def _combine_kernel(dest_ref, ys_ref, x1_ref, rw_ref, g2_ref, fw_ref, yp_ref, ysm_ref, ybuf, sems, *, tok, tm):
    i = pl.program_id(0)
    d = x1_ref.shape[1]
    base = i * tm
    nt = pl.num_programs(0)

    def issue(tile, slot):
        def body(r, carry):
            for j in range(TOP_K):
                pltpu.make_async_copy(ys_ref.at[pl.ds(dest_ref[(tile * tm + r) * TOP_K + j], 1)],
                                      ybuf.at[slot, j, pl.ds(r, 1)], sems.at[slot, j]).start(priority=j % 2)
            return carry
        lax.fori_loop(0, tm, body, 0, unroll=4)

    @pl.when(i == 0)
    def _():
        issue(0, 0)

    @pl.when(i + 1 < nt)
    def _():
        issue(i + 1, (i + 1) % 2)

    slot = i % 2
    for j in range(TOP_K):
        pltpu.make_async_copy(ys_ref.at[pl.ds(0, tm)], ybuf.at[slot, j], sems.at[slot, j]).wait()

    def run(out_ref):
        def slab(s, carry):
            r = pl.multiple_of(s * SUBLANES, SUBLANES)
            rows = pl.ds(r, SUBLANES)
            wts = rw_ref[rows, :]
            ff = wts[:, 0:1] * ybuf[slot, 0, rows, :] + wts[:, 1:2] * ybuf[slot, 1, rows, :]
            g2 = jnp.broadcast_to(g2_ref[pl.ds(_batch_of(base + r, tok), 1), :], (SUBLANES, d))
            out_ref[rows, :] = _rms(x1_ref[rows, :] + g2 * ff, fw_ref[...])
            return carry
        lax.fori_loop(0, tm // SUBLANES, slab, 0)

    npt = tok.n_prompt // tm
    pl.when(i < npt)(lambda: run(yp_ref))
    pl.when(i >= npt)(lambda: run(ysm_ref))


def _combine(dest_flat, ys, x1, rw, mod, final_w, tok, tiles):
    d = x1.shape[1]
    g = mod.shape[0]
    tm = tiles.comb_tm
    npt = tok.n_prompt // tm
    nst = tok.n_sample // tm
    vmem = 2 * (tm * d * 4 + tm * LANES * 4 + g * d * 4 + 2 * tm * d * 4) + 2 * TOP_K * tm * d * 4
    return pl.pallas_call(
        functools.partial(_combine_kernel, tok=tok, tm=tm),
        grid_spec=pltpu.PrefetchScalarGridSpec(
            num_scalar_prefetch=1,
            grid=(tok.n // tm,),
            in_specs=[pl.BlockSpec(memory_space=pl.ANY),
                      pl.BlockSpec((tm, d), lambda i, dest: (i, 0)),
                      pl.BlockSpec((tm, LANES), lambda i, dest: (i, 0)),
                      pl.BlockSpec((g, d), lambda i, dest: (0, 5)),
                      pl.BlockSpec((1, d), lambda i, dest: (0, 0))],
            out_specs=[pl.BlockSpec((tm, d), lambda i, dest: (jnp.minimum(i, npt - 1), 0)),
                       pl.BlockSpec((tm, d), lambda i, dest: (jnp.clip(i - npt, 0, nst - 1), 0))],
            scratch_shapes=[pltpu.VMEM((2, TOP_K, tm, d), F32), pltpu.SemaphoreType.DMA((2, TOP_K))]),
        out_shape=[jax.ShapeDtypeStruct((tok.n_prompt, d), F32),
                   jax.ShapeDtypeStruct((tok.n_sample, d), F32)],
        compiler_params=_cparams(("arbitrary",), vmem),
        name="combine",
    )(dest_flat, ys, x1, rw, mod, final_w.reshape(1, d))


def _gate_layouts(gates, n_rows, row_off, L, n_gate):
    g = gates[row_off:row_off + n_rows, :n_gate]
    return g.reshape(n_rows // L, L, n_gate).transpose(0, 2, 1)


def _layer(xp, xs, c_all, pool_s, c_s, n_s, m_s, lw, tok, tiles):
    (w_ada, b_ada, norm_mix_w, w_in, b_igate, b_fgate, w_pool, pool_scale, mlstm_norm_w, w_out, norm_ffn_w,
     w_router_group, w_router_expert, w_exp_gate, w_exp_up, w_exp_down) = lw
    d = xp.shape[1]
    nheads, dk, dv = c_s.shape[1:]
    pool_in = pool_s.shape[2]
    mqk, mv = nheads * dk, nheads * dv
    offs = (pool_in, pool_in + mqk, pool_in + 2 * mqk, pool_in + 2 * mqk + mv)
    ga_off = offs[3] + mv
    gb_off = ga_off + d
    n_main = gb_off + d
    n_gate = 2 * nheads
    n = tok.n

    mod = _ada(c_all, w_ada, b_ada, tiles)
    w_in_t = w_in.T
    u, gates = _prenorm(xp, xs, mod, norm_mix_w, w_in_t, n_main, n_gate, tok, tiles)
    z = _matmul(u, w_in_t, n_main, True, tiles)

    zeros_pool = jnp.zeros((tok.nb_p, POOL_BUF, pool_in), F32)
    d_p, pool_new_p = _pool(z, zeros_pool, 0, tok.nb_p, tok.rows_p, 1, tiles.pool_tt, 0)
    d_s, pool_new_s = _pool(z, pool_s, tok.n_prompt, tok.nb_s, tok.rows_s, tiles.pool_bt, tok.rows_s, PAST_LEN)

    bias = jnp.concatenate([b_igate, b_fgate])
    bias_row = jnp.pad(bias, (0, LANES - n_gate)).reshape(1, LANES)
    bias_col = bias.reshape(n_gate, 1)
    nw = mlstm_norm_w.reshape(1, mv)
    lp = tiles.mlstm_l
    zc = jnp.zeros((tok.nb_p, nheads, dk, dv), F32)
    zn = jnp.zeros((tok.nb_p, nheads, 1, dk), F32)
    zm = jnp.zeros((tok.nb_p, nheads, 1, 1), F32)
    mo_p, c_p, n_p, m_p = _mlstm(z, gates, _gate_layouts(gates, tok.n_prompt, 0, lp, n_gate), bias_row, bias_col,
                                 nw, zc, zn, zm, 0, tok.nb_p, tok.rows_p, lp, tiles.mlstm_hb_p, offs)
    ls = tok.rows_s
    mo_s, c_n, n_n, m_n = _mlstm(z, gates, _gate_layouts(gates, tok.n_sample, tok.n_prompt, ls, n_gate), bias_row,
                                 bias_col, nw, c_s, n_s.reshape(tok.nb_s, nheads, 1, dk),
                                 m_s.reshape(tok.nb_s, nheads, 1, 1), tok.n_prompt, tok.nb_s, ls, ls,
                                 tiles.mlstm_hb_s, offs)

    merged = _merge(d_p, d_s, w_pool.astype(BF16), pool_scale, z, mo_p, mo_s, ga_off, gb_off, tok, tiles)
    x1 = _out_proj(merged, w_out, xp, xs, mod, tok, tiles)

    w_router = jnp.pad(jnp.concatenate([w_router_group, w_router_expert], axis=1),
                       ((0, 0), (0, LANES - N_GROUPS - N_EXPERTS))).astype(BF16)
    u2p, re, rw = _ffn_norm(x1, mod, norm_ffn_w, w_router, tok, tiles)
    rank_out, cnt_out = _rank(re, tiles)
    d_expert = w_exp_gate.shape[2]
    n_blocks = -(-(n * TOP_K) // tiles.moe_bm) + N_EXPERTS
    plan = _dispatch_plan(re, rank_out, cnt_out, n, n_blocks, d_expert, d, tiles)
    inv = _invert(plan["dest"], n_blocks * tiles.moe_bm)
    nch = d // 2 // LANES
    xsort = _gather(inv, plan["n_used"], u2p, nch, tiles)
    hdn = _moe_up(plan, xsort, nch, w_exp_gate, w_exp_up, tiles)
    ysort = _moe_down(plan, hdn, w_exp_down, tiles)
    states = (pool_new_p, c_p, n_p.reshape(tok.nb_p, nheads, dk), m_p.reshape(tok.nb_p, nheads),
              pool_new_s, c_n, n_n.reshape(tok.nb_s, nheads, dk), m_n.reshape(tok.nb_s, nheads))
    return (plan["dest"], ysort, x1, rw, mod), states


def _forward(x_prompt, x_sample, c_prompt, c_sample, state_pool, state_mlstm_C, state_mlstm_n, state_mlstm_m,
             w_ada, b_ada, norm_mix_w, w_in, b_igate, b_fgate, w_pool, pool_scale, mlstm_norm_w, w_out,
             norm_ffn_w, w_router_group, w_router_expert, w_exp_gate, w_exp_up, w_exp_down, final_norm_w,
             tiles=Tiles()):
    nb_p, rows_p, d = x_prompt.shape
    nb_s, rows_s, _ = x_sample.shape
    depth = w_ada.shape[0]
    assert depth == 1, "the merged-token pipeline is written for a single layer"
    tok = Tok(nb_p * rows_p, nb_s * rows_s, rows_p, rows_s, nb_p, nb_s)
    for tm in (tiles.row_tm, tiles.mm_tm, tiles.merge_tm, tiles.rank_tr, tiles.comb_tm):
        assert tok.n_prompt % tm == 0 and tok.n_sample % tm == 0
    xp = x_prompt.reshape(tok.n_prompt, d)
    xs = x_sample.reshape(tok.n_sample, d)
    g = nb_p + nb_s
    g_pad = -(-g // SUBLANES) * SUBLANES
    c_all = jnp.pad(jnp.concatenate([c_prompt, c_sample], axis=0), ((0, g_pad - g), (0, 0)))
    lw = (w_ada[0], b_ada[0], norm_mix_w[0], w_in[0], b_igate[0], b_fgate[0], w_pool[0], pool_scale[0],
          mlstm_norm_w[0], w_out[0], norm_ffn_w[0], w_router_group[0], w_router_expert[0], w_exp_gate[0],
          w_exp_up[0], w_exp_down[0])
    (dest, ysort, x1, rw, mod), st = _layer(xp, xs, c_all, state_pool[0], state_mlstm_C[0], state_mlstm_n[0],
                                            state_mlstm_m[0], lw, tok, tiles)
    y_p, y_s = _combine(dest, ysort, x1, rw, mod, final_norm_w, tok, tiles)
    return (y_p.reshape(x_prompt.shape), y_s.reshape(x_sample.shape)) + tuple(s[None] for s in st)


def kernel(x_prompt, x_sample, c_prompt, c_sample, state_pool, state_mlstm_C, state_mlstm_n, state_mlstm_m,
           w_ada, b_ada, norm_mix_w, w_in, b_igate, b_fgate, w_pool, pool_scale, mlstm_norm_w, w_out,
           norm_ffn_w, w_router_group, w_router_expert, w_exp_gate, w_exp_up, w_exp_down, final_norm_w):
    return _forward(x_prompt, x_sample, c_prompt, c_sample, state_pool, state_mlstm_C, state_mlstm_n,
                    state_mlstm_m, w_ada, b_ada, norm_mix_w, w_in, b_igate, b_fgate, w_pool, pool_scale,
                    mlstm_norm_w, w_out, norm_ffn_w, w_router_group, w_router_expert, w_exp_gate, w_exp_up,
                    w_exp_down, final_norm_w)
```

```python
import functools
from typing import NamedTuple

import jax
import jax.numpy as jnp
from jax import lax
from jax.experimental import pallas as pl
from jax.experimental.pallas import tpu as pltpu

F32, BF16, I32, U32 = jnp.float32, jnp.bfloat16, jnp.int32, jnp.uint32

EPS = 1e-6
PAST_LEN = 16384
POOL_WINDOWS = (2, 4, 8, 16)
POOL_BUF = 15
N_GROUPS = 8
EXP_PER_GROUP = 8
N_EXPERTS = N_GROUPS * EXP_PER_GROUP
EXP_SHIFT = EXP_PER_GROUP.bit_length() - 1
TOP_K = 2

V7X_VMEM_BYTES = 64 * 2**20
LANES = 128
SUBLANES = 8
BF16_ROWS = 16
MIB = 2**20


class Tok(NamedTuple):
    n_prompt: int
    n_sample: int
    rows_p: int
    rows_s: int
    nb_p: int
    nb_s: int

    @property
    def n(self):
        return self.n_prompt + self.n_sample


class Tiles(NamedTuple):
    ada_tn: int = 512
    row_tm: int = 256
    mm_tm: int = 512
    mm_tn: int = 1024
    pool_tt: int = 256
    pool_bt: int = 16
    mlstm_l: int = 256
    mlstm_hb_p: int = 2
    mlstm_hb_s: int = 4
    mlstm_bb_s: int = 2
    merge_tm: int = 512
    rank_tr: int = 256
    moe_bm: int = 256
    moe_cw: int = 512
    moe_dcw: int = 2048
    comb_tm: int = 256


def _cparams(sem, vmem_bytes):
    limit = int(min(max(vmem_bytes * 5 // 4 + 2 * MIB, 16 * MIB), V7X_VMEM_BYTES - 6 * MIB))
    return pltpu.CompilerParams(dimension_semantics=sem, vmem_limit_bytes=limit)


def _batch_of(row, tok):
    return jnp.where(row < tok.n_prompt, row // tok.rows_p,
                     tok.nb_p + (row - tok.n_prompt) // tok.rows_s)


def _mod_rows(ref, row0, tok, d):
    top = jnp.broadcast_to(ref[pl.ds(_batch_of(row0, tok), 1), :], (SUBLANES, d))
    bot = jnp.broadcast_to(ref[pl.ds(_batch_of(row0 + SUBLANES, tok), 1), :], (SUBLANES, d))
    return jnp.concatenate([top, bot], axis=0)


def _rms(x, w):
    r = lax.rsqrt(jnp.mean(x * x, axis=-1, keepdims=True) + EPS)
    return (x * r) * w


def _log_sigmoid(x):
    return jnp.minimum(x, 0.0) - jnp.log1p(jnp.exp(-jnp.abs(x)))


def _ada_kernel(c_ref, w_ref, b_ref, o_ref):
    c = c_ref[...]
    a = (c * jax.nn.sigmoid(c)).astype(BF16)
    o_ref[...] = jnp.dot(a, w_ref[...].astype(BF16), preferred_element_type=F32) + b_ref[...]


def _ada(c_all, w_ada, b_ada, tiles):
    g, d = c_all.shape
    n = w_ada.shape[1]
    tn = tiles.ada_tn
    vmem = 2 * (g * d * 4 + d * tn * 4 + g * tn * 4) + d * tn * 2
    return pl.pallas_call(
        _ada_kernel,
        grid=(n // tn,),
        in_specs=[pl.BlockSpec((g, d), lambda j: (0, 0)),
                  pl.BlockSpec((d, tn), lambda j: (0, j)),
                  pl.BlockSpec((1, tn), lambda j: (0, j))],
        out_specs=pl.BlockSpec((g, tn), lambda j: (0, j)),
        out_shape=jax.ShapeDtypeStruct((g, n), F32),
        compiler_params=_cparams(("arbitrary",), vmem),
        name="ada",
    )(c_all, w_ada, b_ada.reshape(1, n))


def _dual_specs(tok, tm, d):
    npt = tok.n_prompt // tm
    nst = tok.n_sample // tm
    return (pl.BlockSpec((tm, d), lambda i: (jnp.minimum(i, npt - 1), 0)),
            pl.BlockSpec((tm, d), lambda i: (jnp.clip(i - npt, 0, nst - 1), 0)))


def _prenorm_kernel(xp_ref, xs_ref, sc_ref, sh_ref, nw_ref, wg_ref, u_ref, gates_ref, wgb_ref, *, tok, tm, n_gate):
    i = pl.program_id(0)
    d = u_ref.shape[1]

    @pl.when(i == 0)
    def _():
        wgb_ref[0:n_gate, :] = wg_ref[...].astype(BF16)
        wgb_ref[n_gate:, :] = jnp.zeros((LANES - n_gate, d), BF16)

    def run(x_ref):
        def slab(s, carry):
            r = pl.multiple_of(s * BF16_ROWS, BF16_ROWS)
            row0 = i * tm + r
            xn = _rms(x_ref[pl.ds(r, BF16_ROWS), :], nw_ref[...])
            u = xn * (1.0 + _mod_rows(sc_ref, row0, tok, d)) + _mod_rows(sh_ref, row0, tok, d)
            u_ref[pl.ds(r, BF16_ROWS), :] = u.astype(BF16)
            return carry
        lax.fori_loop(0, tm // BF16_ROWS, slab, 0)

    npt = tok.n_prompt // tm
    pl.when(i < npt)(lambda: run(xp_ref))
    pl.when(i >= npt)(lambda: run(xs_ref))
    gates_ref[...] = lax.dot_general(u_ref[...], wgb_ref[...], (((1,), (1,)), ((), ())),
                                     preferred_element_type=F32)


def _prenorm(xp, xs, mod, norm_w, w_in_t, n_main, n_gate, tok, tiles):
    d = xp.shape[1]
    g = mod.shape[0]
    tm = tiles.row_tm
    assert n_main % n_gate == 0 and n_gate % BF16_ROWS == 0 and n_gate <= LANES
    xp_spec, xs_spec = _dual_specs(tok, tm, d)
    vmem = 2 * (2 * tm * d * 4 + 2 * g * d * 4 + n_gate * d * 4 + tm * d * 2 + tm * LANES * 4) + d * LANES * 2
    return pl.pallas_call(
        functools.partial(_prenorm_kernel, tok=tok, tm=tm, n_gate=n_gate),
        grid=(tok.n // tm,),
        in_specs=[xp_spec, xs_spec,
                  pl.BlockSpec((g, d), lambda i: (0, 1)),
                  pl.BlockSpec((g, d), lambda i: (0, 0)),
                  pl.BlockSpec((1, d), lambda i: (0, 0)),
                  pl.BlockSpec((n_gate, d), lambda i: (n_main // n_gate, 0))],
        out_specs=[pl.BlockSpec((tm, d), lambda i: (i, 0)),
                   pl.BlockSpec((tm, LANES), lambda i: (i, 0))],
        out_shape=[jax.ShapeDtypeStruct((tok.n, d), BF16),
                   jax.ShapeDtypeStruct((tok.n, LANES), F32)],
        scratch_shapes=[pltpu.VMEM((LANES, d), BF16)],
        compiler_params=_cparams(("arbitrary",), vmem),
        name="prenorm",
    )(xp, xs, mod, mod, norm_w.reshape(1, d), w_in_t)


def _matmul_kernel(a_ref, w_ref, o_ref, wb_ref, *, w_is_nk):
    @pl.when(pl.program_id(1) == 0)
    def _():
        wb_ref[...] = w_ref[...].astype(BF16)
    contract_w = 1 if w_is_nk else 0
    o_ref[...] = lax.dot_general(a_ref[...], wb_ref[...], (((1,), (contract_w,)), ((), ())),
                                 preferred_element_type=F32)


def _matmul(a, w, n_cols, w_is_nk, tiles):
    m, k = a.shape
    tm, tn = tiles.mm_tm, tiles.mm_tn
    w_block = (tn, k) if w_is_nk else (k, tn)
    w_index = (lambda j, i: (j, 0)) if w_is_nk else (lambda j, i: (0, j))
    vmem = 2 * (tm * k * 2 + k * tn * 4 + tm * tn * 4) + k * tn * 2
    return pl.pallas_call(
        functools.partial(_matmul_kernel, w_is_nk=w_is_nk),
        grid=(n_cols // tn, m // tm),
        in_specs=[pl.BlockSpec((tm, k), lambda j, i: (i, 0)),
                  pl.BlockSpec(w_block, w_index)],
        out_specs=pl.BlockSpec((tm, tn), lambda j, i: (i, j)),
        out_shape=jax.ShapeDtypeStruct((m, n_cols), F32),
        scratch_shapes=[pltpu.VMEM(w_block, BF16)],
        compiler_params=_cparams(("arbitrary", "arbitrary"), vmem),
        name="matmul",
    )(a, w)


def _pool_kernel(p_ref, st_ref, d_ref, new_ref, ext_ref, *, bt, tt, gin, pos0, nt):
    t = pl.program_id(1)
    cdim = ext_ref.shape[2]
    halo = POOL_BUF + 1

    @pl.when(t == 0)
    def _():
        ext_ref[:, 1:halo, :] = st_ref[...]

    @pl.when(t > 0)
    def _():
        ext_ref[:, 0:halo, :] = ext_ref[:, tt:tt + halo, :]

    ext_ref[:, halo:halo + tt, :] = p_ref[...].reshape(bt, tt, cdim)
    pos = pos0 + t * tt + lax.broadcasted_iota(I32, (bt, tt, gin), 1)
    for g, w in enumerate(POOL_WINDOWS):
        cs = slice(g * gin, (g + 1) * gin)
        cur = ext_ref[:, halo:halo + tt, cs]
        acc = cur
        for j in range(1, w):
            acc = acc + ext_ref[:, halo - j:halo - j + tt, cs]
        cnt = jnp.minimum(pos + 1, w).astype(F32)
        d_ref[:, cs] = (acc / cnt - cur).reshape(bt * tt, gin)

    @pl.when(t == nt - 1)
    def _():
        new_ref[...] = ext_ref[:, tt + 1:tt + halo, :]


def _pool(z, state, row_off, nb, t_len, bt, tt, pos0):
    cdim = state.shape[2]
    gin = cdim // len(POOL_WINDOWS)
    nt = t_len // tt
    rows = bt * tt
    rb0 = row_off // rows
    vmem = 2 * (2 * rows * cdim * 4 + 2 * bt * 16 * cdim * 4) + bt * (tt + 16) * cdim * 4 + 8 * rows * gin * 4
    return pl.pallas_call(
        functools.partial(_pool_kernel, bt=bt, tt=tt, gin=gin, pos0=pos0, nt=nt),
        grid=(nb // bt, nt),
        in_specs=[pl.BlockSpec((rows, cdim), lambda b, t: (rb0 + b * nt + t, 0)),
                  pl.BlockSpec((bt, POOL_BUF, cdim), lambda b, t: (b, 0, 0))],
        out_specs=[pl.BlockSpec((rows, cdim), lambda b, t: (b * nt + t, 0)),
                   pl.BlockSpec((bt, POOL_BUF, cdim), lambda b, t: (b, 0, 0))],
        out_shape=[jax.ShapeDtypeStruct((nb * t_len, cdim), F32),
                   jax.ShapeDtypeStruct((nb, POOL_BUF, cdim), F32)],
        scratch_shapes=[pltpu.VMEM((bt, tt + POOL_BUF + 1, cdim), F32)],
        compiler_params=_cparams(("arbitrary", "arbitrary"), vmem),
        name="pool",
    )(z, state)


def _mlstm_kernel(q_ref, k_ref, v_ref, o_ref, gc_ref, gr_ref, brow_ref, bcol_ref, nw_ref, c0_ref, n0_ref, m0_ref,
                  mo_ref, c_out, n_out, m_out, c_s, n_s, m_s, *, L, bb, hb, dk, dv, nc, nheads):
    hblk = pl.program_id(1)
    c = pl.program_id(2)

    @pl.when(c == 0)
    def _():
        c_s[...] = c0_ref[...]
        n_s[...] = n0_ref[...]
        m_s[...] = m0_ref[...]

    lane = lax.broadcasted_iota(I32, (L, LANES), 1)
    sub = lax.broadcasted_iota(I32, (2 * nheads, L), 0)
    rowi = lax.broadcasted_iota(I32, (L, L), 0)
    coli = lax.broadcasted_iota(I32, (L, L), 1)
    tri = rowi >= coli
    scale = dk ** -0.5

    units = [(bi, hh) for bi in range(bb) for hh in range(hb)]
    heads = range(len(units))
    rsl = [slice(bi * L, (bi + 1) * L) for bi, _ in units]
    gcb = [gc_ref[rsl[i], :] + brow_ref[...] for i in heads]
    grb = [gr_ref[units[i][0]] + bcol_ref[...] for i in heads]
    hs_dyn = [hblk * hb + hh for _, hh in units]
    li_col = [jnp.sum(jnp.where(lane == hs_dyn[i], gcb[i], 0.0), axis=1, keepdims=True) for i in heads]
    lf_col = [jnp.sum(jnp.where(lane == hs_dyn[i] + nheads, gcb[i], 0.0), axis=1, keepdims=True) for i in heads]
    li_row = [jnp.sum(jnp.where(sub == hs_dyn[i], grb[i], 0.0), axis=0, keepdims=True) for i in heads]
    lf_row = [jnp.sum(jnp.where(sub == hs_dyn[i] + nheads, grb[i], 0.0), axis=0, keepdims=True) for i in heads]
    lf_col = [_log_sigmoid(x) for x in lf_col]
    lf_row = [_log_sigmoid(x) for x in lf_row]
    b_col = [jnp.sum(jnp.where(tri, x, 0.0), axis=1, keepdims=True) for x in lf_row]
    b_row = [jnp.sum(jnp.where(rowi <= coli, x, 0.0), axis=0, keepdims=True) for x in lf_col]

    q = [q_ref[rsl[i], hh * dk:(hh + 1) * dk] * scale for i, (_, hh) in enumerate(units)]
    k = [k_ref[rsl[i], hh * dk:(hh + 1) * dk] for i, (_, hh) in enumerate(units)]
    qb = [x.astype(BF16) for x in q]
    kb = [x.astype(BF16) for x in k]
    vb = [v_ref[rsl[i], hh * dv:(hh + 1) * dv].astype(BF16) for i, (_, hh) in enumerate(units)]
    cmat = [c_s[bi, hh] for bi, hh in units]
    nvec = [n_s[bi, hh] for bi, hh in units]
    m_prev = [m_s[bi, hh] for bi, hh in units]

    dmat = [jnp.where(tri, b_col[i] - b_row[i] + li_row[i], -jnp.inf) for i in heads]
    inter = [b_col[i] + m_prev[i] for i in heads]
    m_t = [jnp.maximum(inter[i], jnp.max(dmat[i], axis=1, keepdims=True)) for i in heads]
    dw = [jnp.exp(dmat[i] - m_t[i]) for i in heads]
    iw = [jnp.exp(inter[i] - m_t[i]) for i in heads]
    qk = [lax.dot_general(qb[i], kb[i], (((1,), (1,)), ((), ())), preferred_element_type=F32) for i in heads]
    qc = [jnp.dot(qb[i], cmat[i].astype(BF16), preferred_element_type=F32) for i in heads]
    s = [qk[i] * dw[i] for i in heads]
    num = [jnp.dot(s[i].astype(BF16), vb[i], preferred_element_type=F32) + iw[i] * qc[i] for i in heads]
    den = [jnp.sum(s[i], axis=1, keepdims=True) + iw[i] * jnp.sum(q[i] * nvec[i], axis=1, keepdims=True)
           for i in heads]
    hval = [num[i] / jnp.maximum(jnp.abs(den[i]), jnp.exp(-m_t[i])) for i in heads]
    for i, (_, hh) in enumerate(units):
        hs = slice(hh * dv, (hh + 1) * dv)
        mo_ref[rsl[i], hs] = jax.nn.sigmoid(o_ref[rsl[i], hs]) * _rms(hval[i], nw_ref[:, hs])

    b_last = [x[L - 1:L, :] for x in b_col]
    dl_col = [b_last[i] - b_col[i] + li_col[i] for i in heads]
    dl_row = [b_last[i] - b_row[i] + li_row[i] for i in heads]
    m_new = [jnp.maximum(b_last[i] + m_prev[i], jnp.max(dl_row[i], axis=1, keepdims=True)) for i in heads]
    dec = [jnp.exp(b_last[i] + m_prev[i] - m_new[i]) for i in heads]
    kk = [jnp.exp(dl_col[i] - m_new[i]) * k[i] for i in heads]
    upd = [lax.dot_general(kk[i].astype(BF16), vb[i], (((0,), (0,)), ((), ())), preferred_element_type=F32)
           for i in heads]
    for i, (bi, hh) in enumerate(units):
        c_s[bi, hh] = dec[i] * cmat[i] + upd[i]
        n_s[bi, hh] = dec[i] * nvec[i] + jnp.sum(kk[i], axis=0, keepdims=True)
        m_s[bi, hh] = m_new[i]

    @pl.when(c == nc - 1)
    def _():
        c_out[...] = c_s[...]
        n_out[...] = n_s[...]
        m_out[...] = m_s[...]


def _mlstm(z, gates, gates_t, bias_row, bias_col, norm_w, c0, n0, m0, row_off, nb, t_len, L, bb, hb, offs):
    _, nheads, dk, dv = c0.shape
    nc = t_len // L
    assert bb == 1 or nc == 1
    rows = bb * L
    rb0 = row_off // rows
    nhb = nheads // hb
    q_off, k_off, v_off, o_off = offs
    wq, wv = hb * dk, hb * dv

    def rowblk(b, c):
        return rb0 + b * nc + c

    in_specs = [
        pl.BlockSpec((rows, wq), lambda b, h, c: (rowblk(b, c), q_off // wq + h)),
        pl.BlockSpec((rows, wq), lambda b, h, c: (rowblk(b, c), k_off // wq + h)),
        pl.BlockSpec((rows, wv), lambda b, h, c: (rowblk(b, c), v_off // wv + h)),
        pl.BlockSpec((rows, wv), lambda b, h, c: (rowblk(b, c), o_off // wv + h)),
        pl.BlockSpec((rows, LANES), lambda b, h, c: (rowblk(b, c), 0)),
        pl.BlockSpec((bb, 2 * nheads, L), lambda b, h, c: (b * nc + c, 0, 0)),
        pl.BlockSpec((1, LANES), lambda b, h, c: (0, 0)),
        pl.BlockSpec((2 * nheads, 1), lambda b, h, c: (0, 0)),
        pl.BlockSpec((1, wv), lambda b, h, c: (0, h)),
        pl.BlockSpec((bb, hb, dk, dv), lambda b, h, c: (b, h, 0, 0)),
        pl.BlockSpec((bb, hb, 1, dk), lambda b, h, c: (b, h, 0, 0)),
        pl.BlockSpec((bb, hb, 1, 1), lambda b, h, c: (b, h, 0, 0)),
    ]
    state_bytes = bb * hb * dk * dv * 4
    vmem = (2 * (2 * rows * wq * 4 + 3 * rows * wv * 4 + 2 * state_bytes) + state_bytes
            + 24 * bb * L * max(L, dv) * 4)
    return pl.pallas_call(
        functools.partial(_mlstm_kernel, L=L, bb=bb, hb=hb, dk=dk, dv=dv, nc=nc, nheads=nheads),
        grid=(nb // bb, nhb, nc),
        in_specs=in_specs,
        out_specs=[pl.BlockSpec((rows, wv), lambda b, h, c: (b * nc + c, h)),
                   pl.BlockSpec((bb, hb, dk, dv), lambda b, h, c: (b, h, 0, 0)),
                   pl.BlockSpec((bb, hb, 1, dk), lambda b, h, c: (b, h, 0, 0)),
                   pl.BlockSpec((bb, hb, 1, 1), lambda b, h, c: (b, h, 0, 0))],
        out_shape=[jax.ShapeDtypeStruct((nb * t_len, nheads * dv), F32),
                   jax.ShapeDtypeStruct(c0.shape, F32),
                   jax.ShapeDtypeStruct(n0.shape, F32),
                   jax.ShapeDtypeStruct(m0.shape, F32)],
        scratch_shapes=[pltpu.VMEM((bb, hb, dk, dv), F32), pltpu.VMEM((bb, hb, 1, dk), F32),
                        pltpu.VMEM((bb, hb, 1, 1), F32)],
        compiler_params=_cparams(("arbitrary", "arbitrary", "arbitrary"), vmem),
        name="mlstm",
    )(z, z, z, z, gates, gates_t, bias_row, bias_col, norm_w, c0, n0, m0)


def _merge_kernel(dp_ref, ds_ref, wp_ref, ps_ref, ga_ref, gb_ref, mop_ref, mos_ref, o_ref, *, gin, gout, npt):
    def run(d_ref, mo_ref):
        for gg in range(wp_ref.shape[0]):
            y = jnp.dot(d_ref[:, gg * gin:(gg + 1) * gin].astype(BF16), wp_ref[gg], preferred_element_type=F32)
            cs = slice(gg * gout, (gg + 1) * gout)
            pool_out = y * ps_ref[:, cs]
            merged = jax.nn.sigmoid(ga_ref[:, cs]) * pool_out + jax.nn.sigmoid(gb_ref[:, cs]) * mo_ref[:, cs]
            o_ref[:, cs] = merged.astype(BF16)

    i = pl.program_id(0)
    pl.when(i < npt)(lambda: run(dp_ref, mop_ref))
    pl.when(i >= npt)(lambda: run(ds_ref, mos_ref))


def _merge(d_p, d_s, w_pool_b, pool_scale, z, mo_p, mo_s, ga_off, gb_off, tok, tiles):
    cdim = d_p.shape[1]
    ng, gin, gout = w_pool_b.shape
    dm = ng * gout
    half = dm // 2
    gh = ng // 2
    tm = tiles.merge_tm
    npt = tok.n_prompt // tm
    nst = tok.n_sample // tm

    def prompt_spec(width):
        return pl.BlockSpec((tm, width), lambda i, j: (jnp.minimum(i, npt - 1), jnp.where(i < npt, j, 1)))

    def sample_spec(width):
        return pl.BlockSpec((tm, width), lambda i, j: (jnp.clip(i - npt, 0, nst - 1), jnp.where(i < npt, 0, j)))

    vmem = 2 * (2 * tm * cdim * 2 + gh * gin * gout * 2 + 4 * tm * half * 4 + tm * half * 2) + 6 * tm * gout * 4
    return pl.pallas_call(
        functools.partial(_merge_kernel, gin=gin, gout=gout, npt=npt),
        grid=(tok.n // tm, 2),
        in_specs=[prompt_spec(cdim // 2), sample_spec(cdim // 2),
                  pl.BlockSpec((gh, gin, gout), lambda i, j: (j, 0, 0)),
                  pl.BlockSpec((1, half), lambda i, j: (0, j)),
                  pl.BlockSpec((tm, half), lambda i, j: (i, ga_off // half + j)),
                  pl.BlockSpec((tm, half), lambda i, j: (i, gb_off // half + j)),
                  prompt_spec(half), sample_spec(half)],
        out_specs=pl.BlockSpec((tm, half), lambda i, j: (i, j)),
        out_shape=jax.ShapeDtypeStruct((tok.n, dm), BF16),
        compiler_params=_cparams(("arbitrary", "arbitrary"), vmem),
        name="merge",
    )(d_p, d_s, w_pool_b, pool_scale.reshape(1, dm), z, z, mo_p, mo_s)


def _ffn_norm_kernel(xp_ref, xs_ref, acc_ref, g1_ref, sc_ref, sh_ref, nw_ref, wr_ref,
                     x1_ref, up_ref, re_ref, rw_ref, ub_ref, *, tok, tm):
    i = pl.program_id(0)
    d = x1_ref.shape[1]
    hd = d // 2

    def run(x_ref):
        def slab(s, carry):
            r = pl.multiple_of(s * BF16_ROWS, BF16_ROWS)
            rows = pl.ds(r, BF16_ROWS)
            row0 = i * tm + r
            x1 = x_ref[rows, :] + _mod_rows(g1_ref, row0, tok, d) * acc_ref[rows, :]
            x1_ref[rows, :] = x1
            u = _rms(x1, nw_ref[...]) * (1.0 + _mod_rows(sc_ref, row0, tok, d)) + _mod_rows(sh_ref, row0, tok, d)
            ub = u.astype(BF16)
            ub_ref[rows, :] = ub
            bits = lax.bitcast_convert_type(ub.astype(F32), U32)
            up_ref[rows, :] = (bits[:, hd:] & jnp.uint32(0xFFFF0000)) | (bits[:, :hd] >> 16)
            return carry
        lax.fori_loop(0, tm // BF16_ROWS, slab, 0)

    npt = tok.n_prompt // tm
    pl.when(i < npt)(lambda: run(xp_ref))
    pl.when(i >= npt)(lambda: run(xs_ref))

    lg = jnp.dot(ub_ref[...], wr_ref[...], preferred_element_type=F32)
    lane = lax.broadcasted_iota(I32, lg.shape, 1)
    lanef = lane.astype(F32)
    big = float(LANES)
    is_g = lane < N_GROUPS
    gl = jnp.where(is_g, lg, -jnp.inf)
    gmax = jnp.max(gl, axis=1, keepdims=True)
    grp = jnp.min(jnp.where(gl == gmax, lanef, big), axis=1, keepdims=True)
    p_grp = 1.0 / jnp.sum(jnp.where(is_g, jnp.exp(lg - gmax), 0.0), axis=1, keepdims=True)
    eidx = lane - N_GROUPS
    in_grp = (eidx >= 0) & (eidx < N_EXPERTS) & ((eidx >> EXP_SHIFT).astype(F32) == grp)
    el = jnp.where(in_grp, lg, -jnp.inf)
    v1 = jnp.max(el, axis=1, keepdims=True)
    i1 = jnp.min(jnp.where(el == v1, lanef, big), axis=1, keepdims=True)
    el2 = jnp.where(lanef == i1, -jnp.inf, el)
    v2 = jnp.max(el2, axis=1, keepdims=True)
    i2 = jnp.min(jnp.where(el2 == v2, lanef, big), axis=1, keepdims=True)
    e2 = jnp.exp(v2 - v1)
    w1 = (1.0 / (1.0 + e2)) * p_grp
    w2 = (e2 / (1.0 + e2)) * p_grp
    re_ref[...] = jnp.where(lane == 0, i1 - N_GROUPS, jnp.where(lane == 1, i2 - N_GROUPS, 0.0))
    rw_ref[...] = jnp.where(lane == 0, w1, jnp.where(lane == 1, w2, 0.0))


def _ffn_norm(xp, xs, acc, mod, norm_w, w_router, tok, tiles):
    d = xp.shape[1]
    g = mod.shape[0]
    tm = tiles.row_tm
    xp_spec, xs_spec = _dual_specs(tok, tm, d)
    vmem = (2 * (3 * tm * d * 4 + 3 * g * d * 4 + d * LANES * 2 + tm * d * 4 + tm * d * 2 + 2 * tm * LANES * 4)
            + tm * d * 2 + 16 * tm * LANES * 4)
    return pl.pallas_call(
        functools.partial(_ffn_norm_kernel, tok=tok, tm=tm),
        grid=(tok.n // tm,),
        in_specs=[xp_spec, xs_spec,
                  pl.BlockSpec((tm, d), lambda i: (i, 0)),
                  pl.BlockSpec((g, d), lambda i: (0, 2)),
                  pl.BlockSpec((g, d), lambda i: (0, 4)),
                  pl.BlockSpec((g, d), lambda i: (0, 3)),
                  pl.BlockSpec((1, d), lambda i: (0, 0)),
                  pl.BlockSpec((d, LANES), lambda i: (0, 0))],
        out_specs=[pl.BlockSpec((tm, d), lambda i: (i, 0)),
                   pl.BlockSpec((tm, d // 2), lambda i: (i, 0)),
                   pl.BlockSpec((tm, LANES), lambda i: (i, 0)),
                   pl.BlockSpec((tm, LANES), lambda i: (i, 0))],
        out_shape=[jax.ShapeDtypeStruct((tok.n, d), F32),
                   jax.ShapeDtypeStruct((tok.n, d // 2), U32),
                   jax.ShapeDtypeStruct((tok.n, LANES), F32),
                   jax.ShapeDtypeStruct((tok.n, LANES), F32)],
        scratch_shapes=[pltpu.VMEM((tm, d), BF16)],
        compiler_params=_cparams(("arbitrary",), vmem),
        name="ffn_norm",
    )(xp, xs, acc, mod, mod, mod, norm_w.reshape(1, d), w_router)


def _rank_kernel(re_ref, rank_ref, cnt_ref, carry_ref, *, tr):
    j = pl.program_id(0)
    i = pl.program_id(1)

    @pl.when((j == 0) & (i == 0))
    def _():
        carry_ref[...] = jnp.zeros_like(carry_ref)

    re = re_ref[...]
    lane = lax.broadcasted_iota(I32, re.shape, 1)
    e_col = jnp.sum(jnp.where(lane == j, re, 0.0), axis=1, keepdims=True)
    onehot = lane.astype(F32) == e_col
    rowi = lax.broadcasted_iota(I32, (tr, tr), 0)
    coli = lax.broadcasted_iota(I32, (tr, tr), 1)
    tri = jnp.where(rowi >= coli, 1.0, 0.0).astype(BF16)
    prefix = jnp.dot(tri, jnp.where(onehot, 1.0, 0.0).astype(BF16), preferred_element_type=F32)
    carry = carry_ref[0:1, :]
    rank = jnp.sum(jnp.where(onehot, prefix - 1.0 + carry, 0.0), axis=1, keepdims=True)
    rank_ref[...] = jnp.broadcast_to(rank, rank_ref.shape)
    new_carry = carry + prefix[tr - 1:tr, :]
    carry_ref[...] = jnp.broadcast_to(new_carry, carry_ref.shape)
    cnt_ref[...] = jnp.broadcast_to(new_carry, cnt_ref.shape)


def _rank(re, tiles):
    n = re.shape[0]
    tr = tiles.rank_tr
    nt = n // tr
    return pl.pallas_call(
        functools.partial(_rank_kernel, tr=tr),
        grid=(TOP_K, nt),
        in_specs=[pl.BlockSpec((tr, LANES), lambda j, i: (i, 0))],
        out_specs=[pl.BlockSpec((tr, LANES), lambda j, i: (j * nt + i, 0)),
                   pl.BlockSpec((SUBLANES, LANES), lambda j, i: (0, 0))],
        out_shape=[jax.ShapeDtypeStruct((TOP_K * n, LANES), F32),
                   jax.ShapeDtypeStruct((SUBLANES, LANES), F32)],
        scratch_shapes=[pltpu.VMEM((SUBLANES, LANES), F32)],
        compiler_params=_cparams(("arbitrary", "arbitrary"), 4 * MIB),
        name="rank",
    )(re)


def _invert_kernel(dest_ref, inv_ref, *, n_assign, n_rows):
    def put(a, carry):
        inv_ref[dest_ref[a]] = a // TOP_K
        return carry
    lax.fori_loop(0, n_assign, put, 0, unroll=8)


def _invert(dest_flat, n_rows):
    return pl.pallas_call(
        functools.partial(_invert_kernel, n_assign=dest_flat.shape[0], n_rows=n_rows),
        in_specs=[pl.BlockSpec(memory_space=pltpu.SMEM)],
        out_specs=pl.BlockSpec(memory_space=pltpu.SMEM),
        out_shape=jax.ShapeDtypeStruct((n_rows,), I32),
        name="invert",
    )(dest_flat)


def _gather_kernel(nused_ref, nvalid_ref, inv_ref, src_ref, o_ref, buf, sems, *, bm):
    b = pl.program_id(0)
    n_used = nused_ref[0]

    def row_copy(t, r, slot):
        return pltpu.make_async_copy(src_ref.at[pl.ds(t, 1)], buf.at[slot, pl.ds(r, 1)], sems.at[slot])

    def issue(blk, slot):
        def body(r, carry):
            row_copy(inv_ref[blk * bm + r], r, slot).start()
            return carry
        lax.fori_loop(0, nvalid_ref[blk], body, 0)

    @pl.when(b == 0)
    def _():
        buf[...] = jnp.zeros_like(buf)
        issue(0, 0)

    @pl.when(b + 1 < n_used)
    def _():
        issue(b + 1, (b + 1) % 2)

    @pl.when(b < n_used)
    def _():
        slot = b % 2
        nv = nvalid_ref[b]

        def drain(r, carry):
            row_copy(0, r, slot).wait()
            return carry
        lax.fori_loop(0, nv, drain, 0)
        rows = lax.broadcasted_iota(I32, o_ref.shape, 0)
        o_ref[...] = jnp.where(rows < nv, buf[slot], jnp.zeros_like(o_ref))


def _gather(inv, n_used, n_valid, src, tiles):
    n_rows = inv.shape[0]
    width = src.shape[1]
    bm = tiles.moe_bm
    return pl.pallas_call(
        functools.partial(_gather_kernel, bm=bm),
        grid_spec=pltpu.PrefetchScalarGridSpec(
            num_scalar_prefetch=3,
            grid=(n_rows // bm,),
            in_specs=[pl.BlockSpec(memory_space=pl.ANY)],
            out_specs=pl.BlockSpec((bm, width), lambda b, nu, nv, inv: (jnp.minimum(b, nu[0] - 1), 0)),
            scratch_shapes=[pltpu.VMEM((2, bm, width), src.dtype), pltpu.SemaphoreType.DMA((2,))]),
        out_shape=jax.ShapeDtypeStruct((n_rows, width), src.dtype),
        compiler_params=_cparams(("arbitrary",), 4 * bm * width * 4),
        name="gather",
    )(n_used, n_valid, inv, src)


def _unpack_pair(words):
    lo = lax.bitcast_convert_type(words << 16, F32).astype(BF16)
    hi = lax.bitcast_convert_type(words & jnp.uint32(0xFFFF0000), F32).astype(BF16)
    return lo, hi


def _expert_weights(w, plan_refs, w_hbm, stage, sems, cw):
    _, col_ref, exp_ref, first_ref, _, slot_ref, nxt_e_ref, nxt_c_ref = plan_refs

    def copies(e, c, slot):
        cols = pl.ds(pl.multiple_of(c * cw, cw), cw)
        return [pltpu.make_async_copy(w_hbm[k].at[e, :, cols], stage.at[slot, k], sems.at[slot, k])
                for k in range(len(w_hbm))]

    @pl.when(w == 0)
    def _():
        for cp in copies(exp_ref[0], col_ref[0], 0):
            cp.start()

    @pl.when(first_ref[w] == 1)
    def _():
        for cp in copies(exp_ref[w], col_ref[w], slot_ref[w]):
            cp.wait()

        @pl.when(nxt_e_ref[w] >= 0)
        def _():
            for cp in copies(nxt_e_ref[w], nxt_c_ref[w], 1 - slot_ref[w]):
                cp.start()


def _moe_up_kernel(*refs, cw):
    plan_refs = refs[:8]
    xs_ref, wg_hbm, wu_hbm, h_ref, stage, wgb, wub, sems = refs[8:]
    first_ref, valid_ref, slot_ref = plan_refs[3], plan_refs[4], plan_refs[5]
    w = pl.program_id(0)
    hd = xs_ref.shape[1]
    _expert_weights(w, plan_refs, (wg_hbm, wu_hbm), stage, sems, cw)

    @pl.when(first_ref[w] == 1)
    def _():
        wgb[...] = stage[slot_ref[w], 0].astype(BF16)
        wub[...] = stage[slot_ref[w], 1].astype(BF16)

    @pl.when(valid_ref[w] == 1)
    def _():
        lo, hi = _unpack_pair(xs_ref[...])
        g = (jnp.dot(lo, wgb[:hd, :], preferred_element_type=F32)
             + jnp.dot(hi, wgb[hd:, :], preferred_element_type=F32))
        u = (jnp.dot(lo, wub[:hd, :], preferred_element_type=F32)
             + jnp.dot(hi, wub[hd:, :], preferred_element_type=F32))
        h_ref[...] = ((g * jax.nn.sigmoid(g)) * u).astype(BF16)


def _moe_up(plan, xs, w_gate, w_up, tiles):
    p_rows, hd = xs.shape
    _, d, de = w_gate.shape
    bm, cw = tiles.moe_bm, tiles.moe_cw
    n_items = plan["up"][0].shape[0]
    vmem = 2 * (bm * hd * 4 + bm * cw * 2) + 4 * d * cw * 4 + 2 * d * cw * 2 + 8 * bm * cw * 4 + 2 * bm * d * 2

    def by_item(f):
        return lambda w, blk, col, ex, fi, va, sl, ne, nc: f(w, blk, col)

    return pl.pallas_call(
        functools.partial(_moe_up_kernel, cw=cw),
        grid_spec=pltpu.PrefetchScalarGridSpec(
            num_scalar_prefetch=8,
            grid=(n_items,),
            in_specs=[pl.BlockSpec((bm, hd), by_item(lambda w, blk, col: (blk[w], 0))),
                      pl.BlockSpec(memory_space=pl.ANY),
                      pl.BlockSpec(memory_space=pl.ANY)],
            out_specs=pl.BlockSpec((bm, cw), by_item(lambda w, blk, col: (blk[w], col[w]))),
            scratch_shapes=[pltpu.VMEM((2, 2, d, cw), F32), pltpu.VMEM((d, cw), BF16), pltpu.VMEM((d, cw), BF16),
                            pltpu.SemaphoreType.DMA((2, 2))]),
        out_shape=jax.ShapeDtypeStruct((p_rows, de), BF16),
        compiler_params=_cparams(("arbitrary",), vmem),
        name="moe_up",
    )(*plan["up"], xs, w_gate, w_up)


def _moe_down_kernel(*refs, dcw):
    plan_refs = refs[:8]
    h_ref, wd_hbm, y_ref, stage, wdb, sems = refs[8:]
    first_ref, valid_ref, slot_ref = plan_refs[3], plan_refs[4], plan_refs[5]
    w = pl.program_id(0)
    _expert_weights(w, plan_refs, (wd_hbm,), stage, sems, dcw)

    @pl.when(first_ref[w] == 1)
    def _():
        wdb[...] = stage[slot_ref[w], 0].astype(BF16)

    @pl.when(valid_ref[w] == 1)
    def _():
        y_ref[...] = jnp.dot(h_ref[...], wdb[...], preferred_element_type=F32)


def _moe_down(plan, hdn, w_down, tiles):
    p_rows, de = hdn.shape
    d = w_down.shape[2]
    bm, dcw = tiles.moe_bm, tiles.moe_dcw
    n_items = plan["down"][0].shape[0]
    vmem = 2 * (bm * de * 2 + bm * dcw * 4) + 2 * de * dcw * 4 + de * dcw * 2 + 2 * bm * dcw * 4

    def by_item(f):
        return lambda w, blk, col, ex, fi, va, sl, ne, nc: f(w, blk, col)

    return pl.pallas_call(
        functools.partial(_moe_down_kernel, dcw=dcw),
        grid_spec=pltpu.PrefetchScalarGridSpec(
            num_scalar_prefetch=8,
            grid=(n_items,),
            in_specs=[pl.BlockSpec((bm, de), by_item(lambda w, blk, col: (blk[w], 0))),
                      pl.BlockSpec(memory_space=pl.ANY)],
            out_specs=pl.BlockSpec((bm, dcw), by_item(lambda w, blk, col: (blk[w], col[w]))),
            scratch_shapes=[pltpu.VMEM((2, 1, de, dcw), F32), pltpu.VMEM((de, dcw), BF16),
                            pltpu.SemaphoreType.DMA((2, 1))]),
        out_shape=jax.ShapeDtypeStruct((p_rows, d), F32),
        compiler_params=_cparams(("arbitrary",), vmem),
        name="moe_down",
    )(*plan["down"], hdn, w_down)


def _work_items(nb_e, blk0_e, n_cols, n_items):
    items_e = nb_e * n_cols
    cum = jnp.cumsum(items_e)
    total = cum[-1]
    w = jnp.minimum(jnp.arange(n_items, dtype=I32), total - 1)
    own = (w[:, None] >= (cum - items_e)[None, :]) & (w[:, None] < cum[None, :])

    def pick(v):
        return jnp.sum(jnp.where(own, v[None, :], 0), axis=1)

    e = pick(jnp.arange(N_EXPERTS, dtype=I32))
    r = w - pick(cum - items_e)
    nb = jnp.maximum(pick(nb_e), 1)
    col = r // nb
    blk = pick(blk0_e) + r % nb
    valid = (jnp.arange(n_items, dtype=I32) < total).astype(I32)
    key = e * n_cols + col
    first = jnp.concatenate([jnp.ones((1,), I32), (key[1:] != key[:-1]).astype(I32)]) * valid
    slot = (jnp.cumsum(first) - 1) % 2
    idx = jnp.arange(n_items, dtype=I32)
    nxt = lax.cummin(jnp.where(first == 1, idx, n_items)[::-1])[::-1]
    nxt = jnp.concatenate([nxt[1:], jnp.full((1,), n_items, I32)])
    has_next = nxt < n_items
    pick_next = nxt[:, None] == idx[None, :]
    nxt_e = jnp.where(has_next, jnp.sum(jnp.where(pick_next, e[None, :], 0), axis=1), -1)
    nxt_c = jnp.where(has_next, jnp.sum(jnp.where(pick_next, col[None, :], 0), axis=1), 0)
    return (blk.astype(I32), col.astype(I32), e.astype(I32), first.astype(I32), valid, slot.astype(I32),
            nxt_e.astype(I32), nxt_c.astype(I32))


def _rows_per_block(counts, nb_e, blk0_e, n_blocks, bm):
    b = jnp.arange(n_blocks, dtype=I32)[:, None]
    own = (b >= blk0_e[None, :]) & (b < (blk0_e + nb_e)[None, :])
    rows = jnp.clip(counts[None, :] - (b - blk0_e[None, :]) * bm, 0, bm)
    return jnp.sum(jnp.where(own, rows, 0), axis=1).astype(I32)


def _dispatch_plan(re, rank_out, cnt_out, n, n_blocks, d_expert, d_model, tiles):
    bm = tiles.moe_bm
    eid = re[:, :TOP_K].astype(I32)
    rank = rank_out[:, 0].reshape(TOP_K, n).T.astype(I32)
    counts = cnt_out[0, :N_EXPERTS].astype(I32)
    nb_e = (counts + bm - 1) // bm
    blk0_e = jnp.cumsum(nb_e) - nb_e
    row0 = jnp.sum(jnp.where(eid[:, :, None] == jnp.arange(N_EXPERTS, dtype=I32), blk0_e * bm, 0), axis=2)
    dest = row0 + rank
    plan = {
        "dest": dest.reshape(-1).astype(I32),
        "n_used": jnp.sum(nb_e).reshape(1).astype(I32),
        "n_valid": _rows_per_block(counts, nb_e, blk0_e, n_blocks, bm),
        "up": _work_items(nb_e, blk0_e, d_expert // tiles.moe_cw, n_blocks * (d_expert // tiles.moe_cw)),
        "down": _work_items(nb_e, blk0_e, d_model // tiles.moe_dcw, n_blocks * (d_model // tiles.moe_dcw)),
    }
    return plan


def _combine_kernel(dest_ref, ys_ref, x1_ref, rw_ref, g2_ref, fw_ref, yp_ref, ysm_ref, ybuf, sems, *, tok, tm):
    i = pl.program_id(0)
    d = x1_ref.shape[1]
    base = i * tm
    nt = pl.num_programs(0)

    def issue(tile, slot):
        def body(r, carry):
            for j in range(TOP_K):
                pltpu.make_async_copy(ys_ref.at[pl.ds(dest_ref[(tile * tm + r) * TOP_K + j], 1)],
                                      ybuf.at[slot, j, pl.ds(r, 1)], sems.at[slot, j]).start()
            return carry
        lax.fori_loop(0, tm, body, 0, unroll=4)

    @pl.when(i == 0)
    def _():
        issue(0, 0)

    @pl.when(i + 1 < nt)
    def _():
        issue(i + 1, (i + 1) % 2)

    slot = i % 2
    for j in range(TOP_K):
        pltpu.make_async_copy(ys_ref.at[pl.ds(0, tm)], ybuf.at[slot, j], sems.at[slot, j]).wait()

    def run(out_ref):
        def slab(s, carry):
            r = pl.multiple_of(s * SUBLANES, SUBLANES)
            rows = pl.ds(r, SUBLANES)
            wts = rw_ref[rows, :]
            ff = wts[:, 0:1] * ybuf[slot, 0, rows, :] + wts[:, 1:2] * ybuf[slot, 1, rows, :]
            g2 = jnp.broadcast_to(g2_ref[pl.ds(_batch_of(base + r, tok), 1), :], (SUBLANES, d))
            out_ref[rows, :] = _rms(x1_ref[rows, :] + g2 * ff, fw_ref[...])
            return carry
        lax.fori_loop(0, tm // SUBLANES, slab, 0)

    npt = tok.n_prompt // tm
    pl.when(i < npt)(lambda: run(yp_ref))
    pl.when(i >= npt)(lambda: run(ysm_ref))


def _combine(dest_flat, ys, x1, rw, mod, final_w, tok, tiles):
    d = x1.shape[1]
    g = mod.shape[0]
    tm = tiles.comb_tm
    npt = tok.n_prompt // tm
    nst = tok.n_sample // tm
    vmem = 2 * (tm * d * 4 + tm * LANES * 4 + g * d * 4 + 2 * tm * d * 4) + 2 * TOP_K * tm * d * 4
    return pl.pallas_call(
        functools.partial(_combine_kernel, tok=tok, tm=tm),
        grid_spec=pltpu.PrefetchScalarGridSpec(
            num_scalar_prefetch=1,
            grid=(tok.n // tm,),
            in_specs=[pl.BlockSpec(memory_space=pl.ANY),
                      pl.BlockSpec((tm, d), lambda i, dest: (i, 0)),
                      pl.BlockSpec((tm, LANES), lambda i, dest: (i, 0)),
                      pl.BlockSpec((g, d), lambda i, dest: (0, 5)),
                      pl.BlockSpec((1, d), lambda i, dest: (0, 0))],
            out_specs=[pl.BlockSpec((tm, d), lambda i, dest: (jnp.minimum(i, npt - 1), 0)),
                       pl.BlockSpec((tm, d), lambda i, dest: (jnp.clip(i - npt, 0, nst - 1), 0))],
            scratch_shapes=[pltpu.VMEM((2, TOP_K, tm, d), F32), pltpu.SemaphoreType.DMA((2, TOP_K))]),
        out_shape=[jax.ShapeDtypeStruct((tok.n_prompt, d), F32),
                   jax.ShapeDtypeStruct((tok.n_sample, d), F32)],
        compiler_params=_cparams(("arbitrary",), vmem),
        name="combine",
    )(dest_flat, ys, x1, rw, mod, final_w.reshape(1, d))


def _gate_layouts(gates, n_rows, row_off, L, n_gate):
    g = gates[row_off:row_off + n_rows, :n_gate]
    return g.reshape(n_rows // L, L, n_gate).transpose(0, 2, 1)


def _layer(xp, xs, c_all, pool_s, c_s, n_s, m_s, lw, tok, tiles):
    (w_ada, b_ada, norm_mix_w, w_in, b_igate, b_fgate, w_pool, pool_scale, mlstm_norm_w, w_out, norm_ffn_w,
     w_router_group, w_router_expert, w_exp_gate, w_exp_up, w_exp_down) = lw
    d = xp.shape[1]
    nheads, dk, dv = c_s.shape[1:]
    pool_in = pool_s.shape[2]
    mqk, mv = nheads * dk, nheads * dv
    offs = (pool_in, pool_in + mqk, pool_in + 2 * mqk, pool_in + 2 * mqk + mv)
    ga_off = offs[3] + mv
    gb_off = ga_off + d
    n_main = gb_off + d
    n_gate = 2 * nheads
    n = tok.n

    mod = _ada(c_all, w_ada, b_ada, tiles)
    w_in_t = w_in.T
    u, gates = _prenorm(xp, xs, mod, norm_mix_w, w_in_t, n_main, n_gate, tok, tiles)
    z = _matmul(u, w_in_t, n_main, True, tiles)

    zeros_pool = jnp.zeros((tok.nb_p, POOL_BUF, pool_in), F32)
    d_p, pool_new_p = _pool(z, zeros_pool, 0, tok.nb_p, tok.rows_p, 1, tiles.pool_tt, 0)
    d_s, pool_new_s = _pool(z, pool_s, tok.n_prompt, tok.nb_s, tok.rows_s, tiles.pool_bt, tok.rows_s, PAST_LEN)

    bias = jnp.concatenate([b_igate, b_fgate])
    bias_row = jnp.pad(bias, (0, LANES - n_gate)).reshape(1, LANES)
    bias_col = bias.reshape(n_gate, 1)
    nw = mlstm_norm_w.reshape(1, mv)
    lp = tiles.mlstm_l
    zc = jnp.zeros((tok.nb_p, nheads, dk, dv), F32)
    zn = jnp.zeros((tok.nb_p, nheads, 1, dk), F32)
    zm = jnp.zeros((tok.nb_p, nheads, 1, 1), F32)
    mo_p, c_p, n_p, m_p = _mlstm(z, gates, _gate_layouts(gates, tok.n_prompt, 0, lp, n_gate), bias_row, bias_col,
                                 nw, zc, zn, zm, 0, tok.nb_p, tok.rows_p, lp, 1, tiles.mlstm_hb_p, offs)
    ls = tok.rows_s
    mo_s, c_n, n_n, m_n = _mlstm(z, gates, _gate_layouts(gates, tok.n_sample, tok.n_prompt, ls, n_gate), bias_row,
                                 bias_col, nw, c_s, n_s.reshape(tok.nb_s, nheads, 1, dk),
                                 m_s.reshape(tok.nb_s, nheads, 1, 1), tok.n_prompt, tok.nb_s, ls, ls,
                                 tiles.mlstm_bb_s, tiles.mlstm_hb_s, offs)

    merged = _merge(d_p, d_s, w_pool.astype(BF16), pool_scale, z, mo_p, mo_s, ga_off, gb_off, tok, tiles)
    acc = _matmul(merged, w_out, d, False, tiles)

    w_router = jnp.pad(jnp.concatenate([w_router_group, w_router_expert], axis=1),
                       ((0, 0), (0, LANES - N_GROUPS - N_EXPERTS))).astype(BF16)
    x1, u2p, re, rw = _ffn_norm(xp, xs, acc, mod, norm_ffn_w, w_router, tok, tiles)
    rank_out, cnt_out = _rank(re, tiles)
    d_expert = w_exp_gate.shape[2]
    n_blocks = -(-(n * TOP_K) // tiles.moe_bm) + N_EXPERTS
    plan = _dispatch_plan(re, rank_out, cnt_out, n, n_blocks, d_expert, d, tiles)
    inv = _invert(plan["dest"], n_blocks * tiles.moe_bm)
    xsort = _gather(inv, plan["n_used"], plan["n_valid"], u2p, tiles)
    hdn = _moe_up(plan, xsort, w_exp_gate, w_exp_up, tiles)
    ysort = _moe_down(plan, hdn, w_exp_down, tiles)
    states = (pool_new_p, c_p, n_p.reshape(tok.nb_p, nheads, dk), m_p.reshape(tok.nb_p, nheads),
              pool_new_s, c_n, n_n.reshape(tok.nb_s, nheads, dk), m_n.reshape(tok.nb_s, nheads))
    return (plan["dest"], ysort, x1, rw, mod), states


def _forward(x_prompt, x_sample, c_prompt, c_sample, state_pool, state_mlstm_C, state_mlstm_n, state_mlstm_m,
             w_ada, b_ada, norm_mix_w, w_in, b_igate, b_fgate, w_pool, pool_scale, mlstm_norm_w, w_out,
             norm_ffn_w, w_router_group, w_router_expert, w_exp_gate, w_exp_up, w_exp_down, final_norm_w,
             tiles=Tiles()):
    nb_p, rows_p, d = x_prompt.shape
    nb_s, rows_s, _ = x_sample.shape
    depth = w_ada.shape[0]
    assert depth == 1, "the merged-token pipeline is written for a single layer"
    tok = Tok(nb_p * rows_p, nb_s * rows_s, rows_p, rows_s, nb_p, nb_s)
    for tm in (tiles.row_tm, tiles.mm_tm, tiles.merge_tm, tiles.rank_tr, tiles.comb_tm):
        assert tok.n_prompt % tm == 0 and tok.n_sample % tm == 0
    xp = x_prompt.reshape(tok.n_prompt, d)
    xs = x_sample.reshape(tok.n_sample, d)
    g = nb_p + nb_s
    g_pad = -(-g // SUBLANES) * SUBLANES
    c_all = jnp.pad(jnp.concatenate([c_prompt, c_sample], axis=0), ((0, g_pad - g), (0, 0)))
    lw = (w_ada[0], b_ada[0], norm_mix_w[0], w_in[0], b_igate[0], b_fgate[0], w_pool[0], pool_scale[0],
          mlstm_norm_w[0], w_out[0], norm_ffn_w[0], w_router_group[0], w_router_expert[0], w_exp_gate[0],
          w_exp_up[0], w_exp_down[0])
    (dest, ysort, x1, rw, mod), st = _layer(xp, xs, c_all, state_pool[0], state_mlstm_C[0], state_mlstm_n[0],
                                            state_mlstm_m[0], lw, tok, tiles)
    y_p, y_s = _combine(dest, ysort, x1, rw, mod, final_norm_w, tok, tiles)
    return (y_p.reshape(x_prompt.shape), y_s.reshape(x_sample.shape)) + tuple(s[None] for s in st)


def kernel(x_prompt, x_sample, c_prompt, c_sample, state_pool, state_mlstm_C, state_mlstm_n, state_mlstm_m,
           w_ada, b_ada, norm_mix_w, w_in, b_igate, b_fgate, w_pool, pool_scale, mlstm_norm_w, w_out,
           norm_ffn_w, w_router_group, w_router_expert, w_exp_gate, w_exp_up, w_exp_down, final_norm_w):
    return _forward(x_prompt, x_sample, c_prompt, c_sample, state_pool, state_mlstm_C, state_mlstm_n,
                    state_mlstm_m, w_ada, b_ada, norm_mix_w, w_in, b_igate, b_fgate, w_pool, pool_scale,
                    mlstm_norm_w, w_out, norm_ffn_w, w_router_group, w_router_expert, w_exp_gate, w_exp_up,
                    w_exp_down, final_norm_w)
```

```python
import functools
from typing import NamedTuple

import jax
import jax.numpy as jnp
from jax import lax
from jax.experimental import pallas as pl
from jax.experimental.pallas import tpu as pltpu

F32, BF16, I32, U32 = jnp.float32, jnp.bfloat16, jnp.int32, jnp.uint32

EPS = 1e-6
PAST_LEN = 16384
POOL_WINDOWS = (2, 4, 8, 16)
POOL_BUF = 15
N_GROUPS = 8
EXP_PER_GROUP = 8
N_EXPERTS = N_GROUPS * EXP_PER_GROUP
EXP_SHIFT = EXP_PER_GROUP.bit_length() - 1
TOP_K = 2

V7X_VMEM_BYTES = 64 * 2**20
LANES = 128
SUBLANES = 8
BF16_ROWS = 16
MIB = 2**20


class Tok(NamedTuple):
    n_prompt: int
    n_sample: int
    rows_p: int
    rows_s: int
    nb_p: int
    nb_s: int

    @property
    def n(self):
        return self.n_prompt + self.n_sample


class Tiles(NamedTuple):
    ada_tn: int = 512
    row_tm: int = 256
    mm_tm: int = 1024
    mm_tn: int = 1024
    pool_tt: int = 256
    pool_bt: int = 16
    mlstm_l: int = 256
    mlstm_hb_p: int = 2
    mlstm_hb_s: int = 4
    mlstm_bb_s: int = 4
    merge_tm: int = 512
    rank_tr: int = 256
    moe_bm: int = 256
    moe_cw: int = 512
    moe_dcw: int = 4096
    comb_tm: int = 256


def _cparams(sem, vmem_bytes):
    limit = int(min(max(vmem_bytes * 5 // 4 + 2 * MIB, 16 * MIB), V7X_VMEM_BYTES - 6 * MIB))
    return pltpu.CompilerParams(dimension_semantics=sem, vmem_limit_bytes=limit)


def _batch_of(row, tok):
    return jnp.where(row < tok.n_prompt, row // tok.rows_p,
                     tok.nb_p + (row - tok.n_prompt) // tok.rows_s)


def _mod_rows(ref, row0, tok, d):
    top = jnp.broadcast_to(ref[pl.ds(_batch_of(row0, tok), 1), :], (SUBLANES, d))
    bot = jnp.broadcast_to(ref[pl.ds(_batch_of(row0 + SUBLANES, tok), 1), :], (SUBLANES, d))
    return jnp.concatenate([top, bot], axis=0)


def _rms(x, w):
    r = lax.rsqrt(jnp.mean(x * x, axis=-1, keepdims=True) + EPS)
    return (x * r) * w


def _log_sigmoid(x):
    return jnp.minimum(x, 0.0) - jnp.log1p(jnp.exp(-jnp.abs(x)))


def _ada_kernel(c_ref, w_ref, b_ref, o_ref):
    c = c_ref[...]
    a = (c * jax.nn.sigmoid(c)).astype(BF16)
    o_ref[...] = jnp.dot(a, w_ref[...].astype(BF16), preferred_element_type=F32) + b_ref[...]


def _ada(c_all, w_ada, b_ada, tiles):
    g, d = c_all.shape
    n = w_ada.shape[1]
    tn = tiles.ada_tn
    vmem = 2 * (g * d * 4 + d * tn * 4 + g * tn * 4) + d * tn * 2
    return pl.pallas_call(
        _ada_kernel,
        grid=(n // tn,),
        in_specs=[pl.BlockSpec((g, d), lambda j: (0, 0)),
                  pl.BlockSpec((d, tn), lambda j: (0, j)),
                  pl.BlockSpec((1, tn), lambda j: (0, j))],
        out_specs=pl.BlockSpec((g, tn), lambda j: (0, j)),
        out_shape=jax.ShapeDtypeStruct((g, n), F32),
        compiler_params=_cparams(("arbitrary",), vmem),
        name="ada",
    )(c_all, w_ada, b_ada.reshape(1, n))


def _dual_specs(tok, tm, d):
    npt = tok.n_prompt // tm
    nst = tok.n_sample // tm
    return (pl.BlockSpec((tm, d), lambda i: (jnp.minimum(i, npt - 1), 0)),
            pl.BlockSpec((tm, d), lambda i: (jnp.clip(i - npt, 0, nst - 1), 0)))


def _prenorm_kernel(xp_ref, xs_ref, sc_ref, sh_ref, nw_ref, wg_ref, u_ref, gates_ref, wgb_ref, *, tok, tm, n_gate):
    i = pl.program_id(0)
    d = u_ref.shape[1]

    @pl.when(i == 0)
    def _():
        wgb_ref[0:n_gate, :] = wg_ref[...].astype(BF16)
        wgb_ref[n_gate:, :] = jnp.zeros((LANES - n_gate, d), BF16)

    def run(x_ref):
        def slab(s, carry):
            r = pl.multiple_of(s * BF16_ROWS, BF16_ROWS)
            row0 = i * tm + r
            xn = _rms(x_ref[pl.ds(r, BF16_ROWS), :], nw_ref[...])
            u = xn * (1.0 + _mod_rows(sc_ref, row0, tok, d)) + _mod_rows(sh_ref, row0, tok, d)
            u_ref[pl.ds(r, BF16_ROWS), :] = u.astype(BF16)
            return carry
        lax.fori_loop(0, tm // BF16_ROWS, slab, 0)

    npt = tok.n_prompt // tm
    pl.when(i < npt)(lambda: run(xp_ref))
    pl.when(i >= npt)(lambda: run(xs_ref))
    gates_ref[...] = lax.dot_general(u_ref[...], wgb_ref[...], (((1,), (1,)), ((), ())),
                                     preferred_element_type=F32)


def _prenorm(xp, xs, mod, norm_w, w_in_t, n_main, n_gate, tok, tiles):
    d = xp.shape[1]
    g = mod.shape[0]
    tm = tiles.row_tm
    assert n_main % n_gate == 0 and n_gate % BF16_ROWS == 0 and n_gate <= LANES
    xp_spec, xs_spec = _dual_specs(tok, tm, d)
    vmem = 2 * (2 * tm * d * 4 + 2 * g * d * 4 + n_gate * d * 4 + tm * d * 2 + tm * LANES * 4) + d * LANES * 2
    return pl.pallas_call(
        functools.partial(_prenorm_kernel, tok=tok, tm=tm, n_gate=n_gate),
        grid=(tok.n // tm,),
        in_specs=[xp_spec, xs_spec,
                  pl.BlockSpec((g, d), lambda i: (0, 1)),
                  pl.BlockSpec((g, d), lambda i: (0, 0)),
                  pl.BlockSpec((1, d), lambda i: (0, 0)),
                  pl.BlockSpec((n_gate, d), lambda i: (n_main // n_gate, 0))],
        out_specs=[pl.BlockSpec((tm, d), lambda i: (i, 0)),
                   pl.BlockSpec((tm, LANES), lambda i: (i, 0))],
        out_shape=[jax.ShapeDtypeStruct((tok.n, d), BF16),
                   jax.ShapeDtypeStruct((tok.n, LANES), F32)],
        scratch_shapes=[pltpu.VMEM((LANES, d), BF16)],
        compiler_params=_cparams(("arbitrary",), vmem),
        name="prenorm",
    )(xp, xs, mod, mod, norm_w.reshape(1, d), w_in_t)


def _matmul_kernel(a_ref, w_hbm, o_ref, stage, wb_ref, sem, *, w_is_nk, tn):
    j = pl.program_id(0)
    i = pl.program_id(1)

    def tile_copy(jj):
        cols = pl.ds(pl.multiple_of(jj * tn, tn), tn)
        src = w_hbm.at[cols, :] if w_is_nk else w_hbm.at[:, cols]
        return pltpu.make_async_copy(src, stage, sem.at[0])

    @pl.when((j == 0) & (i == 0))
    def _():
        tile_copy(0).start()

    @pl.when(i == 0)
    def _():
        tile_copy(j).wait()
        wb_ref[...] = stage[...].astype(BF16)

        @pl.when(j + 1 < pl.num_programs(0))
        def _():
            tile_copy(j + 1).start()

    contract_w = 1 if w_is_nk else 0
    o_ref[...] = lax.dot_general(a_ref[...], wb_ref[...], (((1,), (contract_w,)), ((), ())),
                                 preferred_element_type=F32)


def _matmul(a, w, n_cols, w_is_nk, tiles):
    m, k = a.shape
    tm, tn = tiles.mm_tm, tiles.mm_tn
    w_block = (tn, k) if w_is_nk else (k, tn)
    vmem = 2 * (tm * k * 2 + tm * tn * 4) + k * tn * 4 + k * tn * 2
    return pl.pallas_call(
        functools.partial(_matmul_kernel, w_is_nk=w_is_nk, tn=tn),
        grid=(n_cols // tn, m // tm),
        in_specs=[pl.BlockSpec((tm, k), lambda j, i: (i, 0)),
                  pl.BlockSpec(memory_space=pl.ANY)],
        out_specs=pl.BlockSpec((tm, tn), lambda j, i: (i, j)),
        out_shape=jax.ShapeDtypeStruct((m, n_cols), F32),
        scratch_shapes=[pltpu.VMEM(w_block, F32), pltpu.VMEM(w_block, BF16), pltpu.SemaphoreType.DMA((1,))],
        compiler_params=_cparams(("arbitrary", "arbitrary"), vmem),
        name="matmul",
    )(a, w)


def _pool_kernel(p_ref, st_ref, d_ref, new_ref, ext_ref, *, bt, tt, gin, pos0, nt):
    t = pl.program_id(1)
    cdim = ext_ref.shape[2]
    halo = POOL_BUF + 1

    @pl.when(t == 0)
    def _():
        ext_ref[:, 1:halo, :] = st_ref[...]

    @pl.when(t > 0)
    def _():
        ext_ref[:, 0:halo, :] = ext_ref[:, tt:tt + halo, :]

    ext_ref[:, halo:halo + tt, :] = p_ref[...].reshape(bt, tt, cdim)
    pos = pos0 + t * tt + lax.broadcasted_iota(I32, (bt, tt, gin), 1)
    for g, w in enumerate(POOL_WINDOWS):
        cs = slice(g * gin, (g + 1) * gin)
        cur = ext_ref[:, halo:halo + tt, cs]
        acc = cur
        for j in range(1, w):
            acc = acc + ext_ref[:, halo - j:halo - j + tt, cs]
        cnt = jnp.minimum(pos + 1, w).astype(F32)
        d_ref[:, cs] = (acc / cnt - cur).reshape(bt * tt, gin)

    @pl.when(t == nt - 1)
    def _():
        new_ref[...] = ext_ref[:, tt + 1:tt + halo, :]


def _pool(z, state, row_off, nb, t_len, bt, tt, pos0):
    cdim = state.shape[2]
    gin = cdim // len(POOL_WINDOWS)
    nt = t_len // tt
    rows = bt * tt
    rb0 = row_off // rows
    vmem = 2 * (2 * rows * cdim * 4 + 2 * bt * 16 * cdim * 4) + bt * (tt + 16) * cdim * 4 + 8 * rows * gin * 4
    return pl.pallas_call(
        functools.partial(_pool_kernel, bt=bt, tt=tt, gin=gin, pos0=pos0, nt=nt),
        grid=(nb // bt, nt),
        in_specs=[pl.BlockSpec((rows, cdim), lambda b, t: (rb0 + b * nt + t, 0)),
                  pl.BlockSpec((bt, POOL_BUF, cdim), lambda b, t: (b, 0, 0))],
        out_specs=[pl.BlockSpec((rows, cdim), lambda b, t: (b * nt + t, 0)),
                   pl.BlockSpec((bt, POOL_BUF, cdim), lambda b, t: (b, 0, 0))],
        out_shape=[jax.ShapeDtypeStruct((nb * t_len, cdim), F32),
                   jax.ShapeDtypeStruct((nb, POOL_BUF, cdim), F32)],
        scratch_shapes=[pltpu.VMEM((bt, tt + POOL_BUF + 1, cdim), F32)],
        compiler_params=_cparams(("arbitrary", "arbitrary"), vmem),
        name="pool",
    )(z, state)


def _mlstm_kernel(q_ref, k_ref, v_ref, o_ref, gc_ref, gr_ref, brow_ref, bcol_ref, nw_ref, c0_ref, n0_ref, m0_ref,
                  mo_ref, c_out, n_out, m_out, c_s, n_s, m_s, *, L, bb, hb, dk, dv, nc, nheads):
    hblk = pl.program_id(1)
    c = pl.program_id(2)

    @pl.when(c == 0)
    def _():
        c_s[...] = c0_ref[...]
        n_s[...] = n0_ref[...]
        m_s[...] = m0_ref[...]

    lane = lax.broadcasted_iota(I32, (L, LANES), 1)
    sub = lax.broadcasted_iota(I32, (2 * nheads, L), 0)
    rowi = lax.broadcasted_iota(I32, (L, L), 0)
    coli = lax.broadcasted_iota(I32, (L, L), 1)
    tri = rowi >= coli
    scale = dk ** -0.5

    units = [(bi, hh) for bi in range(bb) for hh in range(hb)]
    heads = range(len(units))
    rsl = [slice(bi * L, (bi + 1) * L) for bi, _ in units]
    gcb = [gc_ref[rsl[i], :] + brow_ref[...] for i in heads]
    grb = [gr_ref[units[i][0]] + bcol_ref[...] for i in heads]
    hs_dyn = [hblk * hb + hh for _, hh in units]
    li_col = [jnp.sum(jnp.where(lane == hs_dyn[i], gcb[i], 0.0), axis=1, keepdims=True) for i in heads]
    lf_col = [jnp.sum(jnp.where(lane == hs_dyn[i] + nheads, gcb[i], 0.0), axis=1, keepdims=True) for i in heads]
    li_row = [jnp.sum(jnp.where(sub == hs_dyn[i], grb[i], 0.0), axis=0, keepdims=True) for i in heads]
    lf_row = [jnp.sum(jnp.where(sub == hs_dyn[i] + nheads, grb[i], 0.0), axis=0, keepdims=True) for i in heads]
    lf_col = [_log_sigmoid(x) for x in lf_col]
    lf_row = [_log_sigmoid(x) for x in lf_row]
    b_col = [jnp.sum(jnp.where(tri, x, 0.0), axis=1, keepdims=True) for x in lf_row]
    b_row = [jnp.sum(jnp.where(rowi <= coli, x, 0.0), axis=0, keepdims=True) for x in lf_col]

    q = [q_ref[rsl[i], hh * dk:(hh + 1) * dk] * scale for i, (_, hh) in enumerate(units)]
    k = [k_ref[rsl[i], hh * dk:(hh + 1) * dk] for i, (_, hh) in enumerate(units)]
    qb = [x.astype(BF16) for x in q]
    kb = [x.astype(BF16) for x in k]
    vb = [v_ref[rsl[i], hh * dv:(hh + 1) * dv].astype(BF16) for i, (_, hh) in enumerate(units)]
    cmat = [c_s[bi, hh] for bi, hh in units]
    nvec = [n_s[bi, hh] for bi, hh in units]
    m_prev = [m_s[bi, hh] for bi, hh in units]

    dmat = [jnp.where(tri, b_col[i] - b_row[i] + li_row[i], -jnp.inf) for i in heads]
    inter = [b_col[i] + m_prev[i] for i in heads]
    m_t = [jnp.maximum(inter[i], jnp.max(dmat[i], axis=1, keepdims=True)) for i in heads]
    dw = [jnp.exp(dmat[i] - m_t[i]) for i in heads]
    iw = [jnp.exp(inter[i] - m_t[i]) for i in heads]
    qk = [lax.dot_general(qb[i], kb[i], (((1,), (1,)), ((), ())), preferred_element_type=F32) for i in heads]
    qc = [jnp.dot(qb[i], cmat[i].astype(BF16), preferred_element_type=F32) for i in heads]
    s = [qk[i] * dw[i] for i in heads]
    num = [jnp.dot(s[i].astype(BF16), vb[i], preferred_element_type=F32) + iw[i] * qc[i] for i in heads]
    den = [jnp.sum(s[i], axis=1, keepdims=True) + iw[i] * jnp.sum(q[i] * nvec[i], axis=1, keepdims=True)
           for i in heads]
    hval = [num[i] / jnp.maximum(jnp.abs(den[i]), jnp.exp(-m_t[i])) for i in heads]
    for i, (_, hh) in enumerate(units):
        hs = slice(hh * dv, (hh + 1) * dv)
        mo_ref[rsl[i], hs] = jax.nn.sigmoid(o_ref[rsl[i], hs]) * _rms(hval[i], nw_ref[:, hs])

    b_last = [x[L - 1:L, :] for x in b_col]
    dl_col = [b_last[i] - b_col[i] + li_col[i] for i in heads]
    dl_row = [b_last[i] - b_row[i] + li_row[i] for i in heads]
    m_new = [jnp.maximum(b_last[i] + m_prev[i], jnp.max(dl_row[i], axis=1, keepdims=True)) for i in heads]
    dec = [jnp.exp(b_last[i] + m_prev[i] - m_new[i]) for i in heads]
    kk = [jnp.exp(dl_col[i] - m_new[i]) * k[i] for i in heads]
    upd = [lax.dot_general(kk[i].astype(BF16), vb[i], (((0,), (0,)), ((), ())), preferred_element_type=F32)
           for i in heads]
    for i, (bi, hh) in enumerate(units):
        c_s[bi, hh] = dec[i] * cmat[i] + upd[i]
        n_s[bi, hh] = dec[i] * nvec[i] + jnp.sum(kk[i], axis=0, keepdims=True)
        m_s[bi, hh] = m_new[i]

    @pl.when(c == nc - 1)
    def _():
        c_out[...] = c_s[...]
        n_out[...] = n_s[...]
        m_out[...] = m_s[...]


def _mlstm(z, gates, gates_t, bias_row, bias_col, norm_w, c0, n0, m0, row_off, nb, t_len, L, bb, hb, offs):
    _, nheads, dk, dv = c0.shape
    nc = t_len // L
    assert bb == 1 or nc == 1
    rows = bb * L
    rb0 = row_off // rows
    nhb = nheads // hb
    q_off, k_off, v_off, o_off = offs
    wq, wv = hb * dk, hb * dv

    def rowblk(b, c):
        return rb0 + b * nc + c

    in_specs = [
        pl.BlockSpec((rows, wq), lambda b, h, c: (rowblk(b, c), q_off // wq + h)),
        pl.BlockSpec((rows, wq), lambda b, h, c: (rowblk(b, c), k_off // wq + h)),
        pl.BlockSpec((rows, wv), lambda b, h, c: (rowblk(b, c), v_off // wv + h)),
        pl.BlockSpec((rows, wv), lambda b, h, c: (rowblk(b, c), o_off // wv + h)),
        pl.BlockSpec((rows, LANES), lambda b, h, c: (rowblk(b, c), 0)),
        pl.BlockSpec((bb, 2 * nheads, L), lambda b, h, c: (b * nc + c, 0, 0)),
        pl.BlockSpec((1, LANES), lambda b, h, c: (0, 0)),
        pl.BlockSpec((2 * nheads, 1), lambda b, h, c: (0, 0)),
        pl.BlockSpec((1, wv), lambda b, h, c: (0, h)),
        pl.BlockSpec((bb, hb, dk, dv), lambda b, h, c: (b, h, 0, 0)),
        pl.BlockSpec((bb, hb, 1, dk), lambda b, h, c: (b, h, 0, 0)),
        pl.BlockSpec((bb, hb, 1, 1), lambda b, h, c: (b, h, 0, 0)),
    ]
    state_bytes = bb * hb * dk * dv * 4
    vmem = (2 * (2 * rows * wq * 4 + 3 * rows * wv * 4 + 2 * state_bytes) + state_bytes
            + 24 * bb * L * max(L, dv) * 4)
    return pl.pallas_call(
        functools.partial(_mlstm_kernel, L=L, bb=bb, hb=hb, dk=dk, dv=dv, nc=nc, nheads=nheads),
        grid=(nb // bb, nhb, nc),
        in_specs=in_specs,
        out_specs=[pl.BlockSpec((rows, wv), lambda b, h, c: (b * nc + c, h)),
                   pl.BlockSpec((bb, hb, dk, dv), lambda b, h, c: (b, h, 0, 0)),
                   pl.BlockSpec((bb, hb, 1, dk), lambda b, h, c: (b, h, 0, 0)),
                   pl.BlockSpec((bb, hb, 1, 1), lambda b, h, c: (b, h, 0, 0))],
        out_shape=[jax.ShapeDtypeStruct((nb * t_len, nheads * dv), F32),
                   jax.ShapeDtypeStruct(c0.shape, F32),
                   jax.ShapeDtypeStruct(n0.shape, F32),
                   jax.ShapeDtypeStruct(m0.shape, F32)],
        scratch_shapes=[pltpu.VMEM((bb, hb, dk, dv), F32), pltpu.VMEM((bb, hb, 1, dk), F32),
                        pltpu.VMEM((bb, hb, 1, 1), F32)],
        compiler_params=_cparams(("arbitrary", "arbitrary", "arbitrary"), vmem),
        name="mlstm",
    )(z, z, z, z, gates, gates_t, bias_row, bias_col, norm_w, c0, n0, m0)


def _merge_kernel(dp_ref, ds_ref, wp_ref, ps_ref, ga_ref, gb_ref, mop_ref, mos_ref, o_ref, *, gin, gout, npt):
    def run(d_ref, mo_ref):
        for gg in range(wp_ref.shape[0]):
            y = jnp.dot(d_ref[:, gg * gin:(gg + 1) * gin].astype(BF16), wp_ref[gg], preferred_element_type=F32)
            cs = slice(gg * gout, (gg + 1) * gout)
            pool_out = y * ps_ref[:, cs]
            merged = jax.nn.sigmoid(ga_ref[:, cs]) * pool_out + jax.nn.sigmoid(gb_ref[:, cs]) * mo_ref[:, cs]
            o_ref[:, cs] = merged.astype(BF16)

    i = pl.program_id(0)
    pl.when(i < npt)(lambda: run(dp_ref, mop_ref))
    pl.when(i >= npt)(lambda: run(ds_ref, mos_ref))


def _merge(d_p, d_s, w_pool_b, pool_scale, z, mo_p, mo_s, ga_off, gb_off, tok, tiles):
    cdim = d_p.shape[1]
    ng, gin, gout = w_pool_b.shape
    dm = ng * gout
    half = dm // 2
    gh = ng // 2
    tm = tiles.merge_tm
    npt = tok.n_prompt // tm
    nst = tok.n_sample // tm

    def prompt_spec(width):
        return pl.BlockSpec((tm, width), lambda i, j: (jnp.minimum(i, npt - 1), jnp.where(i < npt, j, 1)))

    def sample_spec(width):
        return pl.BlockSpec((tm, width), lambda i, j: (jnp.clip(i - npt, 0, nst - 1), jnp.where(i < npt, 0, j)))

    vmem = 2 * (2 * tm * cdim * 2 + gh * gin * gout * 2 + 4 * tm * half * 4 + tm * half * 2) + 6 * tm * gout * 4
    return pl.pallas_call(
        functools.partial(_merge_kernel, gin=gin, gout=gout, npt=npt),
        grid=(tok.n // tm, 2),
        in_specs=[prompt_spec(cdim // 2), sample_spec(cdim // 2),
                  pl.BlockSpec((gh, gin, gout), lambda i, j: (j, 0, 0)),
                  pl.BlockSpec((1, half), lambda i, j: (0, j)),
                  pl.BlockSpec((tm, half), lambda i, j: (i, ga_off // half + j)),
                  pl.BlockSpec((tm, half), lambda i, j: (i, gb_off // half + j)),
                  prompt_spec(half), sample_spec(half)],
        out_specs=pl.BlockSpec((tm, half), lambda i, j: (i, j)),
        out_shape=jax.ShapeDtypeStruct((tok.n, dm), BF16),
        compiler_params=_cparams(("arbitrary", "arbitrary"), vmem),
        name="merge",
    )(d_p, d_s, w_pool_b, pool_scale.reshape(1, dm), z, z, mo_p, mo_s)


def _ffn_norm_kernel(xp_ref, xs_ref, acc_ref, g1_ref, sc_ref, sh_ref, nw_ref, wr_ref,
                     x1_ref, up_ref, re_ref, rw_ref, ub_ref, *, tok, tm):
    i = pl.program_id(0)
    d = x1_ref.shape[1]
    hd = d // 2

    def run(x_ref):
        def slab(s, carry):
            r = pl.multiple_of(s * BF16_ROWS, BF16_ROWS)
            rows = pl.ds(r, BF16_ROWS)
            row0 = i * tm + r
            x1 = x_ref[rows, :] + _mod_rows(g1_ref, row0, tok, d) * acc_ref[rows, :]
            x1_ref[rows, :] = x1
            u = _rms(x1, nw_ref[...]) * (1.0 + _mod_rows(sc_ref, row0, tok, d)) + _mod_rows(sh_ref, row0, tok, d)
            ub = u.astype(BF16)
            ub_ref[rows, :] = ub
            bits = lax.bitcast_convert_type(ub.astype(F32), U32)
            up_ref[rows, :] = (bits[:, hd:] & jnp.uint32(0xFFFF0000)) | (bits[:, :hd] >> 16)
            return carry
        lax.fori_loop(0, tm // BF16_ROWS, slab, 0)

    npt = tok.n_prompt // tm
    pl.when(i < npt)(lambda: run(xp_ref))
    pl.when(i >= npt)(lambda: run(xs_ref))

    lg = jnp.dot(ub_ref[...], wr_ref[...], preferred_element_type=F32)
    lane = lax.broadcasted_iota(I32, lg.shape, 1)
    lanef = lane.astype(F32)
    big = float(LANES)
    is_g = lane < N_GROUPS
    gl = jnp.where(is_g, lg, -jnp.inf)
    gmax = jnp.max(gl, axis=1, keepdims=True)
    grp = jnp.min(jnp.where(gl == gmax, lanef, big), axis=1, keepdims=True)
    p_grp = 1.0 / jnp.sum(jnp.where(is_g, jnp.exp(lg - gmax), 0.0), axis=1, keepdims=True)
    eidx = lane - N_GROUPS
    in_grp = (eidx >= 0) & (eidx < N_EXPERTS) & ((eidx >> EXP_SHIFT).astype(F32) == grp)
    el = jnp.where(in_grp, lg, -jnp.inf)
    v1 = jnp.max(el, axis=1, keepdims=True)
    i1 = jnp.min(jnp.where(el == v1, lanef, big), axis=1, keepdims=True)
    el2 = jnp.where(lanef == i1, -jnp.inf, el)
    v2 = jnp.max(el2, axis=1, keepdims=True)
    i2 = jnp.min(jnp.where(el2 == v2, lanef, big), axis=1, keepdims=True)
    e2 = jnp.exp(v2 - v1)
    w1 = (1.0 / (1.0 + e2)) * p_grp
    w2 = (e2 / (1.0 + e2)) * p_grp
    re_ref[...] = jnp.where(lane == 0, i1 - N_GROUPS, jnp.where(lane == 1, i2 - N_GROUPS, 0.0))
    rw_ref[...] = jnp.where(lane == 0, w1, jnp.where(lane == 1, w2, 0.0))


def _ffn_norm(xp, xs, acc, mod, norm_w, w_router, tok, tiles):
    d = xp.shape[1]
    g = mod.shape[0]
    tm = tiles.row_tm
    xp_spec, xs_spec = _dual_specs(tok, tm, d)
    vmem = (2 * (3 * tm * d * 4 + 3 * g * d * 4 + d * LANES * 2 + tm * d * 4 + tm * d * 2 + 2 * tm * LANES * 4)
            + tm * d * 2 + 16 * tm * LANES * 4)
    return pl.pallas_call(
        functools.partial(_ffn_norm_kernel, tok=tok, tm=tm),
        grid=(tok.n // tm,),
        in_specs=[xp_spec, xs_spec,
                  pl.BlockSpec((tm, d), lambda i: (i, 0)),
                  pl.BlockSpec((g, d), lambda i: (0, 2)),
                  pl.BlockSpec((g, d), lambda i: (0, 4)),
                  pl.BlockSpec((g, d), lambda i: (0, 3)),
                  pl.BlockSpec((1, d), lambda i: (0, 0)),
                  pl.BlockSpec((d, LANES), lambda i: (0, 0))],
        out_specs=[pl.BlockSpec((tm, d), lambda i: (i, 0)),
                   pl.BlockSpec((tm, d // 2), lambda i: (i, 0)),
                   pl.BlockSpec((tm, LANES), lambda i: (i, 0)),
                   pl.BlockSpec((tm, LANES), lambda i: (i, 0))],
        out_shape=[jax.ShapeDtypeStruct((tok.n, d), F32),
                   jax.ShapeDtypeStruct((tok.n, d // 2), U32),
                   jax.ShapeDtypeStruct((tok.n, LANES), F32),
                   jax.ShapeDtypeStruct((tok.n, LANES), F32)],
        scratch_shapes=[pltpu.VMEM((tm, d), BF16)],
        compiler_params=_cparams(("arbitrary",), vmem),
        name="ffn_norm",
    )(xp, xs, acc, mod, mod, mod, norm_w.reshape(1, d), w_router)


def _rank_kernel(re_ref, rank_ref, cnt_ref, carry_ref, *, tr):
    j = pl.program_id(0)
    i = pl.program_id(1)

    @pl.when((j == 0) & (i == 0))
    def _():
        carry_ref[...] = jnp.zeros_like(carry_ref)

    re = re_ref[...]
    lane = lax.broadcasted_iota(I32, re.shape, 1)
    e_col = jnp.sum(jnp.where(lane == j, re, 0.0), axis=1, keepdims=True)
    onehot = lane.astype(F32) == e_col
    rowi = lax.broadcasted_iota(I32, (tr, tr), 0)
    coli = lax.broadcasted_iota(I32, (tr, tr), 1)
    tri = jnp.where(rowi >= coli, 1.0, 0.0).astype(BF16)
    prefix = jnp.dot(tri, jnp.where(onehot, 1.0, 0.0).astype(BF16), preferred_element_type=F32)
    carry = carry_ref[0:1, :]
    rank = jnp.sum(jnp.where(onehot, prefix - 1.0 + carry, 0.0), axis=1, keepdims=True)
    rank_ref[...] = jnp.broadcast_to(rank, rank_ref.shape)
    new_carry = carry + prefix[tr - 1:tr, :]
    carry_ref[...] = jnp.broadcast_to(new_carry, carry_ref.shape)
    cnt_ref[...] = jnp.broadcast_to(new_carry, cnt_ref.shape)


def _rank(re, tiles):
    n = re.shape[0]
    tr = tiles.rank_tr
    nt = n // tr
    return pl.pallas_call(
        functools.partial(_rank_kernel, tr=tr),
        grid=(TOP_K, nt),
        in_specs=[pl.BlockSpec((tr, LANES), lambda j, i: (i, 0))],
        out_specs=[pl.BlockSpec((tr, LANES), lambda j, i: (j * nt + i, 0)),
                   pl.BlockSpec((SUBLANES, LANES), lambda j, i: (0, 0))],
        out_shape=[jax.ShapeDtypeStruct((TOP_K * n, LANES), F32),
                   jax.ShapeDtypeStruct((SUBLANES, LANES), F32)],
        scratch_shapes=[pltpu.VMEM((SUBLANES, LANES), F32)],
        compiler_params=_cparams(("arbitrary", "arbitrary"), 4 * MIB),
        name="rank",
    )(re)


def _invert_kernel(dest_ref, inv_ref, *, n_assign, n_rows):
    def put(a, carry):
        inv_ref[dest_ref[a]] = a // TOP_K
        return carry
    lax.fori_loop(0, n_assign, put, 0, unroll=8)


def _invert(dest_flat, n_rows):
    return pl.pallas_call(
        functools.partial(_invert_kernel, n_assign=dest_flat.shape[0], n_rows=n_rows),
        in_specs=[pl.BlockSpec(memory_space=pltpu.SMEM)],
        out_specs=pl.BlockSpec(memory_space=pltpu.SMEM),
        out_shape=jax.ShapeDtypeStruct((n_rows,), I32),
        name="invert",
    )(dest_flat)


def _gather_kernel(nused_ref, nvalid_ref, inv_ref, src_ref, o_ref, buf, sems, *, bm):
    b = pl.program_id(0)
    n_used = nused_ref[0]

    def row_copy(t, r, slot):
        return pltpu.make_async_copy(src_ref.at[pl.ds(t, 1)], buf.at[slot, pl.ds(r, 1)], sems.at[slot])

    def issue(blk, slot):
        def body(r, carry):
            row_copy(inv_ref[blk * bm + r], r, slot).start()
            return carry
        lax.fori_loop(0, nvalid_ref[blk], body, 0)

    @pl.when(b == 0)
    def _():
        buf[...] = jnp.zeros_like(buf)
        issue(0, 0)

    @pl.when(b + 1 < n_used)
    def _():
        issue(b + 1, (b + 1) % 2)

    @pl.when(b < n_used)
    def _():
        slot = b % 2
        nv = nvalid_ref[b]

        def drain(r, carry):
            row_copy(0, r, slot).wait()
            return carry
        lax.fori_loop(0, nv, drain, 0)
        rows = lax.broadcasted_iota(I32, o_ref.shape, 0)
        o_ref[...] = jnp.where(rows < nv, buf[slot], jnp.zeros_like(o_ref))


def _gather(inv, n_used, n_valid, src, tiles):
    n_rows = inv.shape[0]
    width = src.shape[1]
    bm = tiles.moe_bm
    return pl.pallas_call(
        functools.partial(_gather_kernel, bm=bm),
        grid_spec=pltpu.PrefetchScalarGridSpec(
            num_scalar_prefetch=3,
            grid=(n_rows // bm,),
            in_specs=[pl.BlockSpec(memory_space=pl.ANY)],
            out_specs=pl.BlockSpec((bm, width), lambda b, nu, nv, inv: (jnp.minimum(b, nu[0] - 1), 0)),
            scratch_shapes=[pltpu.VMEM((2, bm, width), src.dtype), pltpu.SemaphoreType.DMA((2,))]),
        out_shape=jax.ShapeDtypeStruct((n_rows, width), src.dtype),
        compiler_params=_cparams(("arbitrary",), 4 * bm * width * 4),
        name="gather",
    )(n_used, n_valid, inv, src)


def _unpack_pair(words):
    lo = lax.bitcast_convert_type(words << 16, F32).astype(BF16)
    hi = lax.bitcast_convert_type(words & jnp.uint32(0xFFFF0000), F32).astype(BF16)
    return lo, hi


def _expert_weights(w, plan_refs, w_hbm, stage, sems, cw):
    _, col_ref, exp_ref, first_ref, _, slot_ref, nxt_e_ref, nxt_c_ref = plan_refs

    def copies(e, c, slot):
        cols = pl.ds(pl.multiple_of(c * cw, cw), cw)
        return [pltpu.make_async_copy(w_hbm[k].at[e, :, cols], stage.at[slot, k], sems.at[slot, k])
                for k in range(len(w_hbm))]

    @pl.when(w == 0)
    def _():
        for cp in copies(exp_ref[0], col_ref[0], 0):
            cp.start()

    @pl.when(first_ref[w] == 1)
    def _():
        for cp in copies(exp_ref[w], col_ref[w], slot_ref[w]):
            cp.wait()

        @pl.when(nxt_e_ref[w] >= 0)
        def _():
            for cp in copies(nxt_e_ref[w], nxt_c_ref[w], 1 - slot_ref[w]):
                cp.start()


def _moe_up_kernel(*refs, cw):
    plan_refs = refs[:8]
    xs_ref, wg_hbm, wu_hbm, h_ref, stage, wgb, wub, sems = refs[8:]
    first_ref, valid_ref, slot_ref = plan_refs[3], plan_refs[4], plan_refs[5]
    w = pl.program_id(0)
    hd = xs_ref.shape[1]
    _expert_weights(w, plan_refs, (wg_hbm, wu_hbm), stage, sems, cw)

    @pl.when(first_ref[w] == 1)
    def _():
        wgb[...] = stage[slot_ref[w], 0].astype(BF16)
        wub[...] = stage[slot_ref[w], 1].astype(BF16)

    @pl.when(valid_ref[w] == 1)
    def _():
        lo, hi = _unpack_pair(xs_ref[...])
        g = (jnp.dot(lo, wgb[:hd, :], preferred_element_type=F32)
             + jnp.dot(hi, wgb[hd:, :], preferred_element_type=F32))
        u = (jnp.dot(lo, wub[:hd, :], preferred_element_type=F32)
             + jnp.dot(hi, wub[hd:, :], preferred_element_type=F32))
        h_ref[...] = ((g * jax.nn.sigmoid(g)) * u).astype(BF16)


def _moe_up(plan, xs, w_gate, w_up, tiles):
    p_rows, hd = xs.shape
    _, d, de = w_gate.shape
    bm, cw = tiles.moe_bm, tiles.moe_cw
    n_items = plan["up"][0].shape[0]
    vmem = 2 * (bm * hd * 4 + bm * cw * 2) + 4 * d * cw * 4 + 2 * d * cw * 2 + 8 * bm * cw * 4 + 2 * bm * d * 2

    def by_item(f):
        return lambda w, blk, col, ex, fi, va, sl, ne, nc: f(w, blk, col)

    return pl.pallas_call(
        functools.partial(_moe_up_kernel, cw=cw),
        grid_spec=pltpu.PrefetchScalarGridSpec(
            num_scalar_prefetch=8,
            grid=(n_items,),
            in_specs=[pl.BlockSpec((bm, hd), by_item(lambda w, blk, col: (blk[w], 0))),
                      pl.BlockSpec(memory_space=pl.ANY),
                      pl.BlockSpec(memory_space=pl.ANY)],
            out_specs=pl.BlockSpec((bm, cw), by_item(lambda w, blk, col: (blk[w], col[w]))),
            scratch_shapes=[pltpu.VMEM((2, 2, d, cw), F32), pltpu.VMEM((d, cw), BF16), pltpu.VMEM((d, cw), BF16),
                            pltpu.SemaphoreType.DMA((2, 2))]),
        out_shape=jax.ShapeDtypeStruct((p_rows, de), BF16),
        compiler_params=_cparams(("arbitrary",), vmem),
        name="moe_up",
    )(*plan["up"], xs, w_gate, w_up)


def _moe_down_kernel(*refs, dcw):
    plan_refs = refs[:8]
    h_ref, wd_hbm, y_ref, stage, wdb, sems = refs[8:]
    first_ref, valid_ref, slot_ref = plan_refs[3], plan_refs[4], plan_refs[5]
    w = pl.program_id(0)
    _expert_weights(w, plan_refs, (wd_hbm,), stage, sems, dcw)

    @pl.when(first_ref[w] == 1)
    def _():
        wdb[...] = stage[slot_ref[w], 0].astype(BF16)

    @pl.when(valid_ref[w] == 1)
    def _():
        y_ref[...] = jnp.dot(h_ref[...], wdb[...], preferred_element_type=F32)


def _moe_down(plan, hdn, w_down, tiles):
    p_rows, de = hdn.shape
    d = w_down.shape[2]
    bm, dcw = tiles.moe_bm, tiles.moe_dcw
    n_items = plan["down"][0].shape[0]
    vmem = 2 * (bm * de * 2 + bm * dcw * 4) + 2 * de * dcw * 4 + de * dcw * 2 + 2 * bm * dcw * 4

    def by_item(f):
        return lambda w, blk, col, ex, fi, va, sl, ne, nc: f(w, blk, col)

    return pl.pallas_call(
        functools.partial(_moe_down_kernel, dcw=dcw),
        grid_spec=pltpu.PrefetchScalarGridSpec(
            num_scalar_prefetch=8,
            grid=(n_items,),
            in_specs=[pl.BlockSpec((bm, de), by_item(lambda w, blk, col: (blk[w], 0))),
                      pl.BlockSpec(memory_space=pl.ANY)],
            out_specs=pl.BlockSpec((bm, dcw), by_item(lambda w, blk, col: (blk[w], col[w]))),
            scratch_shapes=[pltpu.VMEM((2, 1, de, dcw), F32), pltpu.VMEM((de, dcw), BF16),
                            pltpu.SemaphoreType.DMA((2, 1))]),
        out_shape=jax.ShapeDtypeStruct((p_rows, d), F32),
        compiler_params=_cparams(("arbitrary",), vmem),
        name="moe_down",
    )(*plan["down"], hdn, w_down)


def _work_items(nb_e, blk0_e, n_cols, n_items):
    items_e = nb_e * n_cols
    cum = jnp.cumsum(items_e)
    total = cum[-1]
    w = jnp.minimum(jnp.arange(n_items, dtype=I32), total - 1)
    own = (w[:, None] >= (cum - items_e)[None, :]) & (w[:, None] < cum[None, :])

    def pick(v):
        return jnp.sum(jnp.where(own, v[None, :], 0), axis=1)

    e = pick(jnp.arange(N_EXPERTS, dtype=I32))
    r = w - pick(cum - items_e)
    nb = jnp.maximum(pick(nb_e), 1)
    col = r // nb
    blk = pick(blk0_e) + r % nb
    valid = (jnp.arange(n_items, dtype=I32) < total).astype(I32)
    key = e * n_cols + col
    first = jnp.concatenate([jnp.ones((1,), I32), (key[1:] != key[:-1]).astype(I32)]) * valid
    slot = (jnp.cumsum(first) - 1) % 2
    idx = jnp.arange(n_items, dtype=I32)
    nxt = lax.cummin(jnp.where(first == 1, idx, n_items)[::-1])[::-1]
    nxt = jnp.concatenate([nxt[1:], jnp.full((1,), n_items, I32)])
    has_next = nxt < n_items
    pick_next = nxt[:, None] == idx[None, :]
    nxt_e = jnp.where(has_next, jnp.sum(jnp.where(pick_next, e[None, :], 0), axis=1), -1)
    nxt_c = jnp.where(has_next, jnp.sum(jnp.where(pick_next, col[None, :], 0), axis=1), 0)
    return (blk.astype(I32), col.astype(I32), e.astype(I32), first.astype(I32), valid, slot.astype(I32),
            nxt_e.astype(I32), nxt_c.astype(I32))


def _rows_per_block(counts, nb_e, blk0_e, n_blocks, bm):
    b = jnp.arange(n_blocks, dtype=I32)[:, None]
    own = (b >= blk0_e[None, :]) & (b < (blk0_e + nb_e)[None, :])
    rows = jnp.clip(counts[None, :] - (b - blk0_e[None, :]) * bm, 0, bm)
    return jnp.sum(jnp.where(own, rows, 0), axis=1).astype(I32)


def _dispatch_plan(re, rank_out, cnt_out, n, n_blocks, d_expert, d_model, tiles):
    bm = tiles.moe_bm
    eid = re[:, :TOP_K].astype(I32)
    rank = rank_out[:, 0].reshape(TOP_K, n).T.astype(I32)
    counts = cnt_out[0, :N_EXPERTS].astype(I32)
    nb_e = (counts + bm - 1) // bm
    blk0_e = jnp.cumsum(nb_e) - nb_e
    row0 = jnp.sum(jnp.where(eid[:, :, None] == jnp.arange(N_EXPERTS, dtype=I32), blk0_e * bm, 0), axis=2)
    dest = row0 + rank
    plan = {
        "dest": dest.reshape(-1).astype(I32),
        "n_used": jnp.sum(nb_e).reshape(1).astype(I32),
        "n_valid": _rows_per_block(counts, nb_e, blk0_e, n_blocks, bm),
        "up": _work_items(nb_e, blk0_e, d_expert // tiles.moe_cw, n_blocks * (d_expert // tiles.moe_cw)),
        "down": _work_items(nb_e, blk0_e, d_model // tiles.moe_dcw, n_blocks * (d_model // tiles.moe_dcw)),
    }
    return plan


def _combine_kernel(dest_ref, ys_ref, x1_ref, rw_ref, g2_ref, fw_ref, yp_ref, ysm_ref, ybuf, sems, *, tok, tm):
    i = pl.program_id(0)
    d = x1_ref.shape[1]
    base = i * tm
    nt = pl.num_programs(0)

    def issue(tile, slot):
        def body(r, carry):
            for j in range(TOP_K):
                pltpu.make_async_copy(ys_ref.at[pl.ds(dest_ref[(tile * tm + r) * TOP_K + j], 1)],
                                      ybuf.at[slot, j, pl.ds(r, 1)], sems.at[slot, j]).start()
            return carry
        lax.fori_loop(0, tm, body, 0, unroll=4)

    @pl.when(i == 0)
    def _():
        issue(0, 0)

    @pl.when(i + 1 < nt)
    def _():
        issue(i + 1, (i + 1) % 2)

    slot = i % 2
    for j in range(TOP_K):
        pltpu.make_async_copy(ys_ref.at[pl.ds(0, tm)], ybuf.at[slot, j], sems.at[slot, j]).wait()

    def run(out_ref):
        def slab(s, carry):
            r = pl.multiple_of(s * SUBLANES, SUBLANES)
            rows = pl.ds(r, SUBLANES)
            wts = rw_ref[rows, :]
            ff = wts[:, 0:1] * ybuf[slot, 0, rows, :] + wts[:, 1:2] * ybuf[slot, 1, rows, :]
            g2 = jnp.broadcast_to(g2_ref[pl.ds(_batch_of(base + r, tok), 1), :], (SUBLANES, d))
            out_ref[rows, :] = _rms(x1_ref[rows, :] + g2 * ff, fw_ref[...])
            return carry
        lax.fori_loop(0, tm // SUBLANES, slab, 0)

    npt = tok.n_prompt // tm
    pl.when(i < npt)(lambda: run(yp_ref))
    pl.when(i >= npt)(lambda: run(ysm_ref))


def _combine(dest_flat, ys, x1, rw, mod, final_w, tok, tiles):
    d = x1.shape[1]
    g = mod.shape[0]
    tm = tiles.comb_tm
    npt = tok.n_prompt // tm
    nst = tok.n_sample // tm
    vmem = 2 * (tm * d * 4 + tm * LANES * 4 + g * d * 4 + 2 * tm * d * 4) + 2 * TOP_K * tm * d * 4
    return pl.pallas_call(
        functools.partial(_combine_kernel, tok=tok, tm=tm),
        grid_spec=pltpu.PrefetchScalarGridSpec(
            num_scalar_prefetch=1,
            grid=(tok.n // tm,),
            in_specs=[pl.BlockSpec(memory_space=pl.ANY),
                      pl.BlockSpec((tm, d), lambda i, dest: (i, 0)),
                      pl.BlockSpec((tm, LANES), lambda i, dest: (i, 0)),
                      pl.BlockSpec((g, d), lambda i, dest: (0, 5)),
                      pl.BlockSpec((1, d), lambda i, dest: (0, 0))],
            out_specs=[pl.BlockSpec((tm, d), lambda i, dest: (jnp.minimum(i, npt - 1), 0)),
                       pl.BlockSpec((tm, d), lambda i, dest: (jnp.clip(i - npt, 0, nst - 1), 0))],
            scratch_shapes=[pltpu.VMEM((2, TOP_K, tm, d), F32), pltpu.SemaphoreType.DMA((2, TOP_K))]),
        out_shape=[jax.ShapeDtypeStruct((tok.n_prompt, d), F32),
                   jax.ShapeDtypeStruct((tok.n_sample, d), F32)],
        compiler_params=_cparams(("arbitrary",), vmem),
        name="combine",
    )(dest_flat, ys, x1, rw, mod, final_w.reshape(1, d))


def _gate_layouts(gates, n_rows, row_off, L, n_gate):
    g = gates[row_off:row_off + n_rows, :n_gate]
    return g.reshape(n_rows // L, L, n_gate).transpose(0, 2, 1)


def _layer(xp, xs, c_all, pool_s, c_s, n_s, m_s, lw, tok, tiles):
    (w_ada, b_ada, norm_mix_w, w_in, b_igate, b_fgate, w_pool, pool_scale, mlstm_norm_w, w_out, norm_ffn_w,
     w_router_group, w_router_expert, w_exp_gate, w_exp_up, w_exp_down) = lw
    d = xp.shape[1]
    nheads, dk, dv = c_s.shape[1:]
    pool_in = pool_s.shape[2]
    mqk, mv = nheads * dk, nheads * dv
    offs = (pool_in, pool_in + mqk, pool_in + 2 * mqk, pool_in + 2 * mqk + mv)
    ga_off = offs[3] + mv
    gb_off = ga_off + d
    n_main = gb_off + d
    n_gate = 2 * nheads
    n = tok.n

    mod = _ada(c_all, w_ada, b_ada, tiles)
    w_in_t = w_in.T
    u, gates = _prenorm(xp, xs, mod, norm_mix_w, w_in_t, n_main, n_gate, tok, tiles)
    z = _matmul(u, w_in_t, n_main, True, tiles)

    zeros_pool = jnp.zeros((tok.nb_p, POOL_BUF, pool_in), F32)
    d_p, pool_new_p = _pool(z, zeros_pool, 0, tok.nb_p, tok.rows_p, 1, tiles.pool_tt, 0)
    d_s, pool_new_s = _pool(z, pool_s, tok.n_prompt, tok.nb_s, tok.rows_s, tiles.pool_bt, tok.rows_s, PAST_LEN)

    bias = jnp.concatenate([b_igate, b_fgate])
    bias_row = jnp.pad(bias, (0, LANES - n_gate)).reshape(1, LANES)
    bias_col = bias.reshape(n_gate, 1)
    nw = mlstm_norm_w.reshape(1, mv)
    lp = tiles.mlstm_l
    zc = jnp.zeros((tok.nb_p, nheads, dk, dv), F32)
    zn = jnp.zeros((tok.nb_p, nheads, 1, dk), F32)
    zm = jnp.zeros((tok.nb_p, nheads, 1, 1), F32)
    mo_p, c_p, n_p, m_p = _mlstm(z, gates, _gate_layouts(gates, tok.n_prompt, 0, lp, n_gate), bias_row, bias_col,
                                 nw, zc, zn, zm, 0, tok.nb_p, tok.rows_p, lp, 1, tiles.mlstm_hb_p, offs)
    ls = tok.rows_s
    mo_s, c_n, n_n, m_n = _mlstm(z, gates, _gate_layouts(gates, tok.n_sample, tok.n_prompt, ls, n_gate), bias_row,
                                 bias_col, nw, c_s, n_s.reshape(tok.nb_s, nheads, 1, dk),
                                 m_s.reshape(tok.nb_s, nheads, 1, 1), tok.n_prompt, tok.nb_s, ls, ls,
                                 tiles.mlstm_bb_s, tiles.mlstm_hb_s, offs)

    merged = _merge(d_p, d_s, w_pool.astype(BF16), pool_scale, z, mo_p, mo_s, ga_off, gb_off, tok, tiles)
    acc = _matmul(merged, w_out, d, False, tiles)

    w_router = jnp.pad(jnp.concatenate([w_router_group, w_router_expert], axis=1),
                       ((0, 0), (0, LANES - N_GROUPS - N_EXPERTS))).astype(BF16)
    x1, u2p, re, rw = _ffn_norm(xp, xs, acc, mod, norm_ffn_w, w_router, tok, tiles)
    rank_out, cnt_out = _rank(re, tiles)
    d_expert = w_exp_gate.shape[2]
    n_blocks = -(-(n * TOP_K) // tiles.moe_bm) + N_EXPERTS
    plan = _dispatch_plan(re, rank_out, cnt_out, n, n_blocks, d_expert, d, tiles)
    inv = _invert(plan["dest"], n_blocks * tiles.moe_bm)
    xsort = _gather(inv, plan["n_used"], plan["n_valid"], u2p, tiles)
    hdn = _moe_up(plan, xsort, w_exp_gate, w_exp_up, tiles)
    ysort = _moe_down(plan, hdn, w_exp_down, tiles)
    states = (pool_new_p, c_p, n_p.reshape(tok.nb_p, nheads, dk), m_p.reshape(tok.nb_p, nheads),
              pool_new_s, c_n, n_n.reshape(tok.nb_s, nheads, dk), m_n.reshape(tok.nb_s, nheads))
    return (plan["dest"], ysort, x1, rw, mod), states


def _forward(x_prompt, x_sample, c_prompt, c_sample, state_pool, state_mlstm_C, state_mlstm_n, state_mlstm_m,
             w_ada, b_ada, norm_mix_w, w_in, b_igate, b_fgate, w_pool, pool_scale, mlstm_norm_w, w_out,
             norm_ffn_w, w_router_group, w_router_expert, w_exp_gate, w_exp_up, w_exp_down, final_norm_w,
             tiles=Tiles()):
    nb_p, rows_p, d = x_prompt.shape
    nb_s, rows_s, _ = x_sample.shape
    depth = w_ada.shape[0]
    assert depth == 1, "the merged-token pipeline is written for a single layer"
    tok = Tok(nb_p * rows_p, nb_s * rows_s, rows_p, rows_s, nb_p, nb_s)
    for tm in (tiles.row_tm, tiles.mm_tm, tiles.merge_tm, tiles.rank_tr, tiles.comb_tm):
        assert tok.n_prompt % tm == 0 and tok.n_sample % tm == 0
    xp = x_prompt.reshape(tok.n_prompt, d)
    xs = x_sample.reshape(tok.n_sample, d)
    g = nb_p + nb_s
    g_pad = -(-g // SUBLANES) * SUBLANES
    c_all = jnp.pad(jnp.concatenate([c_prompt, c_sample], axis=0), ((0, g_pad - g), (0, 0)))
    lw = (w_ada[0], b_ada[0], norm_mix_w[0], w_in[0], b_igate[0], b_fgate[0], w_pool[0], pool_scale[0],
          mlstm_norm_w[0], w_out[0], norm_ffn_w[0], w_router_group[0], w_router_expert[0], w_exp_gate[0],
          w_exp_up[0], w_exp_down[0])
    (dest, ysort, x1, rw, mod), st = _layer(xp, xs, c_all, state_pool[0], state_mlstm_C[0], state_mlstm_n[0],
                                            state_mlstm_m[0], lw, tok, tiles)
    y_p, y_s = _combine(dest, ysort, x1, rw, mod, final_norm_w, tok, tiles)
    return (y_p.reshape(x_prompt.shape), y_s.reshape(x_sample.shape)) + tuple(s[None] for s in st)


def kernel(x_prompt, x_sample, c_prompt, c_sample, state_pool, state_mlstm_C, state_mlstm_n, state_mlstm_m,
           w_ada, b_ada, norm_mix_w, w_in, b_igate, b_fgate, w_pool, pool_scale, mlstm_norm_w, w_out,
           norm_ffn_w, w_router_group, w_router_expert, w_exp_gate, w_exp_up, w_exp_down, final_norm_w):
    return _forward(x_prompt, x_sample, c_prompt, c_sample, state_pool, state_mlstm_C, state_mlstm_n,
                    state_mlstm_m, w_ada, b_ada, norm_mix_w, w_in, b_igate, b_fgate, w_pool, pool_scale,
                    mlstm_norm_w, w_out, norm_ffn_w, w_router_group, w_router_expert, w_exp_gate, w_exp_up,
                    w_exp_down, final_norm_w)
```

```python
import functools
from typing import NamedTuple

import jax
import jax.numpy as jnp
from jax import lax
from jax.experimental import pallas as pl
from jax.experimental.pallas import tpu as pltpu

F32, BF16, I32, U32 = jnp.float32, jnp.bfloat16, jnp.int32, jnp.uint32

EPS = 1e-6
PAST_LEN = 16384
POOL_WINDOWS = (2, 4, 8, 16)
POOL_BUF = 15
N_GROUPS = 8
EXP_PER_GROUP = 8
N_EXPERTS = N_GROUPS * EXP_PER_GROUP
EXP_SHIFT = EXP_PER_GROUP.bit_length() - 1
TOP_K = 2

V7X_VMEM_BYTES = 64 * 2**20
LANES = 128
SUBLANES = 8
BF16_ROWS = 16
MIB = 2**20


class Tok(NamedTuple):
    n_prompt: int
    n_sample: int
    rows_p: int
    rows_s: int
    nb_p: int
    nb_s: int

    @property
    def n(self):
        return self.n_prompt + self.n_sample


class Tiles(NamedTuple):
    ada_tn: int = 1024
    row_tm: int = 256
    mm_tm: int = 1024
    mm_tn: int = 1024
    pool_tt: int = 256
    pool_bt: int = 16
    mlstm_l: int = 256
    mlstm_hb_p: int = 4
    mlstm_hb_s: int = 4
    mlstm_bb_s: int = 4
    merge_tm: int = 512
    rank_tr: int = 256
    moe_bm: int = 256
    moe_cw: int = 512
    moe_dcw: int = 4096
    comb_tm: int = 256


def _cparams(sem, vmem_bytes):
    limit = int(min(max(vmem_bytes * 5 // 4 + 2 * MIB, 16 * MIB), V7X_VMEM_BYTES - 6 * MIB))
    return pltpu.CompilerParams(dimension_semantics=sem, vmem_limit_bytes=limit)


def _batch_of(row, tok):
    return jnp.where(row < tok.n_prompt, row // tok.rows_p,
                     tok.nb_p + (row - tok.n_prompt) // tok.rows_s)


def _mod_rows(ref, row0, tok, d):
    top = jnp.broadcast_to(ref[pl.ds(_batch_of(row0, tok), 1), :], (SUBLANES, d))
    bot = jnp.broadcast_to(ref[pl.ds(_batch_of(row0 + SUBLANES, tok), 1), :], (SUBLANES, d))
    return jnp.concatenate([top, bot], axis=0)


def _rms(x, w):
    r = lax.rsqrt(jnp.mean(x * x, axis=-1, keepdims=True) + EPS)
    return (x * r) * w


def _log_sigmoid(x):
    return jnp.minimum(x, 0.0) - jnp.log1p(jnp.exp(-jnp.abs(x)))


def _ada_kernel(c_ref, w_ref, b_ref, o_ref):
    c = c_ref[...]
    a = (c * jax.nn.sigmoid(c)).astype(BF16)
    o_ref[...] = jnp.dot(a, w_ref[...].astype(BF16), preferred_element_type=F32) + b_ref[...]


def _ada(c_all, w_ada, b_ada, tiles):
    g, d = c_all.shape
    n = w_ada.shape[1]
    tn = tiles.ada_tn
    vmem = 2 * (g * d * 4 + d * tn * 4 + g * tn * 4) + d * tn * 2
    return pl.pallas_call(
        _ada_kernel,
        grid=(n // tn,),
        in_specs=[pl.BlockSpec((g, d), lambda j: (0, 0)),
                  pl.BlockSpec((d, tn), lambda j: (0, j)),
                  pl.BlockSpec((1, tn), lambda j: (0, j))],
        out_specs=pl.BlockSpec((g, tn), lambda j: (0, j)),
        out_shape=jax.ShapeDtypeStruct((g, n), F32),
        compiler_params=_cparams(("arbitrary",), vmem),
        name="ada",
    )(c_all, w_ada, b_ada.reshape(1, n))


def _dual_specs(tok, tm, d):
    npt = tok.n_prompt // tm
    nst = tok.n_sample // tm
    return (pl.BlockSpec((tm, d), lambda i: (jnp.minimum(i, npt - 1), 0)),
            pl.BlockSpec((tm, d), lambda i: (jnp.clip(i - npt, 0, nst - 1), 0)))


def _prenorm_kernel(xp_ref, xs_ref, sc_ref, sh_ref, nw_ref, wg_ref, u_ref, gates_ref, wgb_ref, *, tok, tm, n_gate):
    i = pl.program_id(0)
    d = u_ref.shape[1]

    @pl.when(i == 0)
    def _():
        wgb_ref[0:n_gate, :] = wg_ref[...].astype(BF16)
        wgb_ref[n_gate:, :] = jnp.zeros((LANES - n_gate, d), BF16)

    def run(x_ref):
        def slab(s, carry):
            r = pl.multiple_of(s * BF16_ROWS, BF16_ROWS)
            row0 = i * tm + r
            xn = _rms(x_ref[pl.ds(r, BF16_ROWS), :], nw_ref[...])
            u = xn * (1.0 + _mod_rows(sc_ref, row0, tok, d)) + _mod_rows(sh_ref, row0, tok, d)
            u_ref[pl.ds(r, BF16_ROWS), :] = u.astype(BF16)
            return carry
        lax.fori_loop(0, tm // BF16_ROWS, slab, 0)

    npt = tok.n_prompt // tm
    pl.when(i < npt)(lambda: run(xp_ref))
    pl.when(i >= npt)(lambda: run(xs_ref))
    gates_ref[...] = lax.dot_general(u_ref[...], wgb_ref[...], (((1,), (1,)), ((), ())),
                                     preferred_element_type=F32)


def _prenorm(xp, xs, mod, norm_w, w_in_t, n_main, n_gate, tok, tiles):
    d = xp.shape[1]
    g = mod.shape[0]
    tm = tiles.row_tm
    assert n_main % n_gate == 0 and n_gate % BF16_ROWS == 0 and n_gate <= LANES
    xp_spec, xs_spec = _dual_specs(tok, tm, d)
    vmem = 2 * (2 * tm * d * 4 + 2 * g * d * 4 + n_gate * d * 4 + tm * d * 2 + tm * LANES * 4) + d * LANES * 2
    return pl.pallas_call(
        functools.partial(_prenorm_kernel, tok=tok, tm=tm, n_gate=n_gate),
        grid=(tok.n // tm,),
        in_specs=[xp_spec, xs_spec,
                  pl.BlockSpec((g, d), lambda i: (0, 1)),
                  pl.BlockSpec((g, d), lambda i: (0, 0)),
                  pl.BlockSpec((1, d), lambda i: (0, 0)),
                  pl.BlockSpec((n_gate, d), lambda i: (n_main // n_gate, 0))],
        out_specs=[pl.BlockSpec((tm, d), lambda i: (i, 0)),
                   pl.BlockSpec((tm, LANES), lambda i: (i, 0))],
        out_shape=[jax.ShapeDtypeStruct((tok.n, d), BF16),
                   jax.ShapeDtypeStruct((tok.n, LANES), F32)],
        scratch_shapes=[pltpu.VMEM((LANES, d), BF16)],
        compiler_params=_cparams(("arbitrary",), vmem),
        name="prenorm",
    )(xp, xs, mod, mod, norm_w.reshape(1, d), w_in_t)


def _matmul_kernel(a_ref, w_hbm, o_ref, stage, wb_ref, sem, *, w_is_nk, tn):
    j = pl.program_id(0)
    i = pl.program_id(1)

    def tile_copy(jj):
        cols = pl.ds(pl.multiple_of(jj * tn, tn), tn)
        src = w_hbm.at[cols, :] if w_is_nk else w_hbm.at[:, cols]
        return pltpu.make_async_copy(src, stage, sem.at[0])

    @pl.when((j == 0) & (i == 0))
    def _():
        tile_copy(0).start()

    @pl.when(i == 0)
    def _():
        tile_copy(j).wait()
        wb_ref[...] = stage[...].astype(BF16)

        @pl.when(j + 1 < pl.num_programs(0))
        def _():
            tile_copy(j + 1).start()

    contract_w = 1 if w_is_nk else 0
    o_ref[...] = lax.dot_general(a_ref[...], wb_ref[...], (((1,), (contract_w,)), ((), ())),
                                 preferred_element_type=F32)


def _matmul(a, w, n_cols, w_is_nk, tiles):
    m, k = a.shape
    tm, tn = tiles.mm_tm, tiles.mm_tn
    w_block = (tn, k) if w_is_nk else (k, tn)
    vmem = 2 * (tm * k * 2 + tm * tn * 4) + k * tn * 4 + k * tn * 2
    return pl.pallas_call(
        functools.partial(_matmul_kernel, w_is_nk=w_is_nk, tn=tn),
        grid=(n_cols // tn, m // tm),
        in_specs=[pl.BlockSpec((tm, k), lambda j, i: (i, 0)),
                  pl.BlockSpec(memory_space=pl.ANY)],
        out_specs=pl.BlockSpec((tm, tn), lambda j, i: (i, j)),
        out_shape=jax.ShapeDtypeStruct((m, n_cols), F32),
        scratch_shapes=[pltpu.VMEM(w_block, F32), pltpu.VMEM(w_block, BF16), pltpu.SemaphoreType.DMA((1,))],
        compiler_params=_cparams(("arbitrary", "arbitrary"), vmem),
        name="matmul",
    )(a, w)


def _pool_kernel(p_ref, st_ref, d_ref, new_ref, ext_ref, *, bt, tt, gin, pos0, nt):
    t = pl.program_id(1)
    cdim = ext_ref.shape[2]
    halo = POOL_BUF + 1

    @pl.when(t == 0)
    def _():
        ext_ref[:, 1:halo, :] = st_ref[...]

    @pl.when(t > 0)
    def _():
        ext_ref[:, 0:halo, :] = ext_ref[:, tt:tt + halo, :]

    ext_ref[:, halo:halo + tt, :] = p_ref[...].reshape(bt, tt, cdim)
    pos = pos0 + t * tt + lax.broadcasted_iota(I32, (bt, tt, gin), 1)
    for g, w in enumerate(POOL_WINDOWS):
        cs = slice(g * gin, (g + 1) * gin)
        cur = ext_ref[:, halo:halo + tt, cs]
        acc = cur
        for j in range(1, w):
            acc = acc + ext_ref[:, halo - j:halo - j + tt, cs]
        cnt = jnp.minimum(pos + 1, w).astype(F32)
        d_ref[:, cs] = (acc / cnt - cur).reshape(bt * tt, gin)

    @pl.when(t == nt - 1)
    def _():
        new_ref[...] = ext_ref[:, tt + 1:tt + halo, :]


def _pool(z, state, row_off, nb, t_len, bt, tt, pos0):
    cdim = state.shape[2]
    gin = cdim // len(POOL_WINDOWS)
    nt = t_len // tt
    rows = bt * tt
    rb0 = row_off // rows
    vmem = 2 * (2 * rows * cdim * 4 + 2 * bt * 16 * cdim * 4) + bt * (tt + 16) * cdim * 4 + 8 * rows * gin * 4
    return pl.pallas_call(
        functools.partial(_pool_kernel, bt=bt, tt=tt, gin=gin, pos0=pos0, nt=nt),
        grid=(nb // bt, nt),
        in_specs=[pl.BlockSpec((rows, cdim), lambda b, t: (rb0 + b * nt + t, 0)),
                  pl.BlockSpec((bt, POOL_BUF, cdim), lambda b, t: (b, 0, 0))],
        out_specs=[pl.BlockSpec((rows, cdim), lambda b, t: (b * nt + t, 0)),
                   pl.BlockSpec((bt, POOL_BUF, cdim), lambda b, t: (b, 0, 0))],
        out_shape=[jax.ShapeDtypeStruct((nb * t_len, cdim), F32),
                   jax.ShapeDtypeStruct((nb, POOL_BUF, cdim), F32)],
        scratch_shapes=[pltpu.VMEM((bt, tt + POOL_BUF + 1, cdim), F32)],
        compiler_params=_cparams(("arbitrary", "arbitrary"), vmem),
        name="pool",
    )(z, state)


def _mlstm_kernel(q_ref, k_ref, v_ref, o_ref, gc_ref, gr_ref, brow_ref, bcol_ref, nw_ref, c0_ref, n0_ref, m0_ref,
                  mo_ref, c_out, n_out, m_out, c_s, n_s, m_s, *, L, bb, hb, dk, dv, nc, nheads):
    hblk = pl.program_id(1)
    c = pl.program_id(2)

    @pl.when(c == 0)
    def _():
        c_s[...] = c0_ref[...]
        n_s[...] = n0_ref[...]
        m_s[...] = m0_ref[...]

    lane = lax.broadcasted_iota(I32, (L, LANES), 1)
    sub = lax.broadcasted_iota(I32, (2 * nheads, L), 0)
    rowi = lax.broadcasted_iota(I32, (L, L), 0)
    coli = lax.broadcasted_iota(I32, (L, L), 1)
    tri = rowi >= coli
    scale = dk ** -0.5

    units = [(bi, hh) for bi in range(bb) for hh in range(hb)]
    heads = range(len(units))
    rsl = [slice(bi * L, (bi + 1) * L) for bi, _ in units]
    gcb = [gc_ref[rsl[i], :] + brow_ref[...] for i in heads]
    grb = [gr_ref[units[i][0]] + bcol_ref[...] for i in heads]
    hs_dyn = [hblk * hb + hh for _, hh in units]
    li_col = [jnp.sum(jnp.where(lane == hs_dyn[i], gcb[i], 0.0), axis=1, keepdims=True) for i in heads]
    lf_col = [jnp.sum(jnp.where(lane == hs_dyn[i] + nheads, gcb[i], 0.0), axis=1, keepdims=True) for i in heads]
    li_row = [jnp.sum(jnp.where(sub == hs_dyn[i], grb[i], 0.0), axis=0, keepdims=True) for i in heads]
    lf_row = [jnp.sum(jnp.where(sub == hs_dyn[i] + nheads, grb[i], 0.0), axis=0, keepdims=True) for i in heads]
    lf_col = [_log_sigmoid(x) for x in lf_col]
    lf_row = [_log_sigmoid(x) for x in lf_row]
    b_col = [jnp.sum(jnp.where(tri, x, 0.0), axis=1, keepdims=True) for x in lf_row]
    b_row = [jnp.sum(jnp.where(rowi <= coli, x, 0.0), axis=0, keepdims=True) for x in lf_col]

    q = [q_ref[rsl[i], hh * dk:(hh + 1) * dk] * scale for i, (_, hh) in enumerate(units)]
    k = [k_ref[rsl[i], hh * dk:(hh + 1) * dk] for i, (_, hh) in enumerate(units)]
    qb = [x.astype(BF16) for x in q]
    kb = [x.astype(BF16) for x in k]
    vb = [v_ref[rsl[i], hh * dv:(hh + 1) * dv].astype(BF16) for i, (_, hh) in enumerate(units)]
    cmat = [c_s[bi, hh] for bi, hh in units]
    nvec = [n_s[bi, hh] for bi, hh in units]
    m_prev = [m_s[bi, hh] for bi, hh in units]

    dmat = [jnp.where(tri, b_col[i] - b_row[i] + li_row[i], -jnp.inf) for i in heads]
    inter = [b_col[i] + m_prev[i] for i in heads]
    m_t = [jnp.maximum(inter[i], jnp.max(dmat[i], axis=1, keepdims=True)) for i in heads]
    dw = [jnp.exp(dmat[i] - m_t[i]) for i in heads]
    iw = [jnp.exp(inter[i] - m_t[i]) for i in heads]
    qk = [lax.dot_general(qb[i], kb[i], (((1,), (1,)), ((), ())), preferred_element_type=F32) for i in heads]
    qc = [jnp.dot(qb[i], cmat[i].astype(BF16), preferred_element_type=F32) for i in heads]
    s = [qk[i] * dw[i] for i in heads]
    num = [jnp.dot(s[i].astype(BF16), vb[i], preferred_element_type=F32) + iw[i] * qc[i] for i in heads]
    den = [jnp.sum(s[i], axis=1, keepdims=True) + iw[i] * jnp.sum(q[i] * nvec[i], axis=1, keepdims=True)
           for i in heads]
    hval = [num[i] / jnp.maximum(jnp.abs(den[i]), jnp.exp(-m_t[i])) for i in heads]
    for i, (_, hh) in enumerate(units):
        hs = slice(hh * dv, (hh + 1) * dv)
        mo_ref[rsl[i], hs] = jax.nn.sigmoid(o_ref[rsl[i], hs]) * _rms(hval[i], nw_ref[:, hs])

    b_last = [x[L - 1:L, :] for x in b_col]
    dl_col = [b_last[i] - b_col[i] + li_col[i] for i in heads]
    dl_row = [b_last[i] - b_row[i] + li_row[i] for i in heads]
    m_new = [jnp.maximum(b_last[i] + m_prev[i], jnp.max(dl_row[i], axis=1, keepdims=True)) for i in heads]
    dec = [jnp.exp(b_last[i] + m_prev[i] - m_new[i]) for i in heads]
    kk = [jnp.exp(dl_col[i] - m_new[i]) * k[i] for i in heads]
    upd = [lax.dot_general(kk[i].astype(BF16), vb[i], (((0,), (0,)), ((), ())), preferred_element_type=F32)
           for i in heads]
    for i, (bi, hh) in enumerate(units):
        c_s[bi, hh] = dec[i] * cmat[i] + upd[i]
        n_s[bi, hh] = dec[i] * nvec[i] + jnp.sum(kk[i], axis=0, keepdims=True)
        m_s[bi, hh] = m_new[i]

    @pl.when(c == nc - 1)
    def _():
        c_out[...] = c_s[...]
        n_out[...] = n_s[...]
        m_out[...] = m_s[...]


def _mlstm(z, gates, gates_t, bias_row, bias_col, norm_w, c0, n0, m0, row_off, nb, t_len, L, bb, hb, offs):
    _, nheads, dk, dv = c0.shape
    nc = t_len // L
    assert bb == 1 or nc == 1
    rows = bb * L
    rb0 = row_off // rows
    nhb = nheads // hb
    q_off, k_off, v_off, o_off = offs
    wq, wv = hb * dk, hb * dv

    def rowblk(b, c):
        return rb0 + b * nc + c

    in_specs = [
        pl.BlockSpec((rows, wq), lambda b, h, c: (rowblk(b, c), q_off // wq + h)),
        pl.BlockSpec((rows, wq), lambda b, h, c: (rowblk(b, c), k_off // wq + h)),
        pl.BlockSpec((rows, wv), lambda b, h, c: (rowblk(b, c), v_off // wv + h)),
        pl.BlockSpec((rows, wv), lambda b, h, c: (rowblk(b, c), o_off // wv + h)),
        pl.BlockSpec((rows, LANES), lambda b, h, c: (rowblk(b, c), 0)),
        pl.BlockSpec((bb, 2 * nheads, L), lambda b, h, c: (b * nc + c, 0, 0)),
        pl.BlockSpec((1, LANES), lambda b, h, c: (0, 0)),
        pl.BlockSpec((2 * nheads, 1), lambda b, h, c: (0, 0)),
        pl.BlockSpec((1, wv), lambda b, h, c: (0, h)),
        pl.BlockSpec((bb, hb, dk, dv), lambda b, h, c: (b, h, 0, 0)),
        pl.BlockSpec((bb, hb, 1, dk), lambda b, h, c: (b, h, 0, 0)),
        pl.BlockSpec((bb, hb, 1, 1), lambda b, h, c: (b, h, 0, 0)),
    ]
    state_bytes = bb * hb * dk * dv * 4
    vmem = (2 * (2 * rows * wq * 4 + 3 * rows * wv * 4 + 2 * state_bytes) + state_bytes
            + 24 * bb * L * max(L, dv) * 4)
    return pl.pallas_call(
        functools.partial(_mlstm_kernel, L=L, bb=bb, hb=hb, dk=dk, dv=dv, nc=nc, nheads=nheads),
        grid=(nb // bb, nhb, nc),
        in_specs=in_specs,
        out_specs=[pl.BlockSpec((rows, wv), lambda b, h, c: (b * nc + c, h)),
                   pl.BlockSpec((bb, hb, dk, dv), lambda b, h, c: (b, h, 0, 0)),
                   pl.BlockSpec((bb, hb, 1, dk), lambda b, h, c: (b, h, 0, 0)),
                   pl.BlockSpec((bb, hb, 1, 1), lambda b, h, c: (b, h, 0, 0))],
        out_shape=[jax.ShapeDtypeStruct((nb * t_len, nheads * dv), F32),
                   jax.ShapeDtypeStruct(c0.shape, F32),
                   jax.ShapeDtypeStruct(n0.shape, F32),
                   jax.ShapeDtypeStruct(m0.shape, F32)],
        scratch_shapes=[pltpu.VMEM((bb, hb, dk, dv), F32), pltpu.VMEM((bb, hb, 1, dk), F32),
                        pltpu.VMEM((bb, hb, 1, 1), F32)],
        compiler_params=_cparams(("arbitrary", "arbitrary", "arbitrary"), vmem),
        name="mlstm",
    )(z, z, z, z, gates, gates_t, bias_row, bias_col, norm_w, c0, n0, m0)


def _merge_kernel(dp_ref, ds_ref, wp_ref, ps_ref, ga_ref, gb_ref, mop_ref, mos_ref, o_ref, *, gin, gout, npt):
    def run(d_ref, mo_ref):
        for gg in range(wp_ref.shape[0]):
            y = jnp.dot(d_ref[:, gg * gin:(gg + 1) * gin].astype(BF16), wp_ref[gg], preferred_element_type=F32)
            cs = slice(gg * gout, (gg + 1) * gout)
            pool_out = y * ps_ref[:, cs]
            merged = jax.nn.sigmoid(ga_ref[:, cs]) * pool_out + jax.nn.sigmoid(gb_ref[:, cs]) * mo_ref[:, cs]
            o_ref[:, cs] = merged.astype(BF16)

    i = pl.program_id(0)
    pl.when(i < npt)(lambda: run(dp_ref, mop_ref))
    pl.when(i >= npt)(lambda: run(ds_ref, mos_ref))


def _merge(d_p, d_s, w_pool_b, pool_scale, z, mo_p, mo_s, ga_off, gb_off, tok, tiles):
    cdim = d_p.shape[1]
    ng, gin, gout = w_pool_b.shape
    dm = ng * gout
    half = dm // 2
    gh = ng // 2
    tm = tiles.merge_tm
    npt = tok.n_prompt // tm
    nst = tok.n_sample // tm

    def prompt_spec(width):
        return pl.BlockSpec((tm, width), lambda i, j: (jnp.minimum(i, npt - 1), jnp.where(i < npt, j, 1)))

    def sample_spec(width):
        return pl.BlockSpec((tm, width), lambda i, j: (jnp.clip(i - npt, 0, nst - 1), jnp.where(i < npt, 0, j)))

    vmem = 2 * (2 * tm * cdim * 2 + gh * gin * gout * 2 + 4 * tm * half * 4 + tm * half * 2) + 6 * tm * gout * 4
    return pl.pallas_call(
        functools.partial(_merge_kernel, gin=gin, gout=gout, npt=npt),
        grid=(tok.n // tm, 2),
        in_specs=[prompt_spec(cdim // 2), sample_spec(cdim // 2),
                  pl.BlockSpec((gh, gin, gout), lambda i, j: (j, 0, 0)),
                  pl.BlockSpec((1, half), lambda i, j: (0, j)),
                  pl.BlockSpec((tm, half), lambda i, j: (i, ga_off // half + j)),
                  pl.BlockSpec((tm, half), lambda i, j: (i, gb_off // half + j)),
                  prompt_spec(half), sample_spec(half)],
        out_specs=pl.BlockSpec((tm, half), lambda i, j: (i, j)),
        out_shape=jax.ShapeDtypeStruct((tok.n, dm), BF16),
        compiler_params=_cparams(("arbitrary", "arbitrary"), vmem),
        name="merge",
    )(d_p, d_s, w_pool_b, pool_scale.reshape(1, dm), z, z, mo_p, mo_s)


def _ffn_norm_kernel(xp_ref, xs_ref, acc_ref, g1_ref, sc_ref, sh_ref, nw_ref, wr_ref,
                     x1_ref, up_ref, re_ref, rw_ref, ub_ref, *, tok, tm):
    i = pl.program_id(0)
    d = x1_ref.shape[1]
    hd = d // 2

    def run(x_ref):
        def slab(s, carry):
            r = pl.multiple_of(s * BF16_ROWS, BF16_ROWS)
            rows = pl.ds(r, BF16_ROWS)
            row0 = i * tm + r
            x1 = x_ref[rows, :] + _mod_rows(g1_ref, row0, tok, d) * acc_ref[rows, :]
            x1_ref[rows, :] = x1
            u = _rms(x1, nw_ref[...]) * (1.0 + _mod_rows(sc_ref, row0, tok, d)) + _mod_rows(sh_ref, row0, tok, d)
            ub = u.astype(BF16)
            ub_ref[rows, :] = ub
            bits = lax.bitcast_convert_type(ub.astype(F32), U32)
            up_ref[rows, :] = (bits[:, hd:] & jnp.uint32(0xFFFF0000)) | (bits[:, :hd] >> 16)
            return carry
        lax.fori_loop(0, tm // BF16_ROWS, slab, 0)

    npt = tok.n_prompt // tm
    pl.when(i < npt)(lambda: run(xp_ref))
    pl.when(i >= npt)(lambda: run(xs_ref))

    lg = jnp.dot(ub_ref[...], wr_ref[...], preferred_element_type=F32)
    lane = lax.broadcasted_iota(I32, lg.shape, 1)
    lanef = lane.astype(F32)
    big = float(LANES)
    is_g = lane < N_GROUPS
    gl = jnp.where(is_g, lg, -jnp.inf)
    gmax = jnp.max(gl, axis=1, keepdims=True)
    grp = jnp.min(jnp.where(gl == gmax, lanef, big), axis=1, keepdims=True)
    p_grp = 1.0 / jnp.sum(jnp.where(is_g, jnp.exp(lg - gmax), 0.0), axis=1, keepdims=True)
    eidx = lane - N_GROUPS
    in_grp = (eidx >= 0) & (eidx < N_EXPERTS) & ((eidx >> EXP_SHIFT).astype(F32) == grp)
    el = jnp.where(in_grp, lg, -jnp.inf)
    v1 = jnp.max(el, axis=1, keepdims=True)
    i1 = jnp.min(jnp.where(el == v1, lanef, big), axis=1, keepdims=True)
    el2 = jnp.where(lanef == i1, -jnp.inf, el)
    v2 = jnp.max(el2, axis=1, keepdims=True)
    i2 = jnp.min(jnp.where(el2 == v2, lanef, big), axis=1, keepdims=True)
    e2 = jnp.exp(v2 - v1)
    w1 = (1.0 / (1.0 + e2)) * p_grp
    w2 = (e2 / (1.0 + e2)) * p_grp
    re_ref[...] = jnp.where(lane == 0, i1 - N_GROUPS, jnp.where(lane == 1, i2 - N_GROUPS, 0.0))
    rw_ref[...] = jnp.where(lane == 0, w1, jnp.where(lane == 1, w2, 0.0))


def _ffn_norm(xp, xs, acc, mod, norm_w, w_router, tok, tiles):
    d = xp.shape[1]
    g = mod.shape[0]
    tm = tiles.row_tm
    xp_spec, xs_spec = _dual_specs(tok, tm, d)
    vmem = (2 * (3 * tm * d * 4 + 3 * g * d * 4 + d * LANES * 2 + tm * d * 4 + tm * d * 2 + 2 * tm * LANES * 4)
            + tm * d * 2 + 16 * tm * LANES * 4)
    return pl.pallas_call(
        functools.partial(_ffn_norm_kernel, tok=tok, tm=tm),
        grid=(tok.n // tm,),
        in_specs=[xp_spec, xs_spec,
                  pl.BlockSpec((tm, d), lambda i: (i, 0)),
                  pl.BlockSpec((g, d), lambda i: (0, 2)),
                  pl.BlockSpec((g, d), lambda i: (0, 4)),
                  pl.BlockSpec((g, d), lambda i: (0, 3)),
                  pl.BlockSpec((1, d), lambda i: (0, 0)),
                  pl.BlockSpec((d, LANES), lambda i: (0, 0))],
        out_specs=[pl.BlockSpec((tm, d), lambda i: (i, 0)),
                   pl.BlockSpec((tm, d // 2), lambda i: (i, 0)),
                   pl.BlockSpec((tm, LANES), lambda i: (i, 0)),
                   pl.BlockSpec((tm, LANES), lambda i: (i, 0))],
        out_shape=[jax.ShapeDtypeStruct((tok.n, d), F32),
                   jax.ShapeDtypeStruct((tok.n, d // 2), U32),
                   jax.ShapeDtypeStruct((tok.n, LANES), F32),
                   jax.ShapeDtypeStruct((tok.n, LANES), F32)],
        scratch_shapes=[pltpu.VMEM((tm, d), BF16)],
        compiler_params=_cparams(("arbitrary",), vmem),
        name="ffn_norm",
    )(xp, xs, acc, mod, mod, mod, norm_w.reshape(1, d), w_router)


def _rank_kernel(re_ref, rank_ref, cnt_ref, carry_ref, *, tr):
    j = pl.program_id(0)
    i = pl.program_id(1)

    @pl.when((j == 0) & (i == 0))
    def _():
        carry_ref[...] = jnp.zeros_like(carry_ref)

    re = re_ref[...]
    lane = lax.broadcasted_iota(I32, re.shape, 1)
    e_col = jnp.sum(jnp.where(lane == j, re, 0.0), axis=1, keepdims=True)
    onehot = lane.astype(F32) == e_col
    rowi = lax.broadcasted_iota(I32, (tr, tr), 0)
    coli = lax.broadcasted_iota(I32, (tr, tr), 1)
    tri = jnp.where(rowi >= coli, 1.0, 0.0).astype(BF16)
    prefix = jnp.dot(tri, jnp.where(onehot, 1.0, 0.0).astype(BF16), preferred_element_type=F32)
    carry = carry_ref[0:1, :]
    rank = jnp.sum(jnp.where(onehot, prefix - 1.0 + carry, 0.0), axis=1, keepdims=True)
    rank_ref[...] = jnp.broadcast_to(rank, rank_ref.shape)
    new_carry = carry + prefix[tr - 1:tr, :]
    carry_ref[...] = jnp.broadcast_to(new_carry, carry_ref.shape)
    cnt_ref[...] = jnp.broadcast_to(new_carry, cnt_ref.shape)


def _rank(re, tiles):
    n = re.shape[0]
    tr = tiles.rank_tr
    nt = n // tr
    return pl.pallas_call(
        functools.partial(_rank_kernel, tr=tr),
        grid=(TOP_K, nt),
        in_specs=[pl.BlockSpec((tr, LANES), lambda j, i: (i, 0))],
        out_specs=[pl.BlockSpec((tr, LANES), lambda j, i: (j * nt + i, 0)),
                   pl.BlockSpec((SUBLANES, LANES), lambda j, i: (0, 0))],
        out_shape=[jax.ShapeDtypeStruct((TOP_K * n, LANES), F32),
                   jax.ShapeDtypeStruct((SUBLANES, LANES), F32)],
        scratch_shapes=[pltpu.VMEM((SUBLANES, LANES), F32)],
        compiler_params=_cparams(("arbitrary", "arbitrary"), 4 * MIB),
        name="rank",
    )(re)


def _invert_kernel(dest_ref, inv_ref, *, n_assign, n_rows):
    group = 8

    def put(g, carry):
        a0 = g * group
        dests = [dest_ref[a0 + u] for u in range(group)]
        for u in range(group):
            inv_ref[dests[u]] = lax.shift_right_logical(a0 + u, TOP_K.bit_length() - 1)
        return carry
    lax.fori_loop(0, n_assign // group, put, 0, unroll=2)


def _invert(dest_flat, n_rows):
    assert dest_flat.shape[0] % 8 == 0 and TOP_K & (TOP_K - 1) == 0
    return pl.pallas_call(
        functools.partial(_invert_kernel, n_assign=dest_flat.shape[0], n_rows=n_rows),
        in_specs=[pl.BlockSpec(memory_space=pltpu.SMEM)],
        out_specs=pl.BlockSpec(memory_space=pltpu.SMEM),
        out_shape=jax.ShapeDtypeStruct((n_rows,), I32),
        name="invert",
    )(dest_flat)


def _gather_kernel(nused_ref, nvalid_ref, inv_ref, src_ref, o_ref, buf, sems, *, bm):
    b = pl.program_id(0)
    n_used = nused_ref[0]

    def row_copy(t, r, slot):
        return pltpu.make_async_copy(src_ref.at[pl.ds(t, 1)], buf.at[slot, pl.ds(r, 1)], sems.at[slot])

    def issue(blk, slot):
        def body(r, carry):
            row_copy(inv_ref[blk * bm + r], r, slot).start()
            return carry
        lax.fori_loop(0, nvalid_ref[blk], body, 0)

    @pl.when(b == 0)
    def _():
        buf[...] = jnp.zeros_like(buf)
        issue(0, 0)

    @pl.when(b + 1 < n_used)
    def _():
        issue(b + 1, (b + 1) % 2)

    @pl.when(b < n_used)
    def _():
        slot = b % 2
        nv = nvalid_ref[b]

        def drain(r, carry):
            row_copy(0, r, slot).wait()
            return carry
        lax.fori_loop(0, nv, drain, 0)
        rows = lax.broadcasted_iota(I32, o_ref.shape, 0)
        o_ref[...] = jnp.where(rows < nv, buf[slot], jnp.zeros_like(o_ref))


def _gather(inv, n_used, n_valid, src, tiles):
    n_rows = inv.shape[0]
    width = src.shape[1]
    bm = tiles.moe_bm
    return pl.pallas_call(
        functools.partial(_gather_kernel, bm=bm),
        grid_spec=pltpu.PrefetchScalarGridSpec(
            num_scalar_prefetch=3,
            grid=(n_rows // bm,),
            in_specs=[pl.BlockSpec(memory_space=pl.ANY)],
            out_specs=pl.BlockSpec((bm, width), lambda b, nu, nv, inv: (jnp.minimum(b, nu[0] - 1), 0)),
            scratch_shapes=[pltpu.VMEM((2, bm, width), src.dtype), pltpu.SemaphoreType.DMA((2,))]),
        out_shape=jax.ShapeDtypeStruct((n_rows, width), src.dtype),
        compiler_params=_cparams(("arbitrary",), 4 * bm * width * 4),
        name="gather",
    )(n_used, n_valid, inv, src)


def _unpack_pair(words):
    lo = lax.bitcast_convert_type(words << 16, F32).astype(BF16)
    hi = lax.bitcast_convert_type(words & jnp.uint32(0xFFFF0000), F32).astype(BF16)
    return lo, hi


def _expert_weights(w, plan_refs, w_hbm, stage, sems, cw):
    _, col_ref, exp_ref, first_ref, _, slot_ref, nxt_e_ref, nxt_c_ref = plan_refs

    def copies(e, c, slot):
        cols = pl.ds(pl.multiple_of(c * cw, cw), cw)
        return [pltpu.make_async_copy(w_hbm[k].at[e, :, cols], stage.at[slot, k], sems.at[slot, k])
                for k in range(len(w_hbm))]

    @pl.when(w == 0)
    def _():
        for cp in copies(exp_ref[0], col_ref[0], 0):
            cp.start()

    @pl.when(first_ref[w] == 1)
    def _():
        for cp in copies(exp_ref[w], col_ref[w], slot_ref[w]):
            cp.wait()

        @pl.when(nxt_e_ref[w] >= 0)
        def _():
            for cp in copies(nxt_e_ref[w], nxt_c_ref[w], 1 - slot_ref[w]):
                cp.start()


def _moe_up_kernel(*refs, cw):
    plan_refs = refs[:8]
    xs_ref, wg_hbm, wu_hbm, h_ref, stage, wgb, wub, sems = refs[8:]
    first_ref, valid_ref, slot_ref = plan_refs[3], plan_refs[4], plan_refs[5]
    w = pl.program_id(0)
    hd = xs_ref.shape[1]
    _expert_weights(w, plan_refs, (wg_hbm, wu_hbm), stage, sems, cw)

    @pl.when(first_ref[w] == 1)
    def _():
        wgb[...] = stage[slot_ref[w], 0].astype(BF16)
        wub[...] = stage[slot_ref[w], 1].astype(BF16)

    @pl.when(valid_ref[w] == 1)
    def _():
        lo, hi = _unpack_pair(xs_ref[...])
        g = (jnp.dot(lo, wgb[:hd, :], preferred_element_type=F32)
             + jnp.dot(hi, wgb[hd:, :], preferred_element_type=F32))
        u = (jnp.dot(lo, wub[:hd, :], preferred_element_type=F32)
             + jnp.dot(hi, wub[hd:, :], preferred_element_type=F32))
        h_ref[...] = ((g * jax.nn.sigmoid(g)) * u).astype(BF16)


def _moe_up(plan, xs, w_gate, w_up, tiles):
    p_rows, hd = xs.shape
    _, d, de = w_gate.shape
    bm, cw = tiles.moe_bm, tiles.moe_cw
    n_items = plan["up"][0].shape[0]
    vmem = 2 * (bm * hd * 4 + bm * cw * 2) + 4 * d * cw * 4 + 2 * d * cw * 2 + 8 * bm * cw * 4 + 2 * bm * d * 2

    def by_item(f):
        return lambda w, blk, col, ex, fi, va, sl, ne, nc: f(w, blk, col)

    return pl.pallas_call(
        functools.partial(_moe_up_kernel, cw=cw),
        grid_spec=pltpu.PrefetchScalarGridSpec(
            num_scalar_prefetch=8,
            grid=(n_items,),
            in_specs=[pl.BlockSpec((bm, hd), by_item(lambda w, blk, col: (blk[w], 0))),
                      pl.BlockSpec(memory_space=pl.ANY),
                      pl.BlockSpec(memory_space=pl.ANY)],
            out_specs=pl.BlockSpec((bm, cw), by_item(lambda w, blk, col: (blk[w], col[w]))),
            scratch_shapes=[pltpu.VMEM((2, 2, d, cw), F32), pltpu.VMEM((d, cw), BF16), pltpu.VMEM((d, cw), BF16),
                            pltpu.SemaphoreType.DMA((2, 2))]),
        out_shape=jax.ShapeDtypeStruct((p_rows, de), BF16),
        compiler_params=_cparams(("arbitrary",), vmem),
        name="moe_up",
    )(*plan["up"], xs, w_gate, w_up)


def _moe_down_kernel(*refs, dcw):
    plan_refs = refs[:8]
    h_ref, wd_hbm, y_ref, stage, wdb, sems = refs[8:]
    first_ref, valid_ref, slot_ref = plan_refs[3], plan_refs[4], plan_refs[5]
    w = pl.program_id(0)
    _expert_weights(w, plan_refs, (wd_hbm,), stage, sems, dcw)

    @pl.when(first_ref[w] == 1)
    def _():
        wdb[...] = stage[slot_ref[w], 0].astype(BF16)

    @pl.when(valid_ref[w] == 1)
    def _():
        y_ref[...] = jnp.dot(h_ref[...], wdb[...], preferred_element_type=F32)


def _moe_down(plan, hdn, w_down, tiles):
    p_rows, de = hdn.shape
    d = w_down.shape[2]
    bm, dcw = tiles.moe_bm, tiles.moe_dcw
    n_items = plan["down"][0].shape[0]
    vmem = 2 * (bm * de * 2 + bm * dcw * 4) + 2 * de * dcw * 4 + de * dcw * 2 + 2 * bm * dcw * 4

    def by_item(f):
        return lambda w, blk, col, ex, fi, va, sl, ne, nc: f(w, blk, col)

    return pl.pallas_call(
        functools.partial(_moe_down_kernel, dcw=dcw),
        grid_spec=pltpu.PrefetchScalarGridSpec(
            num_scalar_prefetch=8,
            grid=(n_items,),
            in_specs=[pl.BlockSpec((bm, de), by_item(lambda w, blk, col: (blk[w], 0))),
                      pl.BlockSpec(memory_space=pl.ANY)],
            out_specs=pl.BlockSpec((bm, dcw), by_item(lambda w, blk, col: (blk[w], col[w]))),
            scratch_shapes=[pltpu.VMEM((2, 1, de, dcw), F32), pltpu.VMEM((de, dcw), BF16),
                            pltpu.SemaphoreType.DMA((2, 1))]),
        out_shape=jax.ShapeDtypeStruct((p_rows, d), F32),
        compiler_params=_cparams(("arbitrary",), vmem),
        name="moe_down",
    )(*plan["down"], hdn, w_down)


def _work_items(nb_e, blk0_e, n_cols, n_items):
    items_e = nb_e * n_cols
    cum = jnp.cumsum(items_e)
    total = cum[-1]
    w = jnp.minimum(jnp.arange(n_items, dtype=I32), total - 1)
    own = (w[:, None] >= (cum - items_e)[None, :]) & (w[:, None] < cum[None, :])

    def pick(v):
        return jnp.sum(jnp.where(own, v[None, :], 0), axis=1)

    e = pick(jnp.arange(N_EXPERTS, dtype=I32))
    r = w - pick(cum - items_e)
    nb = jnp.maximum(pick(nb_e), 1)
    col = r // nb
    blk = pick(blk0_e) + r % nb
    valid = (jnp.arange(n_items, dtype=I32) < total).astype(I32)
    key = e * n_cols + col
    first = jnp.concatenate([jnp.ones((1,), I32), (key[1:] != key[:-1]).astype(I32)]) * valid
    slot = (jnp.cumsum(first) - 1) % 2
    idx = jnp.arange(n_items, dtype=I32)
    nxt = lax.cummin(jnp.where(first == 1, idx, n_items)[::-1])[::-1]
    nxt = jnp.concatenate([nxt[1:], jnp.full((1,), n_items, I32)])
    has_next = nxt < n_items
    pick_next = nxt[:, None] == idx[None, :]
    nxt_e = jnp.where(has_next, jnp.sum(jnp.where(pick_next, e[None, :], 0), axis=1), -1)
    nxt_c = jnp.where(has_next, jnp.sum(jnp.where(pick_next, col[None, :], 0), axis=1), 0)
    return (blk.astype(I32), col.astype(I32), e.astype(I32), first.astype(I32), valid, slot.astype(I32),
            nxt_e.astype(I32), nxt_c.astype(I32))


def _rows_per_block(counts, nb_e, blk0_e, n_blocks, bm):
    b = jnp.arange(n_blocks, dtype=I32)[:, None]
    own = (b >= blk0_e[None, :]) & (b < (blk0_e + nb_e)[None, :])
    rows = jnp.clip(counts[None, :] - (b - blk0_e[None, :]) * bm, 0, bm)
    return jnp.sum(jnp.where(own, rows, 0), axis=1).astype(I32)


def _dispatch_plan(re, rank_out, cnt_out, n, n_blocks, d_expert, d_model, tiles):
    bm = tiles.moe_bm
    eid = re[:, :TOP_K].astype(I32)
    rank = rank_out[:, 0].reshape(TOP_K, n).T.astype(I32)
    counts = cnt_out[0, :N_EXPERTS].astype(I32)
    nb_e = (counts + bm - 1) // bm
    blk0_e = jnp.cumsum(nb_e) - nb_e
    row0 = jnp.sum(jnp.where(eid[:, :, None] == jnp.arange(N_EXPERTS, dtype=I32), blk0_e * bm, 0), axis=2)
    dest = row0 + rank
    plan = {
        "dest": dest.reshape(-1).astype(I32),
        "n_used": jnp.sum(nb_e).reshape(1).astype(I32),
        "n_valid": _rows_per_block(counts, nb_e, blk0_e, n_blocks, bm),
        "up": _work_items(nb_e, blk0_e, d_expert // tiles.moe_cw, n_blocks * (d_expert // tiles.moe_cw)),
        "down": _work_items(nb_e, blk0_e, d_model // tiles.moe_dcw, n_blocks * (d_model // tiles.moe_dcw)),
    }
    return plan


def _combine_kernel(dest_ref, ys_ref, x1_ref, rw_ref, g2_ref, fw_ref, yp_ref, ysm_ref, ybuf, sems, *, tok, tm):
    i = pl.program_id(0)
    d = x1_ref.shape[1]
    base = i * tm
    nt = pl.num_programs(0)

    def issue(tile, slot):
        def body(r, carry):
            for j in range(TOP_K):
                pltpu.make_async_copy(ys_ref.at[pl.ds(dest_ref[(tile * tm + r) * TOP_K + j], 1)],
                                      ybuf.at[slot, j, pl.ds(r, 1)], sems.at[slot, j]).start()
            return carry
        lax.fori_loop(0, tm, body, 0, unroll=4)

    @pl.when(i == 0)
    def _():
        issue(0, 0)

    @pl.when(i + 1 < nt)
    def _():
        issue(i + 1, (i + 1) % 2)

    slot = i % 2
    for j in range(TOP_K):
        pltpu.make_async_copy(ys_ref.at[pl.ds(0, tm)], ybuf.at[slot, j], sems.at[slot, j]).wait()

    def run(out_ref):
        def slab(s, carry):
            r = pl.multiple_of(s * SUBLANES, SUBLANES)
            rows = pl.ds(r, SUBLANES)
            wts = rw_ref[rows, :]
            ff = wts[:, 0:1] * ybuf[slot, 0, rows, :] + wts[:, 1:2] * ybuf[slot, 1, rows, :]
            g2 = jnp.broadcast_to(g2_ref[pl.ds(_batch_of(base + r, tok), 1), :], (SUBLANES, d))
            out_ref[rows, :] = _rms(x1_ref[rows, :] + g2 * ff, fw_ref[...])
            return carry
        lax.fori_loop(0, tm // SUBLANES, slab, 0)

    npt = tok.n_prompt // tm
    pl.when(i < npt)(lambda: run(yp_ref))
    pl.when(i >= npt)(lambda: run(ysm_ref))


def _combine(dest_flat, ys, x1, rw, mod, final_w, tok, tiles):
    d = x1.shape[1]
    g = mod.shape[0]
    tm = tiles.comb_tm
    npt = tok.n_prompt // tm
    nst = tok.n_sample // tm
    vmem = 2 * (tm * d * 4 + tm * LANES * 4 + g * d * 4 + 2 * tm * d * 4) + 2 * TOP_K * tm * d * 4
    return pl.pallas_call(
        functools.partial(_combine_kernel, tok=tok, tm=tm),
        grid_spec=pltpu.PrefetchScalarGridSpec(
            num_scalar_prefetch=1,
            grid=(tok.n // tm,),
            in_specs=[pl.BlockSpec(memory_space=pl.ANY),
                      pl.BlockSpec((tm, d), lambda i, dest: (i, 0)),
                      pl.BlockSpec((tm, LANES), lambda i, dest: (i, 0)),
                      pl.BlockSpec((g, d), lambda i, dest: (0, 5)),
                      pl.BlockSpec((1, d), lambda i, dest: (0, 0))],
            out_specs=[pl.BlockSpec((tm, d), lambda i, dest: (jnp.minimum(i, npt - 1), 0)),
                       pl.BlockSpec((tm, d), lambda i, dest: (jnp.clip(i - npt, 0, nst - 1), 0))],
            scratch_shapes=[pltpu.VMEM((2, TOP_K, tm, d), F32), pltpu.SemaphoreType.DMA((2, TOP_K))]),
        out_shape=[jax.ShapeDtypeStruct((tok.n_prompt, d), F32),
                   jax.ShapeDtypeStruct((tok.n_sample, d), F32)],
        compiler_params=_cparams(("arbitrary",), vmem),
        name="combine",
    )(dest_flat, ys, x1, rw, mod, final_w.reshape(1, d))


def _gate_layouts(gates, n_rows, row_off, L, n_gate):
    g = gates[row_off:row_off + n_rows, :n_gate]
    return g.reshape(n_rows // L, L, n_gate).transpose(0, 2, 1)


def _layer(xp, xs, c_all, pool_s, c_s, n_s, m_s, lw, tok, tiles):
    (w_ada, b_ada, norm_mix_w, w_in, b_igate, b_fgate, w_pool, pool_scale, mlstm_norm_w, w_out, norm_ffn_w,
     w_router_group, w_router_expert, w_exp_gate, w_exp_up, w_exp_down) = lw
    d = xp.shape[1]
    nheads, dk, dv = c_s.shape[1:]
    pool_in = pool_s.shape[2]
    mqk, mv = nheads * dk, nheads * dv
    offs = (pool_in, pool_in + mqk, pool_in + 2 * mqk, pool_in + 2 * mqk + mv)
    ga_off = offs[3] + mv
    gb_off = ga_off + d
    n_main = gb_off + d
    n_gate = 2 * nheads
    n = tok.n

    mod = _ada(c_all, w_ada, b_ada, tiles)
    w_in_t = w_in.T
    u, gates = _prenorm(xp, xs, mod, norm_mix_w, w_in_t, n_main, n_gate, tok, tiles)
    z = _matmul(u, w_in_t, n_main, True, tiles)

    zeros_pool = jnp.zeros((tok.nb_p, POOL_BUF, pool_in), F32)
    d_p, pool_new_p = _pool(z, zeros_pool, 0, tok.nb_p, tok.rows_p, 1, tiles.pool_tt, 0)
    d_s, pool_new_s = _pool(z, pool_s, tok.n_prompt, tok.nb_s, tok.rows_s, tiles.pool_bt, tok.rows_s, PAST_LEN)

    bias = jnp.concatenate([b_igate, b_fgate])
    bias_row = jnp.pad(bias, (0, LANES - n_gate)).reshape(1, LANES)
    bias_col = bias.reshape(n_gate, 1)
    nw = mlstm_norm_w.reshape(1, mv)
    lp = tiles.mlstm_l
    zc = jnp.zeros((tok.nb_p, nheads, dk, dv), F32)
    zn = jnp.zeros((tok.nb_p, nheads, 1, dk), F32)
    zm = jnp.zeros((tok.nb_p, nheads, 1, 1), F32)
    mo_p, c_p, n_p, m_p = _mlstm(z, gates, _gate_layouts(gates, tok.n_prompt, 0, lp, n_gate), bias_row, bias_col,
                                 nw, zc, zn, zm, 0, tok.nb_p, tok.rows_p, lp, 1, tiles.mlstm_hb_p, offs)
    ls = tok.rows_s
    mo_s, c_n, n_n, m_n = _mlstm(z, gates, _gate_layouts(gates, tok.n_sample, tok.n_prompt, ls, n_gate), bias_row,
                                 bias_col, nw, c_s, n_s.reshape(tok.nb_s, nheads, 1, dk),
                                 m_s.reshape(tok.nb_s, nheads, 1, 1), tok.n_prompt, tok.nb_s, ls, ls,
                                 tiles.mlstm_bb_s, tiles.mlstm_hb_s, offs)

    merged = _merge(d_p, d_s, w_pool.astype(BF16), pool_scale, z, mo_p, mo_s, ga_off, gb_off, tok, tiles)
    acc = _matmul(merged, w_out, d, False, tiles)

    w_router = jnp.pad(jnp.concatenate([w_router_group, w_router_expert], axis=1),
                       ((0, 0), (0, LANES - N_GROUPS - N_EXPERTS))).astype(BF16)
    x1, u2p, re, rw = _ffn_norm(xp, xs, acc, mod, norm_ffn_w, w_router, tok, tiles)
    rank_out, cnt_out = _rank(re, tiles)
    d_expert = w_exp_gate.shape[2]
    n_blocks = -(-(n * TOP_K) // tiles.moe_bm) + N_EXPERTS
    plan = _dispatch_plan(re, rank_out, cnt_out, n, n_blocks, d_expert, d, tiles)
    inv = _invert(plan["dest"], n_blocks * tiles.moe_bm)
    xsort = _gather(inv, plan["n_used"], plan["n_valid"], u2p, tiles)
    hdn = _moe_up(plan, xsort, w_exp_gate, w_exp_up, tiles)
    ysort = _moe_down(plan, hdn, w_exp_down, tiles)
    states = (pool_new_p, c_p, n_p.reshape(tok.nb_p, nheads, dk), m_p.reshape(tok.nb_p, nheads),
              pool_new_s, c_n, n_n.reshape(tok.nb_s, nheads, dk), m_n.reshape(tok.nb_s, nheads))
    return (plan["dest"], ysort, x1, rw, mod), states


def _forward(x_prompt, x_sample, c_prompt, c_sample, state_pool, state_mlstm_C, state_mlstm_n, state_mlstm_m,
             w_ada, b_ada, norm_mix_w, w_in, b_igate, b_fgate, w_pool, pool_scale, mlstm_norm_w, w_out,
             norm_ffn_w, w_router_group, w_router_expert, w_exp_gate, w_exp_up, w_exp_down, final_norm_w,
             tiles=Tiles()):
    nb_p, rows_p, d = x_prompt.shape
    nb_s, rows_s, _ = x_sample.shape
    depth = w_ada.shape[0]
    assert depth == 1, "the merged-token pipeline is written for a single layer"
    tok = Tok(nb_p * rows_p, nb_s * rows_s, rows_p, rows_s, nb_p, nb_s)
    for tm in (tiles.row_tm, tiles.mm_tm, tiles.merge_tm, tiles.rank_tr, tiles.comb_tm):
        assert tok.n_prompt % tm == 0 and tok.n_sample % tm == 0
    xp = x_prompt.reshape(tok.n_prompt, d)
    xs = x_sample.reshape(tok.n_sample, d)
    g = nb_p + nb_s
    g_pad = -(-g // SUBLANES) * SUBLANES
    c_all = jnp.pad(jnp.concatenate([c_prompt, c_sample], axis=0), ((0, g_pad - g), (0, 0)))
    lw = (w_ada[0], b_ada[0], norm_mix_w[0], w_in[0], b_igate[0], b_fgate[0], w_pool[0], pool_scale[0],
          mlstm_norm_w[0], w_out[0], norm_ffn_w[0], w_router_group[0], w_router_expert[0], w_exp_gate[0],
          w_exp_up[0], w_exp_down[0])
    (dest, ysort, x1, rw, mod), st = _layer(xp, xs, c_all, state_pool[0], state_mlstm_C[0], state_mlstm_n[0],
                                            state_mlstm_m[0], lw, tok, tiles)
    y_p, y_s = _combine(dest, ysort, x1, rw, mod, final_norm_w, tok, tiles)
    return (y_p.reshape(x_prompt.shape), y_s.reshape(x_sample.shape)) + tuple(s[None] for s in st)


def kernel(x_prompt, x_sample, c_prompt, c_sample, state_pool, state_mlstm_C, state_mlstm_n, state_mlstm_m,
           w_ada, b_ada, norm_mix_w, w_in, b_igate, b_fgate, w_pool, pool_scale, mlstm_norm_w, w_out,
           norm_ffn_w, w_router_group, w_router_expert, w_exp_gate, w_exp_up, w_exp_down, final_norm_w):
    return _forward(x_prompt, x_sample, c_prompt, c_sample, state_pool, state_mlstm_C, state_mlstm_n,
                    state_mlstm_m, w_ada, b_ada, norm_mix_w, w_in, b_igate, b_fgate, w_pool, pool_scale,
                    mlstm_norm_w, w_out, norm_ffn_w, w_router_group, w_router_expert, w_exp_gate, w_exp_up,
                    w_exp_down, final_norm_w)
```

```python
import functools
from typing import NamedTuple

import jax
import jax.numpy as jnp
from jax import lax
from jax.experimental import pallas as pl
from jax.experimental.pallas import tpu as pltpu

F32, BF16, I32, U32 = jnp.float32, jnp.bfloat16, jnp.int32, jnp.uint32

EPS = 1e-6
PAST_LEN = 16384
POOL_WINDOWS = (2, 4, 8, 16)
POOL_BUF = 15
N_GROUPS = 8
EXP_PER_GROUP = 8
N_EXPERTS = N_GROUPS * EXP_PER_GROUP
EXP_SHIFT = EXP_PER_GROUP.bit_length() - 1
TOP_K = 2

V7X_VMEM_BYTES = 64 * 2**20
LANES = 128
SUBLANES = 8
BF16_ROWS = 16
MIB = 2**20


class Tok(NamedTuple):
    n_prompt: int
    n_sample: int
    rows_p: int
    rows_s: int
    nb_p: int
    nb_s: int

    @property
    def n(self):
        return self.n_prompt + self.n_sample


class Tiles(NamedTuple):
    ada_tn: int = 512
    row_tm: int = 256
    mm_tm: int = 1024
    mm_tn: int = 1024
    pool_tt: int = 256
    pool_bt: int = 16
    mlstm_l: int = 256
    mlstm_hb_p: int = 4
    mlstm_hb_s: int = 4
    mlstm_bb_s: int = 4
    merge_tm: int = 512
    rank_tr: int = 256
    moe_bm: int = 256
    moe_cw: int = 512
    moe_dcw: int = 4096
    comb_tm: int = 256


def _cparams(sem, vmem_bytes):
    limit = int(min(max(vmem_bytes * 5 // 4 + 2 * MIB, 16 * MIB), V7X_VMEM_BYTES - 6 * MIB))
    return pltpu.CompilerParams(dimension_semantics=sem, vmem_limit_bytes=limit)


def _batch_of(row, tok):
    return jnp.where(row < tok.n_prompt, row // tok.rows_p,
                     tok.nb_p + (row - tok.n_prompt) // tok.rows_s)


def _mod_rows(ref, row0, tok, d):
    top = jnp.broadcast_to(ref[pl.ds(_batch_of(row0, tok), 1), :], (SUBLANES, d))
    bot = jnp.broadcast_to(ref[pl.ds(_batch_of(row0 + SUBLANES, tok), 1), :], (SUBLANES, d))
    return jnp.concatenate([top, bot], axis=0)


def _rms(x, w):
    r = lax.rsqrt(jnp.mean(x * x, axis=-1, keepdims=True) + EPS)
    return (x * r) * w


def _log_sigmoid(x):
    return jnp.minimum(x, 0.0) - jnp.log1p(jnp.exp(-jnp.abs(x)))


def _ada_kernel(c_ref, w_ref, b_ref, o_ref):
    c = c_ref[...]
    a = (c * jax.nn.sigmoid(c)).astype(BF16)
    o_ref[...] = jnp.dot(a, w_ref[...].astype(BF16), preferred_element_type=F32) + b_ref[...]


def _ada(c_all, w_ada, b_ada, tiles):
    g, d = c_all.shape
    n = w_ada.shape[1]
    tn = tiles.ada_tn
    vmem = 2 * (g * d * 4 + d * tn * 4 + g * tn * 4) + d * tn * 2
    return pl.pallas_call(
        _ada_kernel,
        grid=(n // tn,),
        in_specs=[pl.BlockSpec((g, d), lambda j: (0, 0)),
                  pl.BlockSpec((d, tn), lambda j: (0, j)),
                  pl.BlockSpec((1, tn), lambda j: (0, j))],
        out_specs=pl.BlockSpec((g, tn), lambda j: (0, j)),
        out_shape=jax.ShapeDtypeStruct((g, n), F32),
        compiler_params=_cparams(("arbitrary",), vmem),
        name="ada",
    )(c_all, w_ada, b_ada.reshape(1, n))


def _dual_specs(tok, tm, d):
    npt = tok.n_prompt // tm
    nst = tok.n_sample // tm
    return (pl.BlockSpec((tm, d), lambda i: (jnp.minimum(i, npt - 1), 0)),
            pl.BlockSpec((tm, d), lambda i: (jnp.clip(i - npt, 0, nst - 1), 0)))


def _prenorm_kernel(xp_ref, xs_ref, sc_ref, sh_ref, nw_ref, wg_ref, u_ref, gates_ref, wgb_ref, *, tok, tm, n_gate):
    i = pl.program_id(0)
    d = u_ref.shape[1]

    @pl.when(i == 0)
    def _():
        wgb_ref[0:n_gate, :] = wg_ref[...].astype(BF16)
        wgb_ref[n_gate:, :] = jnp.zeros((LANES - n_gate, d), BF16)

    def run(x_ref):
        def slab(s, carry):
            r = pl.multiple_of(s * BF16_ROWS, BF16_ROWS)
            row0 = i * tm + r
            xn = _rms(x_ref[pl.ds(r, BF16_ROWS), :], nw_ref[...])
            u = xn * (1.0 + _mod_rows(sc_ref, row0, tok, d)) + _mod_rows(sh_ref, row0, tok, d)
            u_ref[pl.ds(r, BF16_ROWS), :] = u.astype(BF16)
            return carry
        lax.fori_loop(0, tm // BF16_ROWS, slab, 0, unroll=2)

    npt = tok.n_prompt // tm
    pl.when(i < npt)(lambda: run(xp_ref))
    pl.when(i >= npt)(lambda: run(xs_ref))
    gates_ref[...] = lax.dot_general(u_ref[...], wgb_ref[...], (((1,), (1,)), ((), ())),
                                     preferred_element_type=F32)


def _prenorm(xp, xs, mod, norm_w, w_in_t, n_main, n_gate, tok, tiles):
    d = xp.shape[1]
    g = mod.shape[0]
    tm = tiles.row_tm
    assert n_main % n_gate == 0 and n_gate % BF16_ROWS == 0 and n_gate <= LANES
    xp_spec, xs_spec = _dual_specs(tok, tm, d)
    vmem = 2 * (2 * tm * d * 4 + 2 * g * d * 4 + n_gate * d * 4 + tm * d * 2 + tm * LANES * 4) + d * LANES * 2
    return pl.pallas_call(
        functools.partial(_prenorm_kernel, tok=tok, tm=tm, n_gate=n_gate),
        grid=(tok.n // tm,),
        in_specs=[xp_spec, xs_spec,
                  pl.BlockSpec((g, d), lambda i: (0, 1)),
                  pl.BlockSpec((g, d), lambda i: (0, 0)),
                  pl.BlockSpec((1, d), lambda i: (0, 0)),
                  pl.BlockSpec((n_gate, d), lambda i: (n_main // n_gate, 0))],
        out_specs=[pl.BlockSpec((tm, d), lambda i: (i, 0)),
                   pl.BlockSpec((tm, LANES), lambda i: (i, 0))],
        out_shape=[jax.ShapeDtypeStruct((tok.n, d), BF16),
                   jax.ShapeDtypeStruct((tok.n, LANES), F32)],
        scratch_shapes=[pltpu.VMEM((LANES, d), BF16)],
        compiler_params=_cparams(("arbitrary",), vmem),
        name="prenorm",
    )(xp, xs, mod, mod, norm_w.reshape(1, d), w_in_t)


def _matmul_kernel(a_ref, w_hbm, o_ref, stage, wb_ref, sem, *, w_is_nk, tn):
    j = pl.program_id(0)
    i = pl.program_id(1)

    def tile_copy(jj):
        cols = pl.ds(pl.multiple_of(jj * tn, tn), tn)
        src = w_hbm.at[cols, :] if w_is_nk else w_hbm.at[:, cols]
        return pltpu.make_async_copy(src, stage, sem.at[0])

    @pl.when((j == 0) & (i == 0))
    def _():
        tile_copy(0).start()

    @pl.when(i == 0)
    def _():
        tile_copy(j).wait()
        wb_ref[...] = stage[...].astype(BF16)

        @pl.when(j + 1 < pl.num_programs(0))
        def _():
            tile_copy(j + 1).start()

    contract_w = 1 if w_is_nk else 0
    o_ref[...] = lax.dot_general(a_ref[...], wb_ref[...], (((1,), (contract_w,)), ((), ())),
                                 preferred_element_type=F32)


def _matmul(a, w, n_cols, w_is_nk, tiles):
    m, k = a.shape
    tm, tn = tiles.mm_tm, tiles.mm_tn
    w_block = (tn, k) if w_is_nk else (k, tn)
    vmem = 2 * (tm * k * 2 + tm * tn * 4) + k * tn * 4 + k * tn * 2
    return pl.pallas_call(
        functools.partial(_matmul_kernel, w_is_nk=w_is_nk, tn=tn),
        grid=(n_cols // tn, m // tm),
        in_specs=[pl.BlockSpec((tm, k), lambda j, i: (i, 0)),
                  pl.BlockSpec(memory_space=pl.ANY)],
        out_specs=pl.BlockSpec((tm, tn), lambda j, i: (i, j)),
        out_shape=jax.ShapeDtypeStruct((m, n_cols), F32),
        scratch_shapes=[pltpu.VMEM(w_block, F32), pltpu.VMEM(w_block, BF16), pltpu.SemaphoreType.DMA((1,))],
        compiler_params=_cparams(("arbitrary", "arbitrary"), vmem),
        name="matmul",
    )(a, w)


def _pool_kernel(p_ref, st_ref, d_ref, new_ref, ext_ref, *, bt, tt, gin, pos0, nt):
    t = pl.program_id(1)
    cdim = ext_ref.shape[2]
    halo = POOL_BUF + 1

    @pl.when(t == 0)
    def _():
        ext_ref[:, 1:halo, :] = st_ref[...]

    @pl.when(t > 0)
    def _():
        ext_ref[:, 0:halo, :] = ext_ref[:, tt:tt + halo, :]

    ext_ref[:, halo:halo + tt, :] = p_ref[...].reshape(bt, tt, cdim)
    pos = pos0 + t * tt + lax.broadcasted_iota(I32, (bt, tt, gin), 1)
    for g, w in enumerate(POOL_WINDOWS):
        cs = slice(g * gin, (g + 1) * gin)
        cur = ext_ref[:, halo:halo + tt, cs]
        acc = cur
        for j in range(1, w):
            acc = acc + ext_ref[:, halo - j:halo - j + tt, cs]
        cnt = jnp.minimum(pos + 1, w).astype(F32)
        d_ref[:, cs] = (acc / cnt - cur).reshape(bt * tt, gin)

    @pl.when(t == nt - 1)
    def _():
        new_ref[...] = ext_ref[:, tt + 1:tt + halo, :]


def _pool(z, state, row_off, nb, t_len, bt, tt, pos0):
    cdim = state.shape[2]
    gin = cdim // len(POOL_WINDOWS)
    nt = t_len // tt
    rows = bt * tt
    rb0 = row_off // rows
    vmem = 2 * (2 * rows * cdim * 4 + 2 * bt * 16 * cdim * 4) + bt * (tt + 16) * cdim * 4 + 8 * rows * gin * 4
    return pl.pallas_call(
        functools.partial(_pool_kernel, bt=bt, tt=tt, gin=gin, pos0=pos0, nt=nt),
        grid=(nb // bt, nt),
        in_specs=[pl.BlockSpec((rows, cdim), lambda b, t: (rb0 + b * nt + t, 0)),
                  pl.BlockSpec((bt, POOL_BUF, cdim), lambda b, t: (b, 0, 0))],
        out_specs=[pl.BlockSpec((rows, cdim), lambda b, t: (b * nt + t, 0)),
                   pl.BlockSpec((bt, POOL_BUF, cdim), lambda b, t: (b, 0, 0))],
        out_shape=[jax.ShapeDtypeStruct((nb * t_len, cdim), F32),
                   jax.ShapeDtypeStruct((nb, POOL_BUF, cdim), F32)],
        scratch_shapes=[pltpu.VMEM((bt, tt + POOL_BUF + 1, cdim), F32)],
        compiler_params=_cparams(("arbitrary", "arbitrary"), vmem),
        name="pool",
    )(z, state)


def _mlstm_kernel(q_ref, k_ref, v_ref, o_ref, gc_ref, gr_ref, brow_ref, bcol_ref, nw_ref, c0_ref, n0_ref, m0_ref,
                  mo_ref, c_out, n_out, m_out, c_s, n_s, m_s, *, L, bb, hb, dk, dv, nc, nheads):
    hblk = pl.program_id(1)
    c = pl.program_id(2)

    @pl.when(c == 0)
    def _():
        c_s[...] = c0_ref[...]
        n_s[...] = n0_ref[...]
        m_s[...] = m0_ref[...]

    lane = lax.broadcasted_iota(I32, (L, LANES), 1)
    sub = lax.broadcasted_iota(I32, (2 * nheads, L), 0)
    rowi = lax.broadcasted_iota(I32, (L, L), 0)
    coli = lax.broadcasted_iota(I32, (L, L), 1)
    tri = rowi >= coli
    scale = dk ** -0.5

    units = [(bi, hh) for bi in range(bb) for hh in range(hb)]
    heads = range(len(units))
    rsl = [slice(bi * L, (bi + 1) * L) for bi, _ in units]
    gcb = [gc_ref[rsl[i], :] + brow_ref[...] for i in heads]
    grb = [gr_ref[units[i][0]] + bcol_ref[...] for i in heads]
    hs_dyn = [hblk * hb + hh for _, hh in units]
    li_col = [jnp.sum(jnp.where(lane == hs_dyn[i], gcb[i], 0.0), axis=1, keepdims=True) for i in heads]
    lf_col = [jnp.sum(jnp.where(lane == hs_dyn[i] + nheads, gcb[i], 0.0), axis=1, keepdims=True) for i in heads]
    li_row = [jnp.sum(jnp.where(sub == hs_dyn[i], grb[i], 0.0), axis=0, keepdims=True) for i in heads]
    lf_row = [jnp.sum(jnp.where(sub == hs_dyn[i] + nheads, grb[i], 0.0), axis=0, keepdims=True) for i in heads]
    lf_col = [_log_sigmoid(x) for x in lf_col]
    lf_row = [_log_sigmoid(x) for x in lf_row]
    b_col = [jnp.sum(jnp.where(tri, x, 0.0), axis=1, keepdims=True) for x in lf_row]
    b_row = [jnp.sum(jnp.where(rowi <= coli, x, 0.0), axis=0, keepdims=True) for x in lf_col]

    q = [q_ref[rsl[i], hh * dk:(hh + 1) * dk] * scale for i, (_, hh) in enumerate(units)]
    k = [k_ref[rsl[i], hh * dk:(hh + 1) * dk] for i, (_, hh) in enumerate(units)]
    qb = [x.astype(BF16) for x in q]
    kb = [x.astype(BF16) for x in k]
    vb = [v_ref[rsl[i], hh * dv:(hh + 1) * dv].astype(BF16) for i, (_, hh) in enumerate(units)]
    cmat = [c_s[bi, hh] for bi, hh in units]
    nvec = [n_s[bi, hh] for bi, hh in units]
    m_prev = [m_s[bi, hh] for bi, hh in units]

    dmat = [jnp.where(tri, b_col[i] - b_row[i] + li_row[i], -jnp.inf) for i in heads]
    inter = [b_col[i] + m_prev[i] for i in heads]
    m_t = [jnp.maximum(inter[i], jnp.max(dmat[i], axis=1, keepdims=True)) for i in heads]
    dw = [jnp.exp(dmat[i] - m_t[i]) for i in heads]
    iw = [jnp.exp(inter[i] - m_t[i]) for i in heads]
    qk = [lax.dot_general(qb[i], kb[i], (((1,), (1,)), ((), ())), preferred_element_type=F32) for i in heads]
    qc = [jnp.dot(qb[i], cmat[i].astype(BF16), preferred_element_type=F32) for i in heads]
    s = [qk[i] * dw[i] for i in heads]
    num = [jnp.dot(s[i].astype(BF16), vb[i], preferred_element_type=F32) + iw[i] * qc[i] for i in heads]
    den = [jnp.sum(s[i], axis=1, keepdims=True) + iw[i] * jnp.sum(q[i] * nvec[i], axis=1, keepdims=True)
           for i in heads]
    hval = [num[i] / jnp.maximum(jnp.abs(den[i]), jnp.exp(-m_t[i])) for i in heads]
    for i, (_, hh) in enumerate(units):
        hs = slice(hh * dv, (hh + 1) * dv)
        mo_ref[rsl[i], hs] = jax.nn.sigmoid(o_ref[rsl[i], hs]) * _rms(hval[i], nw_ref[:, hs])

    b_last = [x[L - 1:L, :] for x in b_col]
    dl_col = [b_last[i] - b_col[i] + li_col[i] for i in heads]
    dl_row = [b_last[i] - b_row[i] + li_row[i] for i in heads]
    m_new = [jnp.maximum(b_last[i] + m_prev[i], jnp.max(dl_row[i], axis=1, keepdims=True)) for i in heads]
    dec = [jnp.exp(b_last[i] + m_prev[i] - m_new[i]) for i in heads]
    kk = [jnp.exp(dl_col[i] - m_new[i]) * k[i] for i in heads]
    upd = [lax.dot_general(kk[i].astype(BF16), vb[i], (((0,), (0,)), ((), ())), preferred_element_type=F32)
           for i in heads]
    for i, (bi, hh) in enumerate(units):
        c_s[bi, hh] = dec[i] * cmat[i] + upd[i]
        n_s[bi, hh] = dec[i] * nvec[i] + jnp.sum(kk[i], axis=0, keepdims=True)
        m_s[bi, hh] = m_new[i]

    @pl.when(c == nc - 1)
    def _():
        c_out[...] = c_s[...]
        n_out[...] = n_s[...]
        m_out[...] = m_s[...]


def _mlstm(z, gates, gates_t, bias_row, bias_col, norm_w, c0, n0, m0, row_off, nb, t_len, L, bb, hb, offs):
    _, nheads, dk, dv = c0.shape
    nc = t_len // L
    assert bb == 1 or nc == 1
    rows = bb * L
    rb0 = row_off // rows
    nhb = nheads // hb
    q_off, k_off, v_off, o_off = offs
    wq, wv = hb * dk, hb * dv

    def rowblk(b, c):
        return rb0 + b * nc + c

    in_specs = [
        pl.BlockSpec((rows, wq), lambda b, h, c: (rowblk(b, c), q_off // wq + h)),
        pl.BlockSpec((rows, wq), lambda b, h, c: (rowblk(b, c), k_off // wq + h)),
        pl.BlockSpec((rows, wv), lambda b, h, c: (rowblk(b, c), v_off // wv + h)),
        pl.BlockSpec((rows, wv), lambda b, h, c: (rowblk(b, c), o_off // wv + h)),
        pl.BlockSpec((rows, LANES), lambda b, h, c: (rowblk(b, c), 0)),
        pl.BlockSpec((bb, 2 * nheads, L), lambda b, h, c: (b * nc + c, 0, 0)),
        pl.BlockSpec((1, LANES), lambda b, h, c: (0, 0)),
        pl.BlockSpec((2 * nheads, 1), lambda b, h, c: (0, 0)),
        pl.BlockSpec((1, wv), lambda b, h, c: (0, h)),
        pl.BlockSpec((bb, hb, dk, dv), lambda b, h, c: (b, h, 0, 0)),
        pl.BlockSpec((bb, hb, 1, dk), lambda b, h, c: (b, h, 0, 0)),
        pl.BlockSpec((bb, hb, 1, 1), lambda b, h, c: (b, h, 0, 0)),
    ]
    state_bytes = bb * hb * dk * dv * 4
    vmem = (2 * (2 * rows * wq * 4 + 3 * rows * wv * 4 + 2 * state_bytes) + state_bytes
            + 24 * bb * L * max(L, dv) * 4)
    return pl.pallas_call(
        functools.partial(_mlstm_kernel, L=L, bb=bb, hb=hb, dk=dk, dv=dv, nc=nc, nheads=nheads),
        grid=(nb // bb, nhb, nc),
        in_specs=in_specs,
        out_specs=[pl.BlockSpec((rows, wv), lambda b, h, c: (b * nc + c, h)),
                   pl.BlockSpec((bb, hb, dk, dv), lambda b, h, c: (b, h, 0, 0)),
                   pl.BlockSpec((bb, hb, 1, dk), lambda b, h, c: (b, h, 0, 0)),
                   pl.BlockSpec((bb, hb, 1, 1), lambda b, h, c: (b, h, 0, 0))],
        out_shape=[jax.ShapeDtypeStruct((nb * t_len, nheads * dv), F32),
                   jax.ShapeDtypeStruct(c0.shape, F32),
                   jax.ShapeDtypeStruct(n0.shape, F32),
                   jax.ShapeDtypeStruct(m0.shape, F32)],
        scratch_shapes=[pltpu.VMEM((bb, hb, dk, dv), F32), pltpu.VMEM((bb, hb, 1, dk), F32),
                        pltpu.VMEM((bb, hb, 1, 1), F32)],
        compiler_params=_cparams(("arbitrary", "arbitrary", "arbitrary"), vmem),
        name="mlstm",
    )(z, z, z, z, gates, gates_t, bias_row, bias_col, norm_w, c0, n0, m0)


def _merge_kernel(dp_ref, ds_ref, wp_ref, ps_ref, ga_ref, gb_ref, mop_ref, mos_ref, o_ref, *, gin, gout, npt):
    def run(d_ref, mo_ref):
        for gg in range(wp_ref.shape[0]):
            y = jnp.dot(d_ref[:, gg * gin:(gg + 1) * gin].astype(BF16), wp_ref[gg], preferred_element_type=F32)
            cs = slice(gg * gout, (gg + 1) * gout)
            pool_out = y * ps_ref[:, cs]
            merged = jax.nn.sigmoid(ga_ref[:, cs]) * pool_out + jax.nn.sigmoid(gb_ref[:, cs]) * mo_ref[:, cs]
            o_ref[:, cs] = merged.astype(BF16)

    i = pl.program_id(0)
    pl.when(i < npt)(lambda: run(dp_ref, mop_ref))
    pl.when(i >= npt)(lambda: run(ds_ref, mos_ref))


def _merge(d_p, d_s, w_pool_b, pool_scale, z, mo_p, mo_s, ga_off, gb_off, tok, tiles):
    cdim = d_p.shape[1]
    ng, gin, gout = w_pool_b.shape
    dm = ng * gout
    half = dm // 2
    gh = ng // 2
    tm = tiles.merge_tm
    npt = tok.n_prompt // tm
    nst = tok.n_sample // tm

    def prompt_spec(width):
        return pl.BlockSpec((tm, width), lambda i, j: (jnp.minimum(i, npt - 1), jnp.where(i < npt, j, 1)))

    def sample_spec(width):
        return pl.BlockSpec((tm, width), lambda i, j: (jnp.clip(i - npt, 0, nst - 1), jnp.where(i < npt, 0, j)))

    vmem = 2 * (2 * tm * cdim * 2 + gh * gin * gout * 2 + 4 * tm * half * 4 + tm * half * 2) + 6 * tm * gout * 4
    return pl.pallas_call(
        functools.partial(_merge_kernel, gin=gin, gout=gout, npt=npt),
        grid=(tok.n // tm, 2),
        in_specs=[prompt_spec(cdim // 2), sample_spec(cdim // 2),
                  pl.BlockSpec((gh, gin, gout), lambda i, j: (j, 0, 0)),
                  pl.BlockSpec((1, half), lambda i, j: (0, j)),
                  pl.BlockSpec((tm, half), lambda i, j: (i, ga_off // half + j)),
                  pl.BlockSpec((tm, half), lambda i, j: (i, gb_off // half + j)),
                  prompt_spec(half), sample_spec(half)],
        out_specs=pl.BlockSpec((tm, half), lambda i, j: (i, j)),
        out_shape=jax.ShapeDtypeStruct((tok.n, dm), BF16),
        compiler_params=_cparams(("arbitrary", "arbitrary"), vmem),
        name="merge",
    )(d_p, d_s, w_pool_b, pool_scale.reshape(1, dm), z, z, mo_p, mo_s)


def _ffn_norm_kernel(xp_ref, xs_ref, acc_ref, g1_ref, sc_ref, sh_ref, nw_ref, wr_ref,
                     x1_ref, up_ref, re_ref, rw_ref, ub_ref, *, tok, tm):
    i = pl.program_id(0)
    d = x1_ref.shape[1]
    hd = d // 2

    def run(x_ref):
        def slab(s, carry):
            r = pl.multiple_of(s * BF16_ROWS, BF16_ROWS)
            rows = pl.ds(r, BF16_ROWS)
            row0 = i * tm + r
            x1 = x_ref[rows, :] + _mod_rows(g1_ref, row0, tok, d) * acc_ref[rows, :]
            x1_ref[rows, :] = x1
            u = _rms(x1, nw_ref[...]) * (1.0 + _mod_rows(sc_ref, row0, tok, d)) + _mod_rows(sh_ref, row0, tok, d)
            ub = u.astype(BF16)
            ub_ref[rows, :] = ub
            bits = lax.bitcast_convert_type(ub.astype(F32), U32)
            up_ref[rows, :] = (bits[:, hd:] & jnp.uint32(0xFFFF0000)) | (bits[:, :hd] >> 16)
            return carry
        lax.fori_loop(0, tm // BF16_ROWS, slab, 0)

    npt = tok.n_prompt // tm
    pl.when(i < npt)(lambda: run(xp_ref))
    pl.when(i >= npt)(lambda: run(xs_ref))

    lg = jnp.dot(ub_ref[...], wr_ref[...], preferred_element_type=F32)
    lane = lax.broadcasted_iota(I32, lg.shape, 1)
    lanef = lane.astype(F32)
    big = float(LANES)
    is_g = lane < N_GROUPS
    gl = jnp.where(is_g, lg, -jnp.inf)
    gmax = jnp.max(gl, axis=1, keepdims=True)
    grp = jnp.min(jnp.where(gl == gmax, lanef, big), axis=1, keepdims=True)
    p_grp = 1.0 / jnp.sum(jnp.where(is_g, jnp.exp(lg - gmax), 0.0), axis=1, keepdims=True)
    eidx = lane - N_GROUPS
    in_grp = (eidx >= 0) & (eidx < N_EXPERTS) & ((eidx >> EXP_SHIFT).astype(F32) == grp)
    el = jnp.where(in_grp, lg, -jnp.inf)
    v1 = jnp.max(el, axis=1, keepdims=True)
    i1 = jnp.min(jnp.where(el == v1, lanef, big), axis=1, keepdims=True)
    el2 = jnp.where(lanef == i1, -jnp.inf, el)
    v2 = jnp.max(el2, axis=1, keepdims=True)
    i2 = jnp.min(jnp.where(el2 == v2, lanef, big), axis=1, keepdims=True)
    e2 = jnp.exp(v2 - v1)
    w1 = (1.0 / (1.0 + e2)) * p_grp
    w2 = (e2 / (1.0 + e2)) * p_grp
    re_ref[...] = jnp.where(lane == 0, i1 - N_GROUPS, jnp.where(lane == 1, i2 - N_GROUPS, 0.0))
    rw_ref[...] = jnp.where(lane == 0, w1, jnp.where(lane == 1, w2, 0.0))


def _ffn_norm(xp, xs, acc, mod, norm_w, w_router, tok, tiles):
    d = xp.shape[1]
    g = mod.shape[0]
    tm = tiles.row_tm
    xp_spec, xs_spec = _dual_specs(tok, tm, d)
    vmem = (2 * (3 * tm * d * 4 + 3 * g * d * 4 + d * LANES * 2 + tm * d * 4 + tm * d * 2 + 2 * tm * LANES * 4)
            + tm * d * 2 + 16 * tm * LANES * 4)
    return pl.pallas_call(
        functools.partial(_ffn_norm_kernel, tok=tok, tm=tm),
        grid=(tok.n // tm,),
        in_specs=[xp_spec, xs_spec,
                  pl.BlockSpec((tm, d), lambda i: (i, 0)),
                  pl.BlockSpec((g, d), lambda i: (0, 2)),
                  pl.BlockSpec((g, d), lambda i: (0, 4)),
                  pl.BlockSpec((g, d), lambda i: (0, 3)),
                  pl.BlockSpec((1, d), lambda i: (0, 0)),
                  pl.BlockSpec((d, LANES), lambda i: (0, 0))],
        out_specs=[pl.BlockSpec((tm, d), lambda i: (i, 0)),
                   pl.BlockSpec((tm, d // 2), lambda i: (i, 0)),
                   pl.BlockSpec((tm, LANES), lambda i: (i, 0)),
                   pl.BlockSpec((tm, LANES), lambda i: (i, 0))],
        out_shape=[jax.ShapeDtypeStruct((tok.n, d), F32),
                   jax.ShapeDtypeStruct((tok.n, d // 2), U32),
                   jax.ShapeDtypeStruct((tok.n, LANES), F32),
                   jax.ShapeDtypeStruct((tok.n, LANES), F32)],
        scratch_shapes=[pltpu.VMEM((tm, d), BF16)],
        compiler_params=_cparams(("arbitrary",), vmem),
        name="ffn_norm",
    )(xp, xs, acc, mod, mod, mod, norm_w.reshape(1, d), w_router)


def _rank_kernel(re_ref, rank_ref, cnt_ref, carry_ref, *, tr):
    j = pl.program_id(0)
    i = pl.program_id(1)

    @pl.when((j == 0) & (i == 0))
    def _():
        carry_ref[...] = jnp.zeros_like(carry_ref)

    re = re_ref[...]
    lane = lax.broadcasted_iota(I32, re.shape, 1)
    e_col = jnp.sum(jnp.where(lane == j, re, 0.0), axis=1, keepdims=True)
    onehot = lane.astype(F32) == e_col
    rowi = lax.broadcasted_iota(I32, (tr, tr), 0)
    coli = lax.broadcasted_iota(I32, (tr, tr), 1)
    tri = jnp.where(rowi >= coli, 1.0, 0.0).astype(BF16)
    prefix = jnp.dot(tri, jnp.where(onehot, 1.0, 0.0).astype(BF16), preferred_element_type=F32)
    carry = carry_ref[0:1, :]
    rank = jnp.sum(jnp.where(onehot, prefix - 1.0 + carry, 0.0), axis=1, keepdims=True)
    rank_ref[...] = jnp.broadcast_to(rank, rank_ref.shape)
    new_carry = carry + prefix[tr - 1:tr, :]
    carry_ref[...] = jnp.broadcast_to(new_carry, carry_ref.shape)
    cnt_ref[...] = jnp.broadcast_to(new_carry, cnt_ref.shape)


def _rank(re, tiles):
    n = re.shape[0]
    tr = tiles.rank_tr
    nt = n // tr
    return pl.pallas_call(
        functools.partial(_rank_kernel, tr=tr),
        grid=(TOP_K, nt),
        in_specs=[pl.BlockSpec((tr, LANES), lambda j, i: (i, 0))],
        out_specs=[pl.BlockSpec((tr, LANES), lambda j, i: (j * nt + i, 0)),
                   pl.BlockSpec((SUBLANES, LANES), lambda j, i: (0, 0))],
        out_shape=[jax.ShapeDtypeStruct((TOP_K * n, LANES), F32),
                   jax.ShapeDtypeStruct((SUBLANES, LANES), F32)],
        scratch_shapes=[pltpu.VMEM((SUBLANES, LANES), F32)],
        compiler_params=_cparams(("arbitrary", "arbitrary"), 4 * MIB),
        name="rank",
    )(re)


def _invert_kernel(dest_ref, inv_ref, *, n_assign, n_rows):
    group = 8

    def put(g, carry):
        a0 = g * group
        dests = [dest_ref[a0 + u] for u in range(group)]
        for u in range(group):
            inv_ref[dests[u]] = lax.shift_right_logical(a0 + u, TOP_K.bit_length() - 1)
        return carry
    lax.fori_loop(0, n_assign // group, put, 0, unroll=2)


def _invert(dest_flat, n_rows):
    assert dest_flat.shape[0] % 8 == 0 and TOP_K & (TOP_K - 1) == 0
    return pl.pallas_call(
        functools.partial(_invert_kernel, n_assign=dest_flat.shape[0], n_rows=n_rows),
        in_specs=[pl.BlockSpec(memory_space=pltpu.SMEM)],
        out_specs=pl.BlockSpec(memory_space=pltpu.SMEM),
        out_shape=jax.ShapeDtypeStruct((n_rows,), I32),
        name="invert",
    )(dest_flat)


def _gather_kernel(nused_ref, nvalid_ref, inv_ref, src_ref, o_ref, buf, sems, *, bm):
    b = pl.program_id(0)
    n_used = nused_ref[0]

    def row_copy(t, r, slot):
        return pltpu.make_async_copy(src_ref.at[pl.ds(t, 1)], buf.at[slot, pl.ds(r, 1)], sems.at[slot])

    def issue(blk, slot):
        def body(r, carry):
            row_copy(inv_ref[blk * bm + r], r, slot).start()
            return carry
        lax.fori_loop(0, nvalid_ref[blk], body, 0)

    @pl.when(b == 0)
    def _():
        buf[...] = jnp.zeros_like(buf)
        issue(0, 0)

    @pl.when(b + 1 < n_used)
    def _():
        issue(b + 1, (b + 1) % 2)

    @pl.when(b < n_used)
    def _():
        slot = b % 2
        nv = nvalid_ref[b]

        def drain(r, carry):
            row_copy(0, r, slot).wait()
            return carry
        lax.fori_loop(0, nv, drain, 0)
        rows = lax.broadcasted_iota(I32, o_ref.shape, 0)
        o_ref[...] = jnp.where(rows < nv, buf[slot], jnp.zeros_like(o_ref))


def _gather(inv, n_used, n_valid, src, tiles):
    n_rows = inv.shape[0]
    width = src.shape[1]
    bm = tiles.moe_bm
    return pl.pallas_call(
        functools.partial(_gather_kernel, bm=bm),
        grid_spec=pltpu.PrefetchScalarGridSpec(
            num_scalar_prefetch=3,
            grid=(n_rows // bm,),
            in_specs=[pl.BlockSpec(memory_space=pl.ANY)],
            out_specs=pl.BlockSpec((bm, width), lambda b, nu, nv, inv: (jnp.minimum(b, nu[0] - 1), 0)),
            scratch_shapes=[pltpu.VMEM((2, bm, width), src.dtype), pltpu.SemaphoreType.DMA((2,))]),
        out_shape=jax.ShapeDtypeStruct((n_rows, width), src.dtype),
        compiler_params=_cparams(("arbitrary",), 4 * bm * width * 4),
        name="gather",
    )(n_used, n_valid, inv, src)


def _unpack_pair(words):
    lo = lax.bitcast_convert_type(words << 16, F32).astype(BF16)
    hi = lax.bitcast_convert_type(words & jnp.uint32(0xFFFF0000), F32).astype(BF16)
    return lo, hi


def _expert_weights(w, plan_refs, w_hbm, stage, sems, cw):
    _, col_ref, exp_ref, first_ref, _, slot_ref, nxt_e_ref, nxt_c_ref = plan_refs

    def copies(e, c, slot):
        cols = pl.ds(pl.multiple_of(c * cw, cw), cw)
        return [pltpu.make_async_copy(w_hbm[k].at[e, :, cols], stage.at[slot, k], sems.at[slot, k])
                for k in range(len(w_hbm))]

    @pl.when(w == 0)
    def _():
        for cp in copies(exp_ref[0], col_ref[0], 0):
            cp.start()

    @pl.when(first_ref[w] == 1)
    def _():
        for cp in copies(exp_ref[w], col_ref[w], slot_ref[w]):
            cp.wait()

        @pl.when(nxt_e_ref[w] >= 0)
        def _():
            for cp in copies(nxt_e_ref[w], nxt_c_ref[w], 1 - slot_ref[w]):
                cp.start()


def _moe_up_kernel(*refs, cw):
    plan_refs = refs[:8]
    xs_ref, wg_hbm, wu_hbm, h_ref, stage, wgb, wub, sems = refs[8:]
    first_ref, valid_ref, slot_ref = plan_refs[3], plan_refs[4], plan_refs[5]
    w = pl.program_id(0)
    hd = xs_ref.shape[1]
    _expert_weights(w, plan_refs, (wg_hbm, wu_hbm), stage, sems, cw)

    @pl.when(first_ref[w] == 1)
    def _():
        wgb[...] = stage[slot_ref[w], 0].astype(BF16)
        wub[...] = stage[slot_ref[w], 1].astype(BF16)

    @pl.when(valid_ref[w] == 1)
    def _():
        lo, hi = _unpack_pair(xs_ref[...])
        g = (jnp.dot(lo, wgb[:hd, :], preferred_element_type=F32)
             + jnp.dot(hi, wgb[hd:, :], preferred_element_type=F32))
        u = (jnp.dot(lo, wub[:hd, :], preferred_element_type=F32)
             + jnp.dot(hi, wub[hd:, :], preferred_element_type=F32))
        h_ref[...] = ((g * jax.nn.sigmoid(g)) * u).astype(BF16)


def _moe_up(plan, xs, w_gate, w_up, tiles):
    p_rows, hd = xs.shape
    _, d, de = w_gate.shape
    bm, cw = tiles.moe_bm, tiles.moe_cw
    n_items = plan["up"][0].shape[0]
    vmem = 2 * (bm * hd * 4 + bm * cw * 2) + 4 * d * cw * 4 + 2 * d * cw * 2 + 8 * bm * cw * 4 + 2 * bm * d * 2

    def by_item(f):
        return lambda w, blk, col, ex, fi, va, sl, ne, nc: f(w, blk, col)

    return pl.pallas_call(
        functools.partial(_moe_up_kernel, cw=cw),
        grid_spec=pltpu.PrefetchScalarGridSpec(
            num_scalar_prefetch=8,
            grid=(n_items,),
            in_specs=[pl.BlockSpec((bm, hd), by_item(lambda w, blk, col: (blk[w], 0))),
                      pl.BlockSpec(memory_space=pl.ANY),
                      pl.BlockSpec(memory_space=pl.ANY)],
            out_specs=pl.BlockSpec((bm, cw), by_item(lambda w, blk, col: (blk[w], col[w]))),
            scratch_shapes=[pltpu.VMEM((2, 2, d, cw), F32), pltpu.VMEM((d, cw), BF16), pltpu.VMEM((d, cw), BF16),
                            pltpu.SemaphoreType.DMA((2, 2))]),
        out_shape=jax.ShapeDtypeStruct((p_rows, de), BF16),
        compiler_params=_cparams(("arbitrary",), vmem),
        name="moe_up",
    )(*plan["up"], xs, w_gate, w_up)


def _moe_down_kernel(*refs, dcw):
    plan_refs = refs[:8]
    h_ref, wd_hbm, y_ref, stage, wdb, sems = refs[8:]
    first_ref, valid_ref, slot_ref = plan_refs[3], plan_refs[4], plan_refs[5]
    w = pl.program_id(0)
    _expert_weights(w, plan_refs, (wd_hbm,), stage, sems, dcw)

    @pl.when(first_ref[w] == 1)
    def _():
        wdb[...] = stage[slot_ref[w], 0].astype(BF16)

    @pl.when(valid_ref[w] == 1)
    def _():
        y_ref[...] = jnp.dot(h_ref[...], wdb[...], preferred_element_type=F32)


def _moe_down(plan, hdn, w_down, tiles):
    p_rows, de = hdn.shape
    d = w_down.shape[2]
    bm, dcw = tiles.moe_bm, tiles.moe_dcw
    n_items = plan["down"][0].shape[0]
    vmem = 2 * (bm * de * 2 + bm * dcw * 4) + 2 * de * dcw * 4 + de * dcw * 2 + 2 * bm * dcw * 4

    def by_item(f):
        return lambda w, blk, col, ex, fi, va, sl, ne, nc: f(w, blk, col)

    return pl.pallas_call(
        functools.partial(_moe_down_kernel, dcw=dcw),
        grid_spec=pltpu.PrefetchScalarGridSpec(
            num_scalar_prefetch=8,
            grid=(n_items,),
            in_specs=[pl.BlockSpec((bm, de), by_item(lambda w, blk, col: (blk[w], 0))),
                      pl.BlockSpec(memory_space=pl.ANY)],
            out_specs=pl.BlockSpec((bm, dcw), by_item(lambda w, blk, col: (blk[w], col[w]))),
            scratch_shapes=[pltpu.VMEM((2, 1, de, dcw), F32), pltpu.VMEM((de, dcw), BF16),
                            pltpu.SemaphoreType.DMA((2, 1))]),
        out_shape=jax.ShapeDtypeStruct((p_rows, d), F32),
        compiler_params=_cparams(("arbitrary",), vmem),
        name="moe_down",
    )(*plan["down"], hdn, w_down)


def _work_items(nb_e, blk0_e, n_cols, n_items):
    items_e = nb_e * n_cols
    cum = jnp.cumsum(items_e)
    total = cum[-1]
    w = jnp.minimum(jnp.arange(n_items, dtype=I32), total - 1)
    own = (w[:, None] >= (cum - items_e)[None, :]) & (w[:, None] < cum[None, :])

    def pick(v):
        return jnp.sum(jnp.where(own, v[None, :], 0), axis=1)

    e = pick(jnp.arange(N_EXPERTS, dtype=I32))
    r = w - pick(cum - items_e)
    nb = jnp.maximum(pick(nb_e), 1)
    col = r // nb
    blk = pick(blk0_e) + r % nb
    valid = (jnp.arange(n_items, dtype=I32) < total).astype(I32)
    key = e * n_cols + col
    first = jnp.concatenate([jnp.ones((1,), I32), (key[1:] != key[:-1]).astype(I32)]) * valid
    slot = (jnp.cumsum(first) - 1) % 2
    idx = jnp.arange(n_items, dtype=I32)
    nxt = lax.cummin(jnp.where(first == 1, idx, n_items)[::-1])[::-1]
    nxt = jnp.concatenate([nxt[1:], jnp.full((1,), n_items, I32)])
    has_next = nxt < n_items
    pick_next = nxt[:, None] == idx[None, :]
    nxt_e = jnp.where(has_next, jnp.sum(jnp.where(pick_next, e[None, :], 0), axis=1), -1)
    nxt_c = jnp.where(has_next, jnp.sum(jnp.where(pick_next, col[None, :], 0), axis=1), 0)
    return (blk.astype(I32), col.astype(I32), e.astype(I32), first.astype(I32), valid, slot.astype(I32),
            nxt_e.astype(I32), nxt_c.astype(I32))


def _rows_per_block(counts, nb_e, blk0_e, n_blocks, bm):
    b = jnp.arange(n_blocks, dtype=I32)[:, None]
    own = (b >= blk0_e[None, :]) & (b < (blk0_e + nb_e)[None, :])
    rows = jnp.clip(counts[None, :] - (b - blk0_e[None, :]) * bm, 0, bm)
    return jnp.sum(jnp.where(own, rows, 0), axis=1).astype(I32)


def _dispatch_plan(re, rank_out, cnt_out, n, n_blocks, d_expert, d_model, tiles):
    bm = tiles.moe_bm
    eid = re[:, :TOP_K].astype(I32)
    rank = rank_out[:, 0].reshape(TOP_K, n).T.astype(I32)
    counts = cnt_out[0, :N_EXPERTS].astype(I32)
    nb_e = (counts + bm - 1) // bm
    blk0_e = jnp.cumsum(nb_e) - nb_e
    row0 = jnp.sum(jnp.where(eid[:, :, None] == jnp.arange(N_EXPERTS, dtype=I32), blk0_e * bm, 0), axis=2)
    dest = row0 + rank
    plan = {
        "dest": dest.reshape(-1).astype(I32),
        "n_used": jnp.sum(nb_e).reshape(1).astype(I32),
        "n_valid": _rows_per_block(counts, nb_e, blk0_e, n_blocks, bm),
        "up": _work_items(nb_e, blk0_e, d_expert // tiles.moe_cw, n_blocks * (d_expert // tiles.moe_cw)),
        "down": _work_items(nb_e, blk0_e, d_model // tiles.moe_dcw, n_blocks * (d_model // tiles.moe_dcw)),
    }
    return plan


def _combine_kernel(dest_ref, ys_ref, x1_ref, rw_ref, g2_ref, fw_ref, yp_ref, ysm_ref, ybuf, sems, *, tok, tm):
    i = pl.program_id(0)
    d = x1_ref.shape[1]
    base = i * tm
    nt = pl.num_programs(0)

    def issue(tile, slot):
        def body(r, carry):
            for j in range(TOP_K):
                pltpu.make_async_copy(ys_ref.at[pl.ds(dest_ref[(tile * tm + r) * TOP_K + j], 1)],
                                      ybuf.at[slot, j, pl.ds(r, 1)], sems.at[slot, j]).start()
            return carry
        lax.fori_loop(0, tm, body, 0, unroll=4)

    @pl.when(i == 0)
    def _():
        issue(0, 0)

    @pl.when(i + 1 < nt)
    def _():
        issue(i + 1, (i + 1) % 2)

    slot = i % 2
    for j in range(TOP_K):
        pltpu.make_async_copy(ys_ref.at[pl.ds(0, tm)], ybuf.at[slot, j], sems.at[slot, j]).wait()

    def run(out_ref):
        def slab(s, carry):
            r = pl.multiple_of(s * SUBLANES, SUBLANES)
            rows = pl.ds(r, SUBLANES)
            wts = rw_ref[rows, :]
            ff = wts[:, 0:1] * ybuf[slot, 0, rows, :] + wts[:, 1:2] * ybuf[slot, 1, rows, :]
            g2 = jnp.broadcast_to(g2_ref[pl.ds(_batch_of(base + r, tok), 1), :], (SUBLANES, d))
            out_ref[rows, :] = _rms(x1_ref[rows, :] + g2 * ff, fw_ref[...])
            return carry
        lax.fori_loop(0, tm // SUBLANES, slab, 0, unroll=4)

    npt = tok.n_prompt // tm
    pl.when(i < npt)(lambda: run(yp_ref))
    pl.when(i >= npt)(lambda: run(ysm_ref))


def _combine(dest_flat, ys, x1, rw, mod, final_w, tok, tiles):
    d = x1.shape[1]
    g = mod.shape[0]
    tm = tiles.comb_tm
    npt = tok.n_prompt // tm
    nst = tok.n_sample // tm
    vmem = 2 * (tm * d * 4 + tm * LANES * 4 + g * d * 4 + 2 * tm * d * 4) + 2 * TOP_K * tm * d * 4
    return pl.pallas_call(
        functools.partial(_combine_kernel, tok=tok, tm=tm),
        grid_spec=pltpu.PrefetchScalarGridSpec(
            num_scalar_prefetch=1,
            grid=(tok.n // tm,),
            in_specs=[pl.BlockSpec(memory_space=pl.ANY),
                      pl.BlockSpec((tm, d), lambda i, dest: (i, 0)),
                      pl.BlockSpec((tm, LANES), lambda i, dest: (i, 0)),
                      pl.BlockSpec((g, d), lambda i, dest: (0, 5)),
                      pl.BlockSpec((1, d), lambda i, dest: (0, 0))],
            out_specs=[pl.BlockSpec((tm, d), lambda i, dest: (jnp.minimum(i, npt - 1), 0)),
                       pl.BlockSpec((tm, d), lambda i, dest: (jnp.clip(i - npt, 0, nst - 1), 0))],
            scratch_shapes=[pltpu.VMEM((2, TOP_K, tm, d), F32), pltpu.SemaphoreType.DMA((2, TOP_K))]),
        out_shape=[jax.ShapeDtypeStruct((tok.n_prompt, d), F32),
                   jax.ShapeDtypeStruct((tok.n_sample, d), F32)],
        compiler_params=_cparams(("arbitrary",), vmem),
        name="combine",
    )(dest_flat, ys, x1, rw, mod, final_w.reshape(1, d))


def _gate_layouts(gates, n_rows, row_off, L, n_gate):
    g = gates[row_off:row_off + n_rows, :n_gate]
    return g.reshape(n_rows // L, L, n_gate).transpose(0, 2, 1)


def _layer(xp, xs, c_all, pool_s, c_s, n_s, m_s, lw, tok, tiles):
    (w_ada, b_ada, norm_mix_w, w_in, b_igate, b_fgate, w_pool, pool_scale, mlstm_norm_w, w_out, norm_ffn_w,
     w_router_group, w_router_expert, w_exp_gate, w_exp_up, w_exp_down) = lw
    d = xp.shape[1]
    nheads, dk, dv = c_s.shape[1:]
    pool_in = pool_s.shape[2]
    mqk, mv = nheads * dk, nheads * dv
    offs = (pool_in, pool_in + mqk, pool_in + 2 * mqk, pool_in + 2 * mqk + mv)
    ga_off = offs[3] + mv
    gb_off = ga_off + d
    n_main = gb_off + d
    n_gate = 2 * nheads
    n = tok.n

    mod = _ada(c_all, w_ada, b_ada, tiles)
    w_in_t = w_in.T
    u, gates = _prenorm(xp, xs, mod, norm_mix_w, w_in_t, n_main, n_gate, tok, tiles)
    z = _matmul(u, w_in_t, n_main, True, tiles)

    zeros_pool = jnp.zeros((tok.nb_p, POOL_BUF, pool_in), F32)
    d_p, pool_new_p = _pool(z, zeros_pool, 0, tok.nb_p, tok.rows_p, 1, tiles.pool_tt, 0)
    d_s, pool_new_s = _pool(z, pool_s, tok.n_prompt, tok.nb_s, tok.rows_s, tiles.pool_bt, tok.rows_s, PAST_LEN)

    bias = jnp.concatenate([b_igate, b_fgate])
    bias_row = jnp.pad(bias, (0, LANES - n_gate)).reshape(1, LANES)
    bias_col = bias.reshape(n_gate, 1)
    nw = mlstm_norm_w.reshape(1, mv)
    lp = tiles.mlstm_l
    zc = jnp.zeros((tok.nb_p, nheads, dk, dv), F32)
    zn = jnp.zeros((tok.nb_p, nheads, 1, dk), F32)
    zm = jnp.zeros((tok.nb_p, nheads, 1, 1), F32)
    mo_p, c_p, n_p, m_p = _mlstm(z, gates, _gate_layouts(gates, tok.n_prompt, 0, lp, n_gate), bias_row, bias_col,
                                 nw, zc, zn, zm, 0, tok.nb_p, tok.rows_p, lp, 1, tiles.mlstm_hb_p, offs)
    ls = tok.rows_s
    mo_s, c_n, n_n, m_n = _mlstm(z, gates, _gate_layouts(gates, tok.n_sample, tok.n_prompt, ls, n_gate), bias_row,
                                 bias_col, nw, c_s, n_s.reshape(tok.nb_s, nheads, 1, dk),
                                 m_s.reshape(tok.nb_s, nheads, 1, 1), tok.n_prompt, tok.nb_s, ls, ls,
                                 tiles.mlstm_bb_s, tiles.mlstm_hb_s, offs)

    merged = _merge(d_p, d_s, w_pool.astype(BF16), pool_scale, z, mo_p, mo_s, ga_off, gb_off, tok, tiles)
    acc = _matmul(merged, w_out, d, False, tiles)

    w_router = jnp.pad(jnp.concatenate([w_router_group, w_router_expert], axis=1),
                       ((0, 0), (0, LANES - N_GROUPS - N_EXPERTS))).astype(BF16)
    x1, u2p, re, rw = _ffn_norm(xp, xs, acc, mod, norm_ffn_w, w_router, tok, tiles)
    rank_out, cnt_out = _rank(re, tiles)
    d_expert = w_exp_gate.shape[2]
    n_blocks = -(-(n * TOP_K) // tiles.moe_bm) + N_EXPERTS
    plan = _dispatch_plan(re, rank_out, cnt_out, n, n_blocks, d_expert, d, tiles)
    inv = _invert(plan["dest"], n_blocks * tiles.moe_bm)
    xsort = _gather(inv, plan["n_used"], plan["n_valid"], u2p, tiles)
    hdn = _moe_up(plan, xsort, w_exp_gate, w_exp_up, tiles)
    ysort = _moe_down(plan, hdn, w_exp_down, tiles)
    states = (pool_new_p, c_p, n_p.reshape(tok.nb_p, nheads, dk), m_p.reshape(tok.nb_p, nheads),
              pool_new_s, c_n, n_n.reshape(tok.nb_s, nheads, dk), m_n.reshape(tok.nb_s, nheads))
    return (plan["dest"], ysort, x1, rw, mod), states


def _forward(x_prompt, x_sample, c_prompt, c_sample, state_pool, state_mlstm_C, state_mlstm_n, state_mlstm_m,
             w_ada, b_ada, norm_mix_w, w_in, b_igate, b_fgate, w_pool, pool_scale, mlstm_norm_w, w_out,
             norm_ffn_w, w_router_group, w_router_expert, w_exp_gate, w_exp_up, w_exp_down, final_norm_w,
             tiles=Tiles()):
    nb_p, rows_p, d = x_prompt.shape
    nb_s, rows_s, _ = x_sample.shape
    depth = w_ada.shape[0]
    assert depth == 1, "the merged-token pipeline is written for a single layer"
    tok = Tok(nb_p * rows_p, nb_s * rows_s, rows_p, rows_s, nb_p, nb_s)
    for tm in (tiles.row_tm, tiles.mm_tm, tiles.merge_tm, tiles.rank_tr, tiles.comb_tm):
        assert tok.n_prompt % tm == 0 and tok.n_sample % tm == 0
    xp = x_prompt.reshape(tok.n_prompt, d)
    xs = x_sample.reshape(tok.n_sample, d)
    g = nb_p + nb_s
    g_pad = -(-g // SUBLANES) * SUBLANES
    c_all = jnp.pad(jnp.concatenate([c_prompt, c_sample], axis=0), ((0, g_pad - g), (0, 0)))
    lw = (w_ada[0], b_ada[0], norm_mix_w[0], w_in[0], b_igate[0], b_fgate[0], w_pool[0], pool_scale[0],
          mlstm_norm_w[0], w_out[0], norm_ffn_w[0], w_router_group[0], w_router_expert[0], w_exp_gate[0],
          w_exp_up[0], w_exp_down[0])
    (dest, ysort, x1, rw, mod), st = _layer(xp, xs, c_all, state_pool[0], state_mlstm_C[0], state_mlstm_n[0],
                                            state_mlstm_m[0], lw, tok, tiles)
    y_p, y_s = _combine(dest, ysort, x1, rw, mod, final_norm_w, tok, tiles)
    return (y_p.reshape(x_prompt.shape), y_s.reshape(x_sample.shape)) + tuple(s[None] for s in st)


def kernel(x_prompt, x_sample, c_prompt, c_sample, state_pool, state_mlstm_C, state_mlstm_n, state_mlstm_m,
           w_ada, b_ada, norm_mix_w, w_in, b_igate, b_fgate, w_pool, pool_scale, mlstm_norm_w, w_out,
           norm_ffn_w, w_router_group, w_router_expert, w_exp_gate, w_exp_up, w_exp_down, final_norm_w):
    return _forward(x_prompt, x_sample, c_prompt, c_sample, state_pool, state_mlstm_C, state_mlstm_n,
                    state_mlstm_m, w_ada, b_ada, norm_mix_w, w_in, b_igate, b_fgate, w_pool, pool_scale,
                    mlstm_norm_w, w_out, norm_ffn_w, w_router_group, w_router_expert, w_exp_gate, w_exp_up,
                    w_exp_down, final_norm_w)
```
